```python
import jax
import jax.numpy as jnp
from jax import lax
import numpy as np

D_MODEL = 1024
BATCH = 4
SEQ = 8192
DEPTH = 2

D_MIX = D_MODEL
D_FF = 4 * D_MODEL
NORM_EPS = 1e-5
SHORT_CONV = 4

ATT_HEAD_DIM = 64
ATT_Q_HEADS = (D_MIX // 2) // ATT_HEAD_DIM
ATT_KV_HEADS = ATT_Q_HEADS // 4
ATT_WINDOW = 128
ROPE_THETA = 500000.0
ROPE_DIM = ATT_HEAD_DIM // 4

ML_HEADS = 4
ML_V_DIM = (D_MIX // 2) // ML_HEADS
ML_QK_DIM = ML_V_DIM // 2
ML_CHUNK = 128
ML_NORM_EPS = 1e-6

SSM_HEAD_DIM = 64
SSM_HEADS = (D_MIX // 2) // SSM_HEAD_DIM
SSM_GROUPS = 2
SSM_STATE = 128
SSM_CHUNK = 128

RW_HEAD_DIM = 64
RW_HEADS = (D_MIX // 2) // RW_HEAD_DIM
RW_DECAY_LORA = 64
RW_AAA_LORA = 64
RW_GATE_LORA = 128
RW_LN_EPS = 64e-5

N_EVEN = (DEPTH + 1) // 2
N_ODD = DEPTH // 2

ATT_Q_W = ATT_Q_HEADS * ATT_HEAD_DIM
ATT_KV_W = ATT_KV_HEADS * ATT_HEAD_DIM
ML_QK_W = ML_HEADS * ML_QK_DIM
ML_V_W = ML_HEADS * ML_V_DIM
EVEN_IN = ATT_Q_W + 2 * ATT_KV_W + 2 * ML_QK_W + 2 * ML_V_W + 2 * ML_HEADS
EVEN_OUT = ATT_Q_W + ML_V_W

SSM_INNER = SSM_HEADS * SSM_HEAD_DIM
SSM_BC = SSM_GROUPS * SSM_STATE
SSM_CONV_W = SSM_INNER + 2 * SSM_BC
RW_W = RW_HEADS * RW_HEAD_DIM
RW_IN = 3 * RW_W + RW_DECAY_LORA + RW_AAA_LORA + RW_GATE_LORA
ODD_IN = SSM_INNER + SSM_CONV_W + SSM_HEADS + RW_IN
ODD_OUT = SSM_INNER + RW_W

F32 = jnp.float32

kernel_name = 'hybrid_swa_mlstm_ssd_rwkv7_trunk'


def rmsnorm(x, g):
    xf = x.astype(F32)
    y = xf * lax.rsqrt(jnp.mean(xf * xf, axis=-1, keepdims=True) + NORM_EPS)
    return (y * g.astype(F32)).astype(x.dtype)


def group_rmsnorm(x, g, groups):
    shp = x.shape
    xf = x.astype(F32).reshape(shp[:-1] + (groups, shp[-1] // groups))
    y = xf * lax.rsqrt(jnp.mean(xf * xf, axis=-1, keepdims=True) + NORM_EPS)
    return y.reshape(shp) * g.astype(F32)


def head_layernorm(x, eps):
    xf = x.astype(F32)
    mu = jnp.mean(xf, axis=-1, keepdims=True)
    var = jnp.mean(jnp.square(xf - mu), axis=-1, keepdims=True)
    return (xf - mu) * lax.rsqrt(var + eps)


def split_cols(p, sizes):
    idx = np.cumsum(np.array(sizes))[:-1].tolist()
    return jnp.split(p, idx, axis=-1)


def causal_conv(x, w, b):
    K = w.shape[0]
    T = x.shape[1]
    xp = jnp.pad(x, ((0, 0), (K - 1, 0), (0, 0)))
    out = b
    for tap in range(K):
        out = out + xp[:, tap:tap + T, :] * w[tap]
    return out


def token_shift(x):
    return jnp.pad(x, ((0, 0), (1, 0), (0, 0)))[:, :-1, :]


def partial_rope(x):
    T = x.shape[1]
    half = ROPE_DIM // 2
    inv_freq = ROPE_THETA ** (-jnp.arange(half, dtype=F32) * 2.0 / ROPE_DIM)
    ang = jnp.arange(T, dtype=F32)[:, None] * inv_freq[None, :]
    cos = jnp.cos(ang)[None, :, None, :]
    sin = jnp.sin(ang)[None, :, None, :]
    xf = x.astype(F32)
    x1 = xf[..., :half]
    x2 = xf[..., half:ROPE_DIM]
    rot = jnp.concatenate([x1 * cos - x2 * sin, x2 * cos + x1 * sin, xf[..., ROPE_DIM:]], axis=-1)
    return rot.astype(x.dtype)


def sliding_window_sink_attention(q, k, v, sinks):
    bsz, T, hq, dh = q.shape
    hkv = k.shape[2]
    grp = hq // hkv
    W = ATT_WINDOW
    nb = T // W
    qb = q.reshape(bsz, nb, W, hkv, grp, dh)

    def band(t):
        tp = jnp.pad(t, ((0, 0), (W, 0), (0, 0), (0, 0)))
        prev = tp[:, :T].reshape(bsz, nb, W, hkv, dh)
        cur = t.reshape(bsz, nb, W, hkv, dh)
        return jnp.concatenate([prev, cur], axis=2)

    kb = band(k)
    vb = band(v)
    s = jnp.einsum('bnqhgd,bnkhd->bhgnqk', qb, kb).astype(F32) * (dh ** -0.5)
    i = jnp.arange(W)[:, None]
    j = jnp.arange(2 * W)[None, :]
    blk = jnp.arange(nb)[:, None, None]
    mask = (j > i) & (j <= i + W) & (blk * W - W + j >= 0)
    s = jnp.where(mask, s, -jnp.inf)
    sink = sinks.astype(F32).reshape(hkv, grp, 1, 1, 1)
    m = jnp.maximum(jnp.max(s, axis=-1, keepdims=True), sink)
    p = jnp.exp(s - m)
    p = p / (jnp.sum(p, axis=-1, keepdims=True) + jnp.exp(sink - m))
    o = jnp.einsum('bhgnqk,bnkhd->bnqhgd', p.astype(v.dtype), vb)
    return o.reshape(bsz, T, hq * dh)


def mlstm_chunkwise(q, k, v, log_i, log_f):
    bsz, T, H, dk = q.shape
    dv = v.shape[-1]
    L = ML_CHUNK
    nc = T // L
    qc = q.astype(F32).reshape(bsz, nc, L, H, dk)
    kc = k.astype(F32).reshape(bsz, nc, L, H, dk) * (dk ** -0.5)
    vc = v.astype(F32).reshape(bsz, nc, L, H, dv)
    li = log_i.reshape(bsz, nc, L, H)
    b = jnp.cumsum(log_f.reshape(bsz, nc, L, H), axis=2)
    causal = jnp.tril(jnp.ones((L, L), dtype=bool))
    dlog = b[:, :, :, None, :] - b[:, :, None, :, :] + li[:, :, None, :, :]
    dlog = jnp.where(causal[None, None, :, :, None], dlog, -jnp.inf)
    bL = b[:, :, -1, :]
    a = bL[:, :, None, :] - b + li
    m_loc = jnp.max(a, axis=2)
    wgt = jnp.exp(a - m_loc[:, :, None, :])
    C_loc = jnp.einsum('bnlh,bnlhk,bnlhv->bnhkv', wgt, kc, vc)
    n_loc = jnp.einsum('bnlh,bnlhk->bnhk', wgt, kc)

    def step(carry, inp):
        C, n, m = carry
        Cl, nl, ml, gl = inp
        m_new = jnp.maximum(gl + m, ml)
        fa = jnp.exp(gl + m - m_new)
        fb = jnp.exp(ml - m_new)
        C_new = fa[..., None, None] * C + fb[..., None, None] * Cl
        n_new = fa[..., None] * n + fb[..., None] * nl
        return (C_new, n_new, m_new), (C, n, m)

    init = (jnp.zeros((bsz, H, dk, dv), F32), jnp.zeros((bsz, H, dk), F32), jnp.zeros((bsz, H), F32))
    xs = (jnp.moveaxis(C_loc, 1, 0), jnp.moveaxis(n_loc, 1, 0), jnp.moveaxis(m_loc, 1, 0), jnp.moveaxis(bL, 1, 0))
    _, (C_prev, n_prev, m_prev) = lax.scan(step, init, xs)
    C_prev = jnp.moveaxis(C_prev, 0, 1)
    n_prev = jnp.moveaxis(n_prev, 0, 1)
    m_prev = jnp.moveaxis(m_prev, 0, 1)
    inter_log = b + m_prev[:, :, None, :]
    m_t = jnp.maximum(inter_log, jnp.max(dlog, axis=3))
    w_intra = jnp.exp(dlog - m_t[:, :, :, None, :])
    w_inter = jnp.exp(inter_log - m_t)
    qk = jnp.einsum('bnthd,bnshd->bntsh', qc, kc) * w_intra
    num = jnp.einsum('bntsh,bnshv->bnthv', qk, vc) + w_inter[..., None] * jnp.einsum('bnthk,bnhkv->bnthv', qc, C_prev)
    den = jnp.sum(qk, axis=3) + w_inter * jnp.einsum('bnthk,bnhk->bnth', qc, n_prev)
    h = num / jnp.maximum(jnp.abs(den), jnp.exp(-m_t))[..., None]
    return h.reshape(bsz, T, H, dv)


def ssd_chunked(x, dt, A, Bm, Cm):
    bsz, T, H, P = x.shape
    G = Bm.shape[2]
    N = Bm.shape[-1]
    hg = H // G
    L = SSM_CHUNK
    nc = T // L
    xc = (x.astype(F32) * dt[..., None]).reshape(bsz, nc, L, G, hg, P)
    a_cum = jnp.cumsum((dt * A.astype(F32)).reshape(bsz, nc, L, G, hg), axis=2)
    Bc = Bm.astype(F32).reshape(bsz, nc, L, G, N)
    Cc = Cm.astype(F32).reshape(bsz, nc, L, G, N)
    causal = jnp.tril(jnp.ones((L, L), dtype=bool))[:, :, None, None]
    seg = a_cum[:, :, :, None] - a_cum[:, :, None, :]
    decay = jnp.exp(jnp.where(causal, seg, -jnp.inf))
    cb = jnp.einsum('bctgn,bcsgn->bctsg', Cc, Bc)
    y_diag = jnp.einsum('bctsgh,bcsghp->bctghp', cb[..., None] * decay, xc)
    decay_s = jnp.exp(a_cum[:, :, -1:] - a_cum)
    states = jnp.einsum('bclgn,bclgh,bclghp->bcghpn', Bc, decay_s, xc)
    chunk_decay = jnp.exp(a_cum[:, :, -1])

    def step(S, inp):
        st, dec = inp
        return S * dec[..., None, None] + st, S

    S0 = jnp.zeros((bsz, G, hg, P, N), F32)
    _, S_prev = lax.scan(step, S0, (jnp.moveaxis(states, 1, 0), jnp.moveaxis(chunk_decay, 1, 0)))
    S_prev = jnp.moveaxis(S_prev, 0, 1)
    y_off = jnp.einsum('bctgn,bcghpn->bctghp', Cc, S_prev) * jnp.exp(a_cum)[..., None]
    return (y_diag + y_off).reshape(bsz, T, H, P)


def rwkv7_scan(r, w, k, v, kk, kka):
    bsz, T, H, dh = r.shape

    def step(S, inp):
        r_t, w_t, k_t, v_t, kk_t, b_t = inp
        sa = jnp.einsum('bhvk,bhk->bhv', S, kk_t)
        S = S * w_t[:, :, None, :] - sa[..., None] * b_t[:, :, None, :] + v_t[..., None] * k_t[:, :, None, :]
        return S, jnp.einsum('bhvk,bhk->bhv', S, r_t)

    xs = (jnp.moveaxis(r, 1, 0), jnp.moveaxis(w, 1, 0), jnp.moveaxis(k, 1, 0), jnp.moveaxis(v, 1, 0), jnp.moveaxis(kk, 1, 0), jnp.moveaxis(kka, 1, 0))
    S0 = jnp.zeros((bsz, H, dh, dh), F32)
    _, y = lax.scan(step, S0, xs)
    return jnp.moveaxis(y, 0, 1)


def even_mixer(h, w_in, w_out, sinks, conv_w, conv_b, i_bias, f_bias, ml_norm):
    bsz, T, _ = h.shape
    p = h @ w_in
    q_a, k_a, v_a, qk_m, v_m, o_m, i_m, f_m = split_cols(p, (ATT_Q_W, ATT_KV_W, ATT_KV_W, 2 * ML_QK_W, ML_V_W, ML_V_W, ML_HEADS, ML_HEADS))
    q = partial_rope(q_a.reshape(bsz, T, ATT_Q_HEADS, ATT_HEAD_DIM))
    k = partial_rope(k_a.reshape(bsz, T, ATT_KV_HEADS, ATT_HEAD_DIM))
    v = v_a.reshape(bsz, T, ATT_KV_HEADS, ATT_HEAD_DIM)
    y_att = sliding_window_sink_attention(q, k, v, sinks)
    qk = jax.nn.silu(causal_conv(qk_m, conv_w, conv_b))
    q_m, k_m = jnp.split(qk, 2, axis=-1)
    log_f = jax.nn.log_sigmoid(f_m.astype(F32) + f_bias.astype(F32))
    log_i = i_m.astype(F32) + i_bias.astype(F32)
    hm = mlstm_chunkwise(q_m.reshape(bsz, T, ML_HEADS, ML_QK_DIM), k_m.reshape(bsz, T, ML_HEADS, ML_QK_DIM), v_m.reshape(bsz, T, ML_HEADS, ML_V_DIM), log_i, log_f)
    hm = head_layernorm(hm, ML_NORM_EPS).reshape(bsz, T, ML_V_W) * ml_norm.astype(F32)
    y_ml = (hm * jax.nn.sigmoid(o_m.astype(F32))).astype(h.dtype)
    return jnp.concatenate([y_att, y_ml], axis=-1) @ w_out


def odd_mixer(h, w_in, w_out, ssm_conv_w, ssm_conv_b, ssm_dt_bias, ssm_a_log, ssm_d, ssm_norm,
              rw_mu, rw_w0, rw_w2, rw_a0, rw_a2, rw_g2, rw_k_k, rw_k_a, rw_r_k, rw_ln_w, rw_ln_b):
    bsz, T, _ = h.shape
    p = h @ w_in
    z, xbc, dt_raw, rw = split_cols(p, (SSM_INNER, SSM_CONV_W, SSM_HEADS, RW_IN))
    xbc = jax.nn.silu(causal_conv(xbc, ssm_conv_w, ssm_conv_b))
    xs, Bm, Cm = split_cols(xbc, (SSM_INNER, SSM_BC, SSM_BC))
    dt = jax.nn.softplus(dt_raw.astype(F32) + ssm_dt_bias.astype(F32))
    A = -jnp.exp(ssm_a_log.astype(F32))
    xh = xs.reshape(bsz, T, SSM_HEADS, SSM_HEAD_DIM)
    y = ssd_chunked(xh, dt, A, Bm.reshape(bsz, T, SSM_GROUPS, SSM_STATE), Cm.reshape(bsz, T, SSM_GROUPS, SSM_STATE))
    y = y + ssm_d.astype(F32)[None, None, :, None] * xh.astype(F32)
    y = y.reshape(bsz, T, SSM_INNER) * jax.nn.silu(z.astype(F32))
    y_ssm = group_rmsnorm(y, ssm_norm, SSM_GROUPS).astype(h.dtype)
    rw = rw + (token_shift(rw) - rw) * rw_mu
    r, k, v, wd, ad, gd = split_cols(rw, (RW_W, RW_W, RW_W, RW_DECAY_LORA, RW_AAA_LORA, RW_GATE_LORA))
    w_log = -jax.nn.softplus(-(rw_w0 + jnp.tanh(wd) @ rw_w2).astype(F32)) - 0.5
    decay = jnp.exp(-jnp.exp(w_log))
    a = jax.nn.sigmoid((rw_a0 + ad @ rw_a2).astype(F32))
    g = (jax.nn.sigmoid(gd) @ rw_g2).astype(F32)
    hd = (bsz, T, RW_HEADS, RW_HEAD_DIM)
    rh = r.astype(F32).reshape(hd)
    kh = k.astype(F32).reshape(hd)
    vh = v.astype(F32).reshape(hd)
    ah = a.reshape(hd)
    kk = kh * rw_k_k.astype(F32).reshape(RW_HEADS, RW_HEAD_DIM)
    kk = kk / jnp.maximum(jnp.sqrt(jnp.sum(kk * kk, axis=-1, keepdims=True)), 1e-12)
    kh = kh * (1.0 + (ah - 1.0) * rw_k_a.astype(F32).reshape(RW_HEADS, RW_HEAD_DIM))
    yr = rwkv7_scan(rh, decay.reshape(hd), kh, vh, kk, kk * ah)
    yr = head_layernorm(yr, RW_LN_EPS).reshape(bsz, T, RW_W) * rw_ln_w.astype(F32) + rw_ln_b.astype(F32)
    bonus = (jnp.sum(rh * kh * rw_r_k.astype(F32), axis=-1, keepdims=True) * vh).reshape(bsz, T, RW_W)
    y_rw = ((yr + bonus) * g).astype(h.dtype)
    return jnp.concatenate([y_ssm, y_rw], axis=-1) @ w_out


def squared_relu_mlp(h, w_up, w_down):
    return jnp.square(jax.nn.relu(h @ w_up)) @ w_down


def setup_inputs(seed: int = 0) -> dict:
    key = jax.random.key(seed)
    keys = iter(jax.random.split(key, 48))

    def normal(shape, scale):
        return jax.random.normal(next(keys), shape, F32) * scale

    def gain(shape):
        return 1.0 + normal(shape, 0.05)

    def unif(shape, lo, hi):
        return jax.random.uniform(next(keys), shape, F32, lo, hi)

    E, O, L = N_EVEN, N_ODD, DEPTH
    x = normal((BATCH, SEQ, D_MODEL), 1.0)
    mix_norm = gain((L, D_MODEL))
    mlp_norm = gain((L, D_MODEL))
    w_up = normal((L, D_MODEL, D_FF), D_MODEL ** -0.5)
    w_down = normal((L, D_FF, D_MODEL), D_FF ** -0.5)
    e_w_in = normal((E, D_MODEL, EVEN_IN), D_MODEL ** -0.5)
    e_w_out = normal((E, EVEN_OUT, D_MODEL), EVEN_OUT ** -0.5)
    att_sinks = normal((E, ATT_Q_HEADS), 1.0)
    ml_conv_w = normal((E, SHORT_CONV, 2 * ML_QK_W), SHORT_CONV ** -0.5)
    ml_conv_b = normal((E, 2 * ML_QK_W), 0.02)
    ml_i_bias = normal((E, ML_HEADS), 0.1)
    ml_f_bias = jnp.linspace(3.0, 6.0, ML_HEADS)[None, :] + normal((E, ML_HEADS), 0.1)
    ml_norm = gain((E, ML_V_W))
    o_w_in = normal((O, D_MODEL, ODD_IN), D_MODEL ** -0.5)
    o_w_out = normal((O, ODD_OUT, D_MODEL), ODD_OUT ** -0.5)
    ssm_conv_w = normal((O, SHORT_CONV, SSM_CONV_W), SHORT_CONV ** -0.5)
    ssm_conv_b = normal((O, SSM_CONV_W), 0.02)
    dt0 = jnp.exp(unif((O, SSM_HEADS), float(np.log(1e-3)), float(np.log(1e-1))))
    ssm_dt_bias = dt0 + jnp.log(-jnp.expm1(-dt0))
    ssm_a_log = jnp.log(unif((O, SSM_HEADS), 1.0, 16.0))
    ssm_d = gain((O, SSM_HEADS))
    ssm_norm = gain((O, SSM_INNER))
    rw_mu = unif((O, RW_IN), 0.0, 1.0)
    rw_w0 = jnp.tile(jnp.linspace(-6.0, -0.5, RW_HEAD_DIM), RW_HEADS)[None, :] + normal((O, RW_W), 0.1)
    rw_w2 = normal((O, RW_DECAY_LORA, RW_W), 0.1)
    rw_a0 = normal((O, RW_W), 0.1)
    rw_a2 = normal((O, RW_AAA_LORA, RW_W), 0.1)
    rw_g2 = normal((O, RW_GATE_LORA, RW_W), RW_GATE_LORA ** -0.5)
    rw_k_k = 0.85 + normal((O, RW_W), 0.05)
    rw_k_a = gain((O, RW_W))
    rw_r_k = normal((O, RW_HEADS, RW_HEAD_DIM), 0.1)
    rw_ln_w = gain((O, RW_W))
    rw_ln_b = normal((O, RW_W), 0.02)
    final_norm = gain((D_MODEL,))
    return {'x': x, 'mix_norm': mix_norm, 'mlp_norm': mlp_norm, 'w_up': w_up, 'w_down': w_down,
            'e_w_in': e_w_in, 'e_w_out': e_w_out, 'att_sinks': att_sinks, 'ml_conv_w': ml_conv_w,
            'ml_conv_b': ml_conv_b, 'ml_i_bias': ml_i_bias, 'ml_f_bias': ml_f_bias, 'ml_norm': ml_norm,
            'o_w_in': o_w_in, 'o_w_out': o_w_out, 'ssm_conv_w': ssm_conv_w, 'ssm_conv_b': ssm_conv_b,
            'ssm_dt_bias': ssm_dt_bias, 'ssm_a_log': ssm_a_log, 'ssm_d': ssm_d, 'ssm_norm': ssm_norm,
            'rw_mu': rw_mu, 'rw_w0': rw_w0, 'rw_w2': rw_w2, 'rw_a0': rw_a0, 'rw_a2': rw_a2, 'rw_g2': rw_g2,
            'rw_k_k': rw_k_k, 'rw_k_a': rw_k_a, 'rw_r_k': rw_r_k, 'rw_ln_w': rw_ln_w, 'rw_ln_b': rw_ln_b,
            'final_norm': final_norm}


def reference(x, mix_norm, mlp_norm, w_up, w_down, e_w_in, e_w_out, att_sinks, ml_conv_w, ml_conv_b,
              ml_i_bias, ml_f_bias, ml_norm, o_w_in, o_w_out, ssm_conv_w, ssm_conv_b, ssm_dt_bias,
              ssm_a_log, ssm_d, ssm_norm, rw_mu, rw_w0, rw_w2, rw_a0, rw_a2, rw_g2, rw_k_k, rw_k_a,
              rw_r_k, rw_ln_w, rw_ln_b, final_norm):
    h = x
    for layer in range(DEPTH):
        hn = rmsnorm(h, mix_norm[layer])
        if layer % 2 == 0:
            e = layer // 2
            h = h + even_mixer(hn, e_w_in[e], e_w_out[e], att_sinks[e], ml_conv_w[e], ml_conv_b[e],
                               ml_i_bias[e], ml_f_bias[e], ml_norm[e])
        else:
            o = layer // 2
            h = h + odd_mixer(hn, o_w_in[o], o_w_out[o], ssm_conv_w[o], ssm_conv_b[o], ssm_dt_bias[o],
                              ssm_a_log[o], ssm_d[o], ssm_norm[o], rw_mu[o], rw_w0[o], rw_w2[o],
                              rw_a0[o], rw_a2[o], rw_g2[o], rw_k_k[o], rw_k_a[o], rw_r_k[o],
                              rw_ln_w[o], rw_ln_b[o])
        h = h + squared_relu_mlp(rmsnorm(h, mlp_norm[layer]), w_up[layer], w_down[layer])
    return rmsnorm(h, final_norm)
```

```python
import functools

import jax
import jax.numpy as jnp
from jax import lax
from jax.experimental import pallas as pl
from jax.experimental.pallas import tpu as pltpu

F32 = jnp.float32
BF16 = jnp.bfloat16
HI = lax.Precision.HIGHEST

D_MODEL = 1024
D_FF = 4 * D_MODEL
NORM_EPS = 1e-5
CONV_TAPS = 4

ATT_HEAD_DIM = 64
ATT_Q_HEADS = 8
ATT_KV_HEADS = 2
ATT_GROUP = ATT_Q_HEADS // ATT_KV_HEADS
ATT_WINDOW = 128
ROPE_THETA = 500000.0
ROPE_DIM = 16
ATT_Q_W = ATT_Q_HEADS * ATT_HEAD_DIM
ATT_KV_W = ATT_KV_HEADS * ATT_HEAD_DIM

ML_HEADS = 4
ML_V_DIM = 128
ML_QK_DIM = 64
ML_CHUNK = 128
ML_NORM_EPS = 1e-6
ML_QK_W = ML_HEADS * ML_QK_DIM
ML_V_W = ML_HEADS * ML_V_DIM

SSM_HEAD_DIM = 64
SSM_HEADS = 8
SSM_GROUPS = 2
SSM_HEADS_PER_GROUP = SSM_HEADS // SSM_GROUPS
SSM_STATE = 128
SSM_CHUNK = 128
SSM_INNER = SSM_HEADS * SSM_HEAD_DIM
SSM_BC = SSM_GROUPS * SSM_STATE
SSM_CONV_W = SSM_INNER + 2 * SSM_BC

RW_HEAD_DIM = 64
RW_HEADS = 8
RW_W = RW_HEADS * RW_HEAD_DIM
RW_DECAY_LORA = 64
RW_AAA_LORA = 64
RW_GATE_LORA = 128
RW_IN = 3 * RW_W + RW_DECAY_LORA + RW_AAA_LORA + RW_GATE_LORA
RW_LN_EPS = 64e-5
RW_CHUNK = 64

V7X_VMEM_BYTES = 64 * 1024 * 1024
VMEM_LIMIT_BYTES = V7X_VMEM_BYTES - 16 * 1024 * 1024
SUBLANES = 8
LANES = 128

PROJ_ROWS = 512
FF_CHUNK = 512


def _params(*sem):
    return pltpu.CompilerParams(dimension_semantics=sem, vmem_limit_bytes=VMEM_LIMIT_BYTES)


def _const_spec(shape):
    nd = len(shape)
    return pl.BlockSpec(shape, lambda *_: (0,) * nd, pipeline_mode=pl.Buffered(1))


def _iota(shape, dim):
    return lax.broadcasted_iota(jnp.int32, shape, dim)


def _tri_lower(n):
    return (_iota((n, n), 1) <= _iota((n, n), 0)).astype(F32)


def _tri_upper(n):
    return (_iota((n, n), 0) <= _iota((n, n), 1)).astype(F32)


def _eye(n):
    return (_iota((n, n), 0) == _iota((n, n), 1)).astype(F32)


def _dot(a, b, precision=None):
    return jnp.dot(a, b, preferred_element_type=F32, precision=precision)


def _dot_nt(a, b, precision=None):
    return lax.dot_general(a, b, (((1,), (1,)), ((), ())), preferred_element_type=F32, precision=precision)


def _sigmoid(x):
    return 1.0 / (1.0 + jnp.exp(-x))


def _softplus(x):
    return jnp.maximum(x, 0.0) + jnp.log1p(jnp.exp(-jnp.abs(x)))


def _silu(x):
    return x * _sigmoid(x)


def _rmsnorm(x, g):
    y = x * lax.rsqrt(jnp.mean(x * x, axis=-1, keepdims=True) + NORM_EPS)
    return y * g


def _shifted_rows(x, tail, k):
    rolled = pltpu.roll(x, k, axis=0)
    tail_rolled = pltpu.roll(tail, k, axis=0)
    first = jnp.where(_iota((SUBLANES, 1), 0) < k, tail_rolled, rolled[:SUBLANES])
    return jnp.concatenate([first, rolled[SUBLANES:]], axis=0)


def _causal_conv_silu(x, tail, w, b):
    out = b + x * w[CONV_TAPS - 1:CONV_TAPS, :]
    for k in range(1, CONV_TAPS):
        out = out + _shifted_rows(x, tail, k) * w[CONV_TAPS - 1 - k:CONV_TAPS - k, :]
    return _silu(out)


def _rope(x, cos, sin_lo, sin_hi, reps):
    w = x.shape[1]
    c = jnp.concatenate([cos] * reps, axis=1)
    s_lo = jnp.concatenate([sin_lo] * reps, axis=1)
    s_hi = jnp.concatenate([sin_hi] * reps, axis=1)
    half = ROPE_DIM // 2
    nxt = pltpu.roll(x, w - half, axis=1)
    prv = pltpu.roll(x, half, axis=1)
    return x * c + nxt * s_lo + prv * s_hi


def _inproj_even_kernel(x_ref, g_ref, cos_ref, slo_ref, shi_ref, wq_ref, wk_ref, wv_ref, wqk_ref, wvm_ref,
                        wom_ref, wgt_ref, qa_ref, ka_ref, va_ref, qkm_ref, vm_ref, om_ref, gt_ref):
    hb = _rmsnorm(x_ref[...], g_ref[...]).astype(BF16)
    cos, s_lo, s_hi = cos_ref[...], slo_ref[...], shi_ref[...]
    q = _rope(_dot(hb, wq_ref[...]), cos, s_lo, s_hi, ATT_Q_W // LANES)
    qa_ref[...] = (q * (ATT_HEAD_DIM ** -0.5)).astype(BF16)
    ka_ref[...] = _rope(_dot(hb, wk_ref[...]), cos, s_lo, s_hi, ATT_KV_W // LANES).astype(BF16)
    va_ref[...] = _dot(hb, wv_ref[...]).astype(BF16)
    qkm_ref[...] = _dot(hb, wqk_ref[...])
    vm_ref[...] = _dot(hb, wvm_ref[...]).astype(BF16)
    om_ref[...] = _dot(hb, wom_ref[...])
    gt_ref[...] = _dot_nt(wgt_ref[...], hb)


def _inproj_even(x2, g, tables, w, seq):
    n = x2.shape[0]
    tm = PROJ_ROWS
    per_seq = seq // tm
    row = lambda width: pl.BlockSpec((tm, width), lambda i: (i, 0))
    tab = pl.BlockSpec((tm, LANES), lambda i: (i % per_seq, 0))
    wq, wk, wv, wqk, wvm, wom, wgt = w
    out_shapes = (
        jax.ShapeDtypeStruct((n, ATT_Q_W), BF16), jax.ShapeDtypeStruct((n, ATT_KV_W), BF16),
        jax.ShapeDtypeStruct((n, ATT_KV_W), BF16), jax.ShapeDtypeStruct((n, 2 * ML_QK_W), F32),
        jax.ShapeDtypeStruct((n, ML_V_W), BF16), jax.ShapeDtypeStruct((n, ML_V_W), F32),
        jax.ShapeDtypeStruct((2 * ML_HEADS, n), F32))
    return pl.pallas_call(
        _inproj_even_kernel,
        grid=(n // tm,),
        in_specs=[row(D_MODEL), _const_spec((1, D_MODEL)), tab, tab, tab] + [_const_spec(a.shape) for a in w],
        out_specs=(row(ATT_Q_W), row(ATT_KV_W), row(ATT_KV_W), row(2 * ML_QK_W), row(ML_V_W), row(ML_V_W),
                   pl.BlockSpec((2 * ML_HEADS, tm), lambda i: (0, i))),
        out_shape=out_shapes,
        compiler_params=_params("arbitrary"),
        name="inproj_even",
    )(x2, g, *tables, wq, wk, wv, wqk, wvm, wom, wgt)


def _attention_kernel(q_ref, kc_ref, kp_ref, vc_ref, vp_ref, sink_ref, o_ref):
    blk = pl.program_id(1)
    w = ATT_WINDOW
    q = q_ref[...]
    k2 = jnp.concatenate([kp_ref[...], kc_ref[...]], axis=0)
    v2 = jnp.concatenate([vp_ref[...], vc_ref[...]], axis=0)
    i = _iota((w, 2 * w), 0)
    j = _iota((w, 2 * w), 1)
    mask = (j > i) & (j <= i + w) & ((j >= w) | (blk > 0))
    sinks = sink_ref[...]
    outs = []
    for h in range(ATT_Q_HEADS):
        kv = h // ATT_GROUP
        qh = q[:, h * ATT_HEAD_DIM:(h + 1) * ATT_HEAD_DIM]
        kh = k2[:, kv * ATT_HEAD_DIM:(kv + 1) * ATT_HEAD_DIM]
        vh = v2[:, kv * ATT_HEAD_DIM:(kv + 1) * ATT_HEAD_DIM]
        s = jnp.where(mask, _dot_nt(qh, kh), -jnp.inf)
        sink = sinks[:, h:h + 1]
        m = jnp.maximum(jnp.max(s, axis=-1, keepdims=True), sink)
        p = jnp.exp(s - m)
        p = p / (jnp.sum(p, axis=-1, keepdims=True) + jnp.exp(sink - m))
        outs.append(_dot(p.astype(BF16), vh))
    o_ref[...] = jnp.concatenate(outs, axis=1).astype(BF16)


def _attention(qa, ka, va, sinks, batch, seq):
    n = qa.shape[0]
    w = ATT_WINDOW
    nb = seq // w
    cur = lambda width: pl.BlockSpec((w, width), lambda b, j: (b * nb + j, 0))
    prev = lambda width: pl.BlockSpec((w, width), lambda b, j: (b * nb + jnp.maximum(j - 1, 0), 0))
    return pl.pallas_call(
        _attention_kernel,
        grid=(batch, nb),
        in_specs=[cur(ATT_Q_W), cur(ATT_KV_W), prev(ATT_KV_W), cur(ATT_KV_W), prev(ATT_KV_W),
                  _const_spec((1, ATT_Q_HEADS))],
        out_specs=cur(ATT_Q_W),
        out_shape=jax.ShapeDtypeStruct((n, ATT_Q_W), BF16),
        compiler_params=_params("arbitrary", "arbitrary"),
        name="swa_attention",
    )(qa, ka, ka, va, va, sinks)


def _mlstm_kernel(qk_ref, tail_ref, v_ref, o_ref, gt_ref, cw_ref, cb_ref, gb_ref, nw_ref, y_ref, c_s, n_s, m_s):
    chunk = pl.program_id(1)
    L = ML_CHUNK

    @pl.when(chunk == 0)
    def _():
        c_s[...] = jnp.zeros_like(c_s)
        n_s[...] = jnp.zeros_like(n_s)
        m_s[...] = jnp.zeros_like(m_s)

    tail = jnp.where(chunk == 0, 0.0, tail_ref[...])
    qk = _causal_conv_silu(qk_ref[...], tail, cw_ref[...], cb_ref[...])
    v_all = v_ref[...]

    g = gt_ref[...] + gb_ref[...]
    lg = jnp.where(_iota(g.shape, 0) < ML_HEADS, g, -_softplus(-g))
    tri_l, tri_u, eye = _tri_lower(L), _tri_upper(L), _eye(L)
    cum_row = _dot(lg, tri_u, HI)
    lg_col = _dot_nt(eye, lg, HI)
    cum_col = _dot_nt(tri_l, lg, HI)
    causal = _iota((L, L), 1) <= _iota((L, L), 0)

    outs = []
    for h in range(ML_HEADS):
        q = qk[:, h * ML_QK_DIM:(h + 1) * ML_QK_DIM]
        k = qk[:, ML_QK_W + h * ML_QK_DIM:ML_QK_W + (h + 1) * ML_QK_DIM] * (ML_QK_DIM ** -0.5)
        v = v_all[:, h * ML_V_DIM:(h + 1) * ML_V_DIM]
        b_col = cum_col[:, ML_HEADS + h:ML_HEADS + h + 1]
        b_row = cum_row[ML_HEADS + h:ML_HEADS + h + 1, :]
        li_col = lg_col[:, h:h + 1]
        li_row = lg[h:h + 1, :]
        b_last = b_col[L - 1:L, :]
        c_prev = c_s[h]
        n_prev = n_s[h:h + 1, :]
        m_prev = m_s[h:h + 1, 0:1]

        dlog = jnp.where(causal, b_col - b_row + li_row, -jnp.inf)
        inter_log = b_col + m_prev
        m_t = jnp.maximum(inter_log, jnp.max(dlog, axis=1, keepdims=True))
        w_intra = jnp.exp(dlog - m_t)
        w_inter = jnp.exp(inter_log - m_t)
        qb = q.astype(BF16)
        kb = k.astype(BF16)
        qk_w = _dot_nt(qb, kb) * w_intra
        num = _dot(qk_w.astype(BF16), v) + w_inter * _dot(qb, c_prev.astype(BF16))
        den = jnp.sum(qk_w, axis=1, keepdims=True) + w_inter * jnp.sum(q * n_prev, axis=1, keepdims=True)
        outs.append(num / jnp.maximum(jnp.abs(den), jnp.exp(-m_t)))

        a = b_last - b_col + li_col
        m_loc = jnp.max(a, axis=0, keepdims=True)
        kw = k * jnp.exp(a - m_loc)
        c_loc = lax.dot_general(kw.astype(BF16), v, (((0,), (0,)), ((), ())), preferred_element_type=F32)
        n_loc = jnp.sum(kw, axis=0, keepdims=True)
        m_new = jnp.maximum(b_last + m_prev, m_loc)
        fa = jnp.exp(b_last + m_prev - m_new)
        fb = jnp.exp(m_loc - m_new)
        c_s[h] = fa * c_prev + fb * c_loc
        n_s[h:h + 1, :] = fa * n_prev + fb * n_loc
        m_s[h:h + 1, 0:1] = m_new

    gate = _sigmoid(o_ref[...])
    nw = nw_ref[...]
    normed = []
    for h in range(ML_HEADS):
        x = outs[h]
        mu = jnp.mean(x, axis=-1, keepdims=True)
        var = jnp.mean(jnp.square(x - mu), axis=-1, keepdims=True)
        normed.append((x - mu) * lax.rsqrt(var + ML_NORM_EPS))
    hm = jnp.concatenate(normed, axis=1) * nw
    y_ref[...] = (hm * gate).astype(BF16)


def _mlstm(qkm, vm, om, gt, conv_w, conv_b, gate_bias, norm_w, batch, seq):
    n = qkm.shape[0]
    L = ML_CHUNK
    nc = seq // L
    per8 = L // SUBLANES
    cur = lambda width: pl.BlockSpec((L, width), lambda b, c: (b * nc + c, 0))
    tail = pl.BlockSpec((SUBLANES, 2 * ML_QK_W), lambda b, c: (jnp.maximum((b * nc + c) * per8 - 1, 0), 0))
    return pl.pallas_call(
        _mlstm_kernel,
        grid=(batch, nc),
        in_specs=[cur(2 * ML_QK_W), tail, cur(ML_V_W), cur(ML_V_W),
                  pl.BlockSpec((2 * ML_HEADS, L), lambda b, c: (0, b * nc + c)),
                  _const_spec(conv_w.shape), _const_spec(conv_b.shape), _const_spec(gate_bias.shape),
                  _const_spec(norm_w.shape)],
        out_specs=cur(ML_V_W),
        out_shape=jax.ShapeDtypeStruct((n, ML_V_W), BF16),
        scratch_shapes=[pltpu.VMEM((ML_HEADS, ML_QK_DIM, ML_V_DIM), F32),
                        pltpu.VMEM((SUBLANES, ML_QK_DIM), F32),
                        pltpu.VMEM((SUBLANES, LANES), F32)],
        compiler_params=_params("arbitrary", "arbitrary"),
        name="mlstm",
    )(qkm, qkm, vm, om, gt, conv_w, conv_b, gate_bias, norm_w)


def _outproj_mlp_kernel(ya_ref, yb_ref, h_ref, wa_ref, wb_ref, g_ref, wu_ref, wd_ref, fg_ref, o_ref,
                        *, final_norm):
    h1 = h_ref[...] + _dot(ya_ref[...], wa_ref[...]) + _dot(yb_ref[...], wb_ref[...])
    hb = _rmsnorm(h1, g_ref[...]).astype(BF16)
    mlp = None
    for c in range(D_FF // FF_CHUNK):
        u = _dot(hb, wu_ref[:, c * FF_CHUNK:(c + 1) * FF_CHUNK])
        a = jnp.square(jnp.maximum(u, 0.0)).astype(BF16)
        d = _dot(a, wd_ref[c * FF_CHUNK:(c + 1) * FF_CHUNK, :])
        mlp = d if mlp is None else mlp + d
    h2 = h1 + mlp
    if final_norm:
        h2 = _rmsnorm(h2, fg_ref[...])
    o_ref[...] = h2


def _outproj_mlp(ya, yb, h, wa, wb, g, wu, wd, fg, final_norm):
    n = h.shape[0]
    tm = PROJ_ROWS
    row = lambda width: pl.BlockSpec((tm, width), lambda i: (i, 0))
    return pl.pallas_call(
        functools.partial(_outproj_mlp_kernel, final_norm=final_norm),
        grid=(n // tm,),
        in_specs=[row(ya.shape[1]), row(yb.shape[1]), row(D_MODEL)]
                 + [_const_spec(a.shape) for a in (wa, wb, g, wu, wd, fg)],
        out_specs=row(D_MODEL),
        out_shape=jax.ShapeDtypeStruct((n, D_MODEL), F32),
        compiler_params=_params("arbitrary"),
        name="outproj_mlp_final" if final_norm else "outproj_mlp",
    )(ya, yb, h, wa, wb, g, wu, wd, fg)


def _inproj_odd_kernel(x_ref, g_ref, wz_ref, wx_ref, wdt_ref, wrw_ref, z_ref, xbc_ref, dt_ref, rw_ref):
    hb = _rmsnorm(x_ref[...], g_ref[...]).astype(BF16)
    z_ref[...] = _dot(hb, wz_ref[...])
    xbc_ref[...] = _dot(hb, wx_ref[...])
    dt_ref[...] = _dot_nt(wdt_ref[...], hb)
    rw_ref[...] = _dot(hb, wrw_ref[...])


def _inproj_odd(x2, g, w):
    n = x2.shape[0]
    tm = PROJ_ROWS
    row = lambda width: pl.BlockSpec((tm, width), lambda i: (i, 0))
    out_shapes = (jax.ShapeDtypeStruct((n, SSM_INNER), F32), jax.ShapeDtypeStruct((n, SSM_CONV_W), F32),
                  jax.ShapeDtypeStruct((SSM_HEADS, n), F32), jax.ShapeDtypeStruct((n, RW_IN), F32))
    return pl.pallas_call(
        _inproj_odd_kernel,
        grid=(n // tm,),
        in_specs=[row(D_MODEL), _const_spec((1, D_MODEL))] + [_const_spec(a.shape) for a in w],
        out_specs=(row(SSM_INNER), row(SSM_CONV_W), pl.BlockSpec((SSM_HEADS, tm), lambda i: (0, i)), row(RW_IN)),
        out_shape=out_shapes,
        compiler_params=_params("arbitrary"),
        name="inproj_odd",
    )(x2, g, *w)


def _ssd_kernel(xbc_ref, tail_ref, z_ref, dt_ref, cw_ref, cb_ref, dtb_ref, alog_ref, d_ref, nw_ref, y_ref, s_s):
    chunk = pl.program_id(1)
    L = SSM_CHUNK
    P = SSM_HEAD_DIM
    HG = SSM_HEADS_PER_GROUP

    @pl.when(chunk == 0)
    def _():
        s_s[...] = jnp.zeros_like(s_s)

    tail = jnp.where(chunk == 0, 0.0, tail_ref[...])
    xbc = _causal_conv_silu(xbc_ref[...], tail, cw_ref[...], cb_ref[...])
    xs = xbc[:, :SSM_INNER]
    bm = xbc[:, SSM_INNER:SSM_INNER + SSM_BC]
    cm = xbc[:, SSM_INNER + SSM_BC:]

    dt_row = _softplus(dt_ref[...] + dtb_ref[...])
    a_row = dt_row * (-jnp.exp(alog_ref[...]))
    tri_l, tri_u, eye = _tri_lower(L), _tri_upper(L), _eye(L)
    acum_row = _dot(a_row, tri_u, HI)
    acum_col = _dot_nt(tri_l, a_row, HI)
    dt_col = _dot_nt(eye, dt_row, HI)
    causal = _iota((L, L), 1) <= _iota((L, L), 0)
    eye_b = eye.astype(BF16)

    ys = []
    for g in range(SSM_GROUPS):
        bg = bm[:, g * SSM_STATE:(g + 1) * SSM_STATE].astype(BF16)
        cg = cm[:, g * SSM_STATE:(g + 1) * SSM_STATE].astype(BF16)
        cb = _dot_nt(cg, bg)
        bg_t = _dot_nt(eye_b, bg).astype(BF16)
        s_prev = s_s[g]
        y_off = _dot(cg, s_prev.astype(BF16))
        xw, cds = [], []
        for j in range(HG):
            h = g * HG + j
            a_col = acum_col[:, h:h + 1]
            a_r = acum_row[h:h + 1, :]
            a_last = a_col[L - 1:L, :]
            decay = jnp.exp(jnp.where(causal, a_col - a_r, -jnp.inf))
            xc = xs[:, h * P:(h + 1) * P] * dt_col[:, h:h + 1]
            y_h = _dot((cb * decay).astype(BF16), xc.astype(BF16))
            y_h = y_h + y_off[:, j * P:(j + 1) * P] * jnp.exp(a_col)
            ys.append(y_h)
            xw.append(xc * jnp.exp(a_last - a_col))
            cds.append(jnp.broadcast_to(jnp.exp(a_last), (1, P)))
        states = _dot(bg_t, jnp.concatenate(xw, axis=1).astype(BF16))
        s_s[g] = s_prev * jnp.concatenate(cds, axis=1) + states

    y = jnp.concatenate(ys, axis=1) + d_ref[...] * xs
    y = y * _silu(z_ref[...])
    gw = SSM_INNER // SSM_GROUPS
    normed = []
    for g in range(SSM_GROUPS):
        yg = y[:, g * gw:(g + 1) * gw]
        normed.append(yg * lax.rsqrt(jnp.mean(yg * yg, axis=-1, keepdims=True) + NORM_EPS))
    y_ref[...] = (jnp.concatenate(normed, axis=1) * nw_ref[...]).astype(BF16)


def _ssd(xbc, z, dt_t, conv_w, conv_b, dt_bias, a_log, d_row, norm_w, batch, seq):
    n = xbc.shape[0]
    L = SSM_CHUNK
    nc = seq // L
    per8 = L // SUBLANES
    cur = lambda width: pl.BlockSpec((L, width), lambda b, c: (b * nc + c, 0))
    tail = pl.BlockSpec((SUBLANES, SSM_CONV_W), lambda b, c: (jnp.maximum((b * nc + c) * per8 - 1, 0), 0))
    consts = (conv_w, conv_b, dt_bias, a_log, d_row, norm_w)
    return pl.pallas_call(
        _ssd_kernel,
        grid=(batch, nc),
        in_specs=[cur(SSM_CONV_W), tail, cur(SSM_INNER),
                  pl.BlockSpec((SSM_HEADS, L), lambda b, c: (0, b * nc + c))]
                 + [_const_spec(a.shape) for a in consts],
        out_specs=cur(SSM_INNER),
        out_shape=jax.ShapeDtypeStruct((n, SSM_INNER), BF16),
        scratch_shapes=[pltpu.VMEM((SSM_GROUPS, SSM_STATE, SSM_HEADS_PER_GROUP * SSM_HEAD_DIM), F32)],
        compiler_params=_params("arbitrary", "arbitrary"),
        name="ssd",
    )(xbc, xbc, z, dt_t, *consts)


def _rwkv_kernel(rw_ref, tail_ref, mu_ref, w0_ref, w2_ref, a0_ref, a2_ref, g2_ref, kk_ref, ka_ref, rk_ref,
                 lnw_ref, lnb_ref, y_ref, st_s):
    chunk = pl.program_id(1)
    C = RW_CHUNK
    D = RW_HEAD_DIM

    @pl.when(chunk == 0)
    def _():
        st_s[...] = jnp.zeros_like(st_s)

    x = rw_ref[...]
    before = jnp.where(chunk == 0, 0.0, tail_ref[...][SUBLANES - 1:SUBLANES, :])
    prev = jnp.where(_iota((C, 1), 0) == 0, before, pltpu.roll(x, 1, axis=0))
    xm = x + (prev - x) * mu_ref[...]
    r = xm[:, 0:RW_W]
    k = xm[:, RW_W:2 * RW_W]
    v = xm[:, 2 * RW_W:3 * RW_W]
    o = 3 * RW_W
    wd = xm[:, o:o + RW_DECAY_LORA]
    ad = xm[:, o + RW_DECAY_LORA:o + RW_DECAY_LORA + RW_AAA_LORA]
    gd = xm[:, o + RW_DECAY_LORA + RW_AAA_LORA:]

    w_log = -_softplus(-(w0_ref[...] + _dot(jnp.tanh(wd), w2_ref[...], HI))) - 0.5
    ld = -jnp.exp(w_log)
    a = _sigmoid(a0_ref[...] + _dot(ad, a2_ref[...], HI))
    gate = _dot(_sigmoid(gd), g2_ref[...], HI)

    same_head = (_iota((RW_W, RW_W), 0) // D == _iota((RW_W, RW_W), 1) // D).astype(F32)
    head_sum = lambda t: _dot(t, same_head, HI)

    kk = k * kk_ref[...]
    kk = kk / jnp.maximum(jnp.sqrt(head_sum(kk * kk)), 1e-12)
    k2 = k * (1.0 + (a - 1.0) * ka_ref[...])
    bvec = kk * a

    cl = _dot(_tri_lower(C), ld, HI)
    c_last = cl[C - 1:C, :]
    e_neg = jnp.exp(-cl)
    e_end = jnp.exp(c_last - cl)
    rw_ = r * jnp.exp(cl)
    kkw = kk * jnp.exp(cl - ld)
    bd = bvec * e_neg
    kd = k2 * e_neg
    bd_end = bvec * e_end
    kd_end = k2 * e_end
    w_end = jnp.broadcast_to(jnp.exp(c_last), (SUBLANES, RW_W))

    strict = _iota((C, C), 1) < _iota((C, C), 0)
    incl = _iota((C, C), 1) <= _iota((C, C), 0)
    eye = _eye(C)
    ys = []
    for h in range(RW_HEADS):
        sl = slice(h * D, (h + 1) * D)
        lhs = jnp.concatenate([kkw[:, sl], rw_[:, sl]], axis=0)
        rhs = jnp.concatenate([bd[:, sl], kd[:, sl]], axis=0)
        gm = _dot_nt(lhs, rhs, HI)
        a_ab = jnp.where(strict, gm[:C, :C], 0.0)
        a_ak = jnp.where(strict, gm[:C, C:], 0.0)
        a_rb = jnp.where(incl, gm[C:, :C], 0.0)
        a_rk = jnp.where(incl, gm[C:, C:], 0.0)
        pw = -a_ab
        inv = eye + pw
        sq = 1
        while 2 * sq < C:
            pw = _dot(pw, pw, HI)
            inv = inv + _dot(inv, pw, HI)
            sq *= 2
        vh = v[:, sl]
        tw = _dot(inv, jnp.concatenate([_dot(a_ak, vh, HI), kkw[:, sl]], axis=1), HI)
        st = st_s[h]
        xs = _dot(jnp.concatenate([tw[:, D:], rw_[:, sl]], axis=0), st, HI)
        u = -tw[:, :D] - xs[:C]
        ys.append(xs[C:] + _dot(a_rb, u, HI) + _dot(a_rk, vh, HI))
        zt = _dot_nt(eye, jnp.concatenate([bd_end[:, sl], kd_end[:, sl], w_end[:, sl]], axis=0), HI)
        st_s[h] = zt[:, 2 * C:2 * C + 1] * st + _dot(zt[:, :C], u, HI) + _dot(zt[:, C:2 * C], vh, HI)

    y = jnp.concatenate(ys, axis=1)
    mu = head_sum(y) * (1.0 / D)
    var = head_sum(jnp.square(y - mu)) * (1.0 / D)
    yn = (y - mu) * lax.rsqrt(var + RW_LN_EPS) * lnw_ref[...] + lnb_ref[...]
    bonus = head_sum(r * k2 * rk_ref[...]) * v
    y_ref[...] = ((yn + bonus) * gate).astype(BF16)


def _rwkv(rw, consts, batch, seq):
    n = rw.shape[0]
    C = RW_CHUNK
    nc = seq // C
    per8 = C // SUBLANES
    cur = lambda width: pl.BlockSpec((C, width), lambda b, c: (b * nc + c, 0))
    tail = pl.BlockSpec((SUBLANES, RW_IN), lambda b, c: (jnp.maximum((b * nc + c) * per8 - 1, 0), 0))
    return pl.pallas_call(
        _rwkv_kernel,
        grid=(batch, nc),
        in_specs=[cur(RW_IN), tail] + [_const_spec(a.shape) for a in consts],
        out_specs=cur(RW_W),
        out_shape=jax.ShapeDtypeStruct((n, RW_W), BF16),
        scratch_shapes=[pltpu.VMEM((RW_HEADS, RW_HEAD_DIM, RW_HEAD_DIM), F32)],
        compiler_params=_params("arbitrary", "arbitrary"),
        name="rwkv7",
    )(rw, rw, *consts)


def _rope_tables(seq):
    half = ROPE_DIM // 2
    inv_freq = ROPE_THETA ** (-jnp.arange(half, dtype=F32) * 2.0 / ROPE_DIM)
    ang = jnp.arange(seq, dtype=F32)[:, None] * inv_freq[None, :]
    cos, sin = jnp.cos(ang), jnp.sin(ang)
    ones = jnp.ones((seq, ATT_HEAD_DIM - ROPE_DIM), F32)
    zeros = jnp.zeros((seq, ATT_HEAD_DIM - ROPE_DIM), F32)
    zh = jnp.zeros((seq, half), F32)
    reps = LANES // ATT_HEAD_DIM
    cos_t = jnp.concatenate([cos, cos, ones] * reps, axis=1)
    sin_lo = jnp.concatenate([-sin, zh, zeros] * reps, axis=1)
    sin_hi = jnp.concatenate([zh, sin, zeros] * reps, axis=1)
    return cos_t, sin_lo, sin_hi


def _row(v):
    return v.reshape(1, -1).astype(F32)


def _col(v):
    return v.reshape(-1, 1).astype(F32)


def kernel(x, mix_norm, mlp_norm, w_up, w_down, e_w_in, e_w_out, att_sinks, ml_conv_w, ml_conv_b, ml_i_bias,
           ml_f_bias, ml_norm, o_w_in, o_w_out, ssm_conv_w, ssm_conv_b, ssm_dt_bias, ssm_a_log, ssm_d, ssm_norm,
           rw_mu, rw_w0, rw_w2, rw_a0, rw_a2, rw_g2, rw_k_k, rw_k_a, rw_r_k, rw_ln_w, rw_ln_b, final_norm):
    batch, seq, _ = x.shape
    h = x.reshape(batch * seq, D_MODEL)
    fg = _row(final_norm)

    w = e_w_in[0].astype(BF16)
    o = 0
    parts = []
    for width in (ATT_Q_W, ATT_KV_W, ATT_KV_W, 2 * ML_QK_W, ML_V_W, ML_V_W):
        parts.append(w[:, o:o + width])
        o += width
    parts.append(w[:, o:o + 2 * ML_HEADS].T)
    qa, ka, va, qkm, vm, om, gt = _inproj_even(h, _row(mix_norm[0]), _rope_tables(seq), parts, seq)
    y_att = _attention(qa, ka, va, _row(att_sinks[0]), batch, seq)
    gate_bias = _col(jnp.concatenate([ml_i_bias[0], ml_f_bias[0]]))
    y_ml = _mlstm(qkm, vm, om, gt, ml_conv_w[0].astype(F32), _row(ml_conv_b[0]), gate_bias, _row(ml_norm[0]),
                  batch, seq)
    wo = e_w_out[0].astype(BF16)
    h = _outproj_mlp(y_att, y_ml, h, wo[:ATT_Q_W], wo[ATT_Q_W:], _row(mlp_norm[0]), w_up[0].astype(BF16),
                     w_down[0].astype(BF16), fg, final_norm=False)

    w = o_w_in[0].astype(BF16)
    o_x = SSM_INNER
    o_dt = o_x + SSM_CONV_W
    o_rw = o_dt + SSM_HEADS
    parts = [w[:, :o_x], w[:, o_x:o_dt], w[:, o_dt:o_rw].T, w[:, o_rw:]]
    z, xbc, dt_t, rw = _inproj_odd(h, _row(mix_norm[1]), parts)
    y_ssm = _ssd(xbc, z, dt_t, ssm_conv_w[0].astype(F32), _row(ssm_conv_b[0]), _col(ssm_dt_bias[0]),
                 _col(ssm_a_log[0]), _row(jnp.repeat(ssm_d[0], SSM_HEAD_DIM)), _row(ssm_norm[0]), batch, seq)
    rw_consts = (_row(rw_mu[0]), _row(rw_w0[0]), rw_w2[0].astype(F32), _row(rw_a0[0]), rw_a2[0].astype(F32),
                 rw_g2[0].astype(F32), _row(rw_k_k[0]), _row(rw_k_a[0]), _row(rw_r_k[0]), _row(rw_ln_w[0]),
                 _row(rw_ln_b[0]))
    y_rw = _rwkv(rw, rw_consts, batch, seq)
    wo = o_w_out[0].astype(BF16)
    h = _outproj_mlp(y_ssm, y_rw, h, wo[:SSM_INNER], wo[SSM_INNER:], _row(mlp_norm[1]), w_up[1].astype(BF16),
                     w_down[1].astype(BF16), fg, final_norm=True)
    return h.reshape(batch, seq, D_MODEL)
```

```python
import functools

import jax
import jax.numpy as jnp
from jax import lax
from jax.experimental import pallas as pl
from jax.experimental.pallas import tpu as pltpu

F32 = jnp.float32
BF16 = jnp.bfloat16
HI = lax.Precision.HIGHEST

D_MODEL = 1024
D_FF = 4 * D_MODEL
NORM_EPS = 1e-5
CONV_TAPS = 4

ATT_HEAD_DIM = 64
ATT_Q_HEADS = 8
ATT_KV_HEADS = 2
ATT_GROUP = ATT_Q_HEADS // ATT_KV_HEADS
ATT_WINDOW = 128
ROPE_THETA = 500000.0
ROPE_DIM = 16
ATT_Q_W = ATT_Q_HEADS * ATT_HEAD_DIM
ATT_KV_W = ATT_KV_HEADS * ATT_HEAD_DIM

ML_HEADS = 4
ML_V_DIM = 128
ML_QK_DIM = 64
ML_CHUNK = 128
ML_NORM_EPS = 1e-6
ML_QK_W = ML_HEADS * ML_QK_DIM
ML_V_W = ML_HEADS * ML_V_DIM

SSM_HEAD_DIM = 64
SSM_HEADS = 8
SSM_GROUPS = 2
SSM_HEADS_PER_GROUP = SSM_HEADS // SSM_GROUPS
SSM_STATE = 128
SSM_CHUNK = 128
SSM_INNER = SSM_HEADS * SSM_HEAD_DIM
SSM_BC = SSM_GROUPS * SSM_STATE
SSM_CONV_W = SSM_INNER + 2 * SSM_BC

RW_HEAD_DIM = 64
RW_HEADS = 8
RW_W = RW_HEADS * RW_HEAD_DIM
RW_DECAY_LORA = 64
RW_AAA_LORA = 64
RW_GATE_LORA = 128
RW_IN = 3 * RW_W + RW_DECAY_LORA + RW_AAA_LORA + RW_GATE_LORA
RW_LN_EPS = 64e-5
RW_CHUNK = 64

V7X_VMEM_BYTES = 64 * 1024 * 1024
VMEM_LIMIT_BYTES = V7X_VMEM_BYTES - 16 * 1024 * 1024
SUBLANES = 8
LANES = 128

PROJ_ROWS = 512
FF_CHUNK = 512


def _params(*sem):
    return pltpu.CompilerParams(dimension_semantics=sem, vmem_limit_bytes=VMEM_LIMIT_BYTES)


def _const_spec(shape):
    nd = len(shape)
    return pl.BlockSpec(shape, lambda *_: (0,) * nd, pipeline_mode=pl.Buffered(1))


def _iota(shape, dim):
    return lax.broadcasted_iota(jnp.int32, shape, dim)


def _tri_lower(n):
    return (_iota((n, n), 1) <= _iota((n, n), 0)).astype(F32)


def _tri_upper(n):
    return (_iota((n, n), 0) <= _iota((n, n), 1)).astype(F32)


def _eye(n):
    return (_iota((n, n), 0) == _iota((n, n), 1)).astype(F32)


def _dot(a, b, precision=None):
    return jnp.dot(a, b, preferred_element_type=F32, precision=precision)


def _dot_nt(a, b, precision=None):
    return lax.dot_general(a, b, (((1,), (1,)), ((), ())), preferred_element_type=F32, precision=precision)


def _sigmoid(x):
    return 1.0 / (1.0 + jnp.exp(-x))


def _softplus(x):
    return jnp.maximum(x, 0.0) + jnp.log1p(jnp.exp(-jnp.abs(x)))


def _silu(x):
    return x * _sigmoid(x)


def _rmsnorm(x, g):
    y = x * lax.rsqrt(jnp.mean(x * x, axis=-1, keepdims=True) + NORM_EPS)
    return y * g


def _shifted_rows(x, tail, k):
    rolled = pltpu.roll(x, k, axis=0)
    tail_rolled = pltpu.roll(tail, k, axis=0)
    first = jnp.where(_iota((SUBLANES, 1), 0) < k, tail_rolled, rolled[:SUBLANES])
    return jnp.concatenate([first, rolled[SUBLANES:]], axis=0)


def _causal_conv_silu(x, tail, w, b):
    out = b + x * w[CONV_TAPS - 1:CONV_TAPS, :]
    for k in range(1, CONV_TAPS):
        out = out + _shifted_rows(x, tail, k) * w[CONV_TAPS - 1 - k:CONV_TAPS - k, :]
    return _silu(out)


def _rope(x, cos, sin_lo, sin_hi, reps):
    w = x.shape[1]
    c = jnp.concatenate([cos] * reps, axis=1)
    s_lo = jnp.concatenate([sin_lo] * reps, axis=1)
    s_hi = jnp.concatenate([sin_hi] * reps, axis=1)
    half = ROPE_DIM // 2
    nxt = pltpu.roll(x, w - half, axis=1)
    prv = pltpu.roll(x, half, axis=1)
    return x * c + nxt * s_lo + prv * s_hi


def _inproj_even_kernel(x_ref, g_ref, cos_ref, slo_ref, shi_ref, wq_ref, wk_ref, wv_ref, wqk_ref, wvm_ref,
                        wom_ref, wgt_ref, qa_ref, ka_ref, va_ref, qkm_ref, vm_ref, om_ref, gt_ref):
    hb = _rmsnorm(x_ref[...], g_ref[...]).astype(BF16)
    cos, s_lo, s_hi = cos_ref[...], slo_ref[...], shi_ref[...]
    q = _rope(_dot(hb, wq_ref[...]), cos, s_lo, s_hi, ATT_Q_W // LANES)
    qa_ref[...] = (q * (ATT_HEAD_DIM ** -0.5)).astype(BF16)
    ka_ref[...] = _rope(_dot(hb, wk_ref[...]), cos, s_lo, s_hi, ATT_KV_W // LANES).astype(BF16)
    va_ref[...] = _dot(hb, wv_ref[...]).astype(BF16)
    qkm_ref[...] = _dot(hb, wqk_ref[...])
    vm_ref[...] = _dot(hb, wvm_ref[...]).astype(BF16)
    om_ref[...] = _dot(hb, wom_ref[...])
    gt_ref[...] = _dot_nt(wgt_ref[...], hb)


def _inproj_even(x2, g, tables, w, seq):
    n = x2.shape[0]
    tm = PROJ_ROWS
    per_seq = seq // tm
    row = lambda width: pl.BlockSpec((tm, width), lambda i: (i, 0))
    tab = pl.BlockSpec((tm, LANES), lambda i: (i % per_seq, 0))
    wq, wk, wv, wqk, wvm, wom, wgt = w
    out_shapes = (
        jax.ShapeDtypeStruct((n, ATT_Q_W), BF16), jax.ShapeDtypeStruct((n, ATT_KV_W), BF16),
        jax.ShapeDtypeStruct((n, ATT_KV_W), BF16), jax.ShapeDtypeStruct((n, 2 * ML_QK_W), F32),
        jax.ShapeDtypeStruct((n, ML_V_W), BF16), jax.ShapeDtypeStruct((n, ML_V_W), F32),
        jax.ShapeDtypeStruct((2 * ML_HEADS, n), F32))
    return pl.pallas_call(
        _inproj_even_kernel,
        grid=(n // tm,),
        in_specs=[row(D_MODEL), _const_spec((1, D_MODEL)), tab, tab, tab] + [_const_spec(a.shape) for a in w],
        out_specs=(row(ATT_Q_W), row(ATT_KV_W), row(ATT_KV_W), row(2 * ML_QK_W), row(ML_V_W), row(ML_V_W),
                   pl.BlockSpec((2 * ML_HEADS, tm), lambda i: (0, i))),
        out_shape=out_shapes,
        compiler_params=_params("arbitrary"),
        name="inproj_even",
    )(x2, g, *tables, wq, wk, wv, wqk, wvm, wom, wgt)


def _attention_kernel(q_ref, kc_ref, kp_ref, vc_ref, vp_ref, sink_ref, o_ref):
    blk = pl.program_id(1)
    w = ATT_WINDOW
    q = q_ref[...]
    k2 = jnp.concatenate([kp_ref[...], kc_ref[...]], axis=0)
    v2 = jnp.concatenate([vp_ref[...], vc_ref[...]], axis=0)
    i = _iota((w, 2 * w), 0)
    j = _iota((w, 2 * w), 1)
    mask = (j > i) & (j <= i + w) & ((j >= w) | (blk > 0))
    sinks = sink_ref[...]
    outs = []
    for h in range(ATT_Q_HEADS):
        kv = h // ATT_GROUP
        qh = q[:, h * ATT_HEAD_DIM:(h + 1) * ATT_HEAD_DIM]
        kh = k2[:, kv * ATT_HEAD_DIM:(kv + 1) * ATT_HEAD_DIM]
        vh = v2[:, kv * ATT_HEAD_DIM:(kv + 1) * ATT_HEAD_DIM]
        s = jnp.where(mask, _dot_nt(qh, kh), -jnp.inf)
        sink = sinks[:, h:h + 1]
        m = jnp.maximum(jnp.max(s, axis=-1, keepdims=True), sink)
        p = jnp.exp(s - m)
        p = p / (jnp.sum(p, axis=-1, keepdims=True) + jnp.exp(sink - m))
        outs.append(_dot(p.astype(BF16), vh))
    o_ref[...] = jnp.concatenate(outs, axis=1).astype(BF16)


def _attention(qa, ka, va, sinks, batch, seq):
    n = qa.shape[0]
    w = ATT_WINDOW
    nb = seq // w
    cur = lambda width: pl.BlockSpec((w, width), lambda b, j: (b * nb + j, 0))
    prev = lambda width: pl.BlockSpec((w, width), lambda b, j: (b * nb + jnp.maximum(j - 1, 0), 0))
    return pl.pallas_call(
        _attention_kernel,
        grid=(batch, nb),
        in_specs=[cur(ATT_Q_W), cur(ATT_KV_W), prev(ATT_KV_W), cur(ATT_KV_W), prev(ATT_KV_W),
                  _const_spec((1, ATT_Q_HEADS))],
        out_specs=cur(ATT_Q_W),
        out_shape=jax.ShapeDtypeStruct((n, ATT_Q_W), BF16),
        compiler_params=_params("arbitrary", "arbitrary"),
        name="swa_attention",
    )(qa, ka, ka, va, va, sinks)


def _mlstm_kernel(qk_ref, tail_ref, v_ref, o_ref, gt_ref, cw_ref, cb_ref, gb_ref, nw_ref, y_ref, c_s, n_s, m_s):
    chunk = pl.program_id(1)
    L = ML_CHUNK

    @pl.when(chunk == 0)
    def _():
        c_s[...] = jnp.zeros_like(c_s)
        n_s[...] = jnp.zeros_like(n_s)
        m_s[...] = jnp.zeros_like(m_s)

    tail = jnp.where(chunk == 0, 0.0, tail_ref[...])
    qk = _causal_conv_silu(qk_ref[...], tail, cw_ref[...], cb_ref[...])
    v_all = v_ref[...]

    g = gt_ref[...] + gb_ref[...]
    lg = jnp.where(_iota(g.shape, 0) < ML_HEADS, g, -_softplus(-g))
    tri_l, tri_u, eye = _tri_lower(L), _tri_upper(L), _eye(L)
    cum_row = _dot(lg, tri_u, HI)
    lg_col = _dot_nt(eye, lg, HI)
    cum_col = _dot_nt(tri_l, lg, HI)
    causal = _iota((L, L), 1) <= _iota((L, L), 0)

    outs = []
    for h in range(ML_HEADS):
        q = qk[:, h * ML_QK_DIM:(h + 1) * ML_QK_DIM]
        k = qk[:, ML_QK_W + h * ML_QK_DIM:ML_QK_W + (h + 1) * ML_QK_DIM] * (ML_QK_DIM ** -0.5)
        v = v_all[:, h * ML_V_DIM:(h + 1) * ML_V_DIM]
        b_col = cum_col[:, ML_HEADS + h:ML_HEADS + h + 1]
        b_row = cum_row[ML_HEADS + h:ML_HEADS + h + 1, :]
        li_col = lg_col[:, h:h + 1]
        li_row = lg[h:h + 1, :]
        b_last = b_col[L - 1:L, :]
        c_prev = c_s[h]
        n_prev = n_s[h:h + 1, :]
        m_prev = m_s[h:h + 1, 0:1]

        dlog = jnp.where(causal, b_col - b_row + li_row, -jnp.inf)
        inter_log = b_col + m_prev
        m_t = jnp.maximum(inter_log, jnp.max(dlog, axis=1, keepdims=True))
        w_intra = jnp.exp(dlog - m_t)
        w_inter = jnp.exp(inter_log - m_t)
        qb = q.astype(BF16)
        kb = k.astype(BF16)
        qk_w = _dot_nt(qb, kb) * w_intra
        num = _dot(qk_w.astype(BF16), v) + w_inter * _dot(qb, c_prev.astype(BF16))
        den = jnp.sum(qk_w, axis=1, keepdims=True) + w_inter * jnp.sum(q * n_prev, axis=1, keepdims=True)
        outs.append(num / jnp.maximum(jnp.abs(den), jnp.exp(-m_t)))

        a = b_last - b_col + li_col
        m_loc = jnp.max(a, axis=0, keepdims=True)
        kw = k * jnp.exp(a - m_loc)
        c_loc = lax.dot_general(kw.astype(BF16), v, (((0,), (0,)), ((), ())), preferred_element_type=F32)
        n_loc = jnp.sum(kw, axis=0, keepdims=True)
        m_new = jnp.maximum(b_last + m_prev, m_loc)
        fa = jnp.exp(b_last + m_prev - m_new)
        fb = jnp.exp(m_loc - m_new)
        c_s[h] = fa * c_prev + fb * c_loc
        n_s[h:h + 1, :] = fa * n_prev + fb * n_loc
        m_s[h:h + 1, 0:1] = m_new

    gate = _sigmoid(o_ref[...])
    nw = nw_ref[...]
    normed = []
    for h in range(ML_HEADS):
        x = outs[h]
        mu = jnp.mean(x, axis=-1, keepdims=True)
        var = jnp.mean(jnp.square(x - mu), axis=-1, keepdims=True)
        normed.append((x - mu) * lax.rsqrt(var + ML_NORM_EPS))
    hm = jnp.concatenate(normed, axis=1) * nw
    y_ref[...] = (hm * gate).astype(BF16)


def _mlstm(qkm, vm, om, gt, conv_w, conv_b, gate_bias, norm_w, batch, seq):
    n = qkm.shape[0]
    L = ML_CHUNK
    nc = seq // L
    per8 = L // SUBLANES
    cur = lambda width: pl.BlockSpec((L, width), lambda b, c: (b * nc + c, 0))
    tail = pl.BlockSpec((SUBLANES, 2 * ML_QK_W), lambda b, c: (jnp.maximum((b * nc + c) * per8 - 1, 0), 0))
    return pl.pallas_call(
        _mlstm_kernel,
        grid=(batch, nc),
        in_specs=[cur(2 * ML_QK_W), tail, cur(ML_V_W), cur(ML_V_W),
                  pl.BlockSpec((2 * ML_HEADS, L), lambda b, c: (0, b * nc + c)),
                  _const_spec(conv_w.shape), _const_spec(conv_b.shape), _const_spec(gate_bias.shape),
                  _const_spec(norm_w.shape)],
        out_specs=cur(ML_V_W),
        out_shape=jax.ShapeDtypeStruct((n, ML_V_W), BF16),
        scratch_shapes=[pltpu.VMEM((ML_HEADS, ML_QK_DIM, ML_V_DIM), F32),
                        pltpu.VMEM((SUBLANES, ML_QK_DIM), F32),
                        pltpu.VMEM((SUBLANES, LANES), F32)],
        compiler_params=_params("arbitrary", "arbitrary"),
        name="mlstm",
    )(qkm, qkm, vm, om, gt, conv_w, conv_b, gate_bias, norm_w)


def _outproj_mlp_kernel(ya_ref, yb_ref, h_ref, wa_ref, wb_ref, g_ref, wu_ref, wd_ref, fg_ref, o_ref,
                        *, final_norm):
    h1 = h_ref[...] + _dot(ya_ref[...], wa_ref[...]) + _dot(yb_ref[...], wb_ref[...])
    hb = _rmsnorm(h1, g_ref[...]).astype(BF16)
    mlp = None
    for c in range(D_FF // FF_CHUNK):
        u = _dot(hb, wu_ref[:, c * FF_CHUNK:(c + 1) * FF_CHUNK])
        a = jnp.square(jnp.maximum(u, 0.0)).astype(BF16)
        d = _dot(a, wd_ref[c * FF_CHUNK:(c + 1) * FF_CHUNK, :])
        mlp = d if mlp is None else mlp + d
    h2 = h1 + mlp
    if final_norm:
        h2 = _rmsnorm(h2, fg_ref[...])
    o_ref[...] = h2


def _outproj_mlp(ya, yb, h, wa, wb, g, wu, wd, fg, final_norm):
    n = h.shape[0]
    tm = PROJ_ROWS
    row = lambda width: pl.BlockSpec((tm, width), lambda i: (i, 0))
    return pl.pallas_call(
        functools.partial(_outproj_mlp_kernel, final_norm=final_norm),
        grid=(n // tm,),
        in_specs=[row(ya.shape[1]), row(yb.shape[1]), row(D_MODEL)]
                 + [_const_spec(a.shape) for a in (wa, wb, g, wu, wd, fg)],
        out_specs=row(D_MODEL),
        out_shape=jax.ShapeDtypeStruct((n, D_MODEL), F32),
        compiler_params=_params("arbitrary"),
        name="outproj_mlp_final" if final_norm else "outproj_mlp",
    )(ya, yb, h, wa, wb, g, wu, wd, fg)


def _inproj_odd_kernel(x_ref, g_ref, wz_ref, wx_ref, wdt_ref, wrw_ref, z_ref, xbc_ref, dt_ref, rw_ref):
    hb = _rmsnorm(x_ref[...], g_ref[...]).astype(BF16)
    z_ref[...] = _dot(hb, wz_ref[...])
    xbc_ref[...] = _dot(hb, wx_ref[...])
    dt_ref[...] = _dot_nt(wdt_ref[...], hb)
    rw_ref[...] = _dot(hb, wrw_ref[...])


def _inproj_odd(x2, g, w):
    n = x2.shape[0]
    tm = PROJ_ROWS
    row = lambda width: pl.BlockSpec((tm, width), lambda i: (i, 0))
    out_shapes = (jax.ShapeDtypeStruct((n, SSM_INNER), F32), jax.ShapeDtypeStruct((n, SSM_CONV_W), F32),
                  jax.ShapeDtypeStruct((SSM_HEADS, n), F32), jax.ShapeDtypeStruct((n, RW_IN), F32))
    return pl.pallas_call(
        _inproj_odd_kernel,
        grid=(n // tm,),
        in_specs=[row(D_MODEL), _const_spec((1, D_MODEL))] + [_const_spec(a.shape) for a in w],
        out_specs=(row(SSM_INNER), row(SSM_CONV_W), pl.BlockSpec((SSM_HEADS, tm), lambda i: (0, i)), row(RW_IN)),
        out_shape=out_shapes,
        compiler_params=_params("arbitrary"),
        name="inproj_odd",
    )(x2, g, *w)


def _ssd_kernel(xbc_ref, tail_ref, z_ref, dt_ref, cw_ref, cb_ref, dtb_ref, alog_ref, d_ref, nw_ref, y_ref, s_s):
    chunk = pl.program_id(1)
    L = SSM_CHUNK
    P = SSM_HEAD_DIM
    HG = SSM_HEADS_PER_GROUP

    @pl.when(chunk == 0)
    def _():
        s_s[...] = jnp.zeros_like(s_s)

    tail = jnp.where(chunk == 0, 0.0, tail_ref[...])
    xbc = _causal_conv_silu(xbc_ref[...], tail, cw_ref[...], cb_ref[...])
    xs = xbc[:, :SSM_INNER]
    bm = xbc[:, SSM_INNER:SSM_INNER + SSM_BC]
    cm = xbc[:, SSM_INNER + SSM_BC:]

    dt_row = _softplus(dt_ref[...] + dtb_ref[...])
    a_row = dt_row * (-jnp.exp(alog_ref[...]))
    tri_l, tri_u, eye = _tri_lower(L), _tri_upper(L), _eye(L)
    acum_row = _dot(a_row, tri_u, HI)
    acum_col = _dot_nt(tri_l, a_row, HI)
    dt_col = _dot_nt(eye, dt_row, HI)
    causal = _iota((L, L), 1) <= _iota((L, L), 0)
    eye_b = eye.astype(BF16)

    ys = []
    for g in range(SSM_GROUPS):
        bg = bm[:, g * SSM_STATE:(g + 1) * SSM_STATE].astype(BF16)
        cg = cm[:, g * SSM_STATE:(g + 1) * SSM_STATE].astype(BF16)
        cb = _dot_nt(cg, bg)
        bg_t = _dot_nt(eye_b, bg).astype(BF16)
        s_prev = s_s[g]
        y_off = _dot(cg, s_prev.astype(BF16))
        xw, cds = [], []
        for j in range(HG):
            h = g * HG + j
            a_col = acum_col[:, h:h + 1]
            a_r = acum_row[h:h + 1, :]
            a_last = a_col[L - 1:L, :]
            decay = jnp.exp(jnp.where(causal, a_col - a_r, -jnp.inf))
            xc = xs[:, h * P:(h + 1) * P] * dt_col[:, h:h + 1]
            y_h = _dot((cb * decay).astype(BF16), xc.astype(BF16))
            y_h = y_h + y_off[:, j * P:(j + 1) * P] * jnp.exp(a_col)
            ys.append(y_h)
            xw.append(xc * jnp.exp(a_last - a_col))
            cds.append(jnp.broadcast_to(jnp.exp(a_last), (1, P)))
        states = _dot(bg_t, jnp.concatenate(xw, axis=1).astype(BF16))
        s_s[g] = s_prev * jnp.concatenate(cds, axis=1) + states

    y = jnp.concatenate(ys, axis=1) + d_ref[...] * xs
    y = y * _silu(z_ref[...])
    gw = SSM_INNER // SSM_GROUPS
    normed = []
    for g in range(SSM_GROUPS):
        yg = y[:, g * gw:(g + 1) * gw]
        normed.append(yg * lax.rsqrt(jnp.mean(yg * yg, axis=-1, keepdims=True) + NORM_EPS))
    y_ref[...] = (jnp.concatenate(normed, axis=1) * nw_ref[...]).astype(BF16)


def _ssd(xbc, z, dt_t, conv_w, conv_b, dt_bias, a_log, d_row, norm_w, batch, seq):
    n = xbc.shape[0]
    L = SSM_CHUNK
    nc = seq // L
    per8 = L // SUBLANES
    cur = lambda width: pl.BlockSpec((L, width), lambda b, c: (b * nc + c, 0))
    tail = pl.BlockSpec((SUBLANES, SSM_CONV_W), lambda b, c: (jnp.maximum((b * nc + c) * per8 - 1, 0), 0))
    consts = (conv_w, conv_b, dt_bias, a_log, d_row, norm_w)
    return pl.pallas_call(
        _ssd_kernel,
        grid=(batch, nc),
        in_specs=[cur(SSM_CONV_W), tail, cur(SSM_INNER),
                  pl.BlockSpec((SSM_HEADS, L), lambda b, c: (0, b * nc + c))]
                 + [_const_spec(a.shape) for a in consts],
        out_specs=cur(SSM_INNER),
        out_shape=jax.ShapeDtypeStruct((n, SSM_INNER), BF16),
        scratch_shapes=[pltpu.VMEM((SSM_GROUPS, SSM_STATE, SSM_HEADS_PER_GROUP * SSM_HEAD_DIM), F32)],
        compiler_params=_params("arbitrary", "arbitrary"),
        name="ssd",
    )(xbc, xbc, z, dt_t, *consts)


def _split_bf16(x):
    hi = x.astype(BF16)
    lo = (x - hi.astype(F32)).astype(BF16)
    return hi, lo


def _dot_ones_exact(t, ones_matrix):
    hi, lo = _split_bf16(t)
    return _dot(hi, ones_matrix) + _dot(lo, ones_matrix)


RW_NEWTON_EXACT_STEPS = 1


def _unit_lower_inverses(mats, eye, n):
    splits = [_split_bf16(a) for a in mats]
    xbs = [(eye - a).astype(BF16) for a in mats]
    xs = None
    steps = 0
    terms = 2
    while terms < n:
        terms *= 2
        steps += 1
    for it in range(steps):
        if it < steps - RW_NEWTON_EXACT_STEPS:
            prods = [_dot(a_hi, xb) for (a_hi, _), xb in zip(splits, xbs)]
        else:
            prods = [_dot(a_hi, xb) + _dot(a_lo, xb) for (a_hi, a_lo), xb in zip(splits, xbs)]
        resids = [(eye - xb.astype(F32) - p).astype(BF16) for xb, p in zip(xbs, prods)]
        xs = [xb.astype(F32) + _dot(xb, e) for xb, e in zip(xbs, resids)]
        xbs = [x.astype(BF16) for x in xs]
    return xs


def _rwkv_kernel(rw_ref, tail_ref, mu_ref, w0_ref, w2_ref, a0_ref, a2_ref, g2_ref, kk_ref, ka_ref, rk_ref,
                 lnw_ref, lnb_ref, y_ref, st_s, *, nb):
    chunk = pl.program_id(0)
    C = RW_CHUNK
    D = RW_HEAD_DIM
    R = nb * C

    @pl.when(chunk == 0)
    def _():
        st_s[...] = jnp.zeros_like(st_s)

    x = rw_ref[...].reshape(R, RW_IN)
    rolled = pltpu.roll(x, 1, axis=0)
    first_row = _iota((C, 1), 0) == 0
    prev = []
    for b in range(nb):
        before = jnp.where(chunk == 0, 0.0, tail_ref[b, SUBLANES - 1:SUBLANES, :])
        prev.append(jnp.where(first_row, before, rolled[b * C:(b + 1) * C]))
    prev = jnp.concatenate(prev, axis=0)
    xm = x + (prev - x) * mu_ref[...]
    r = xm[:, 0:RW_W]
    k = xm[:, RW_W:2 * RW_W]
    v = xm[:, 2 * RW_W:3 * RW_W]
    o = 3 * RW_W
    wd = xm[:, o:o + RW_DECAY_LORA]
    ad = xm[:, o + RW_DECAY_LORA:o + RW_DECAY_LORA + RW_AAA_LORA]
    gd = xm[:, o + RW_DECAY_LORA + RW_AAA_LORA:]

    w_log = -_softplus(-(w0_ref[...] + _dot(jnp.tanh(wd), w2_ref[...], HI))) - 0.5
    ld = -jnp.exp(w_log)
    a = _sigmoid(a0_ref[...] + _dot(ad, a2_ref[...], HI))
    gate = _dot(_sigmoid(gd), g2_ref[...], HI)

    same_head = (_iota((RW_W, RW_W), 0) // D == _iota((RW_W, RW_W), 1) // D).astype(BF16)
    head_sum = lambda t: _dot_ones_exact(t, same_head)

    kk = k * kk_ref[...]
    kk = kk / jnp.maximum(jnp.sqrt(head_sum(kk * kk)), 1e-12)
    k2 = k * (1.0 + (a - 1.0) * ka_ref[...])
    bvec = kk * a

    tri = ((_iota((R, R), 1) <= _iota((R, R), 0)) & (_iota((R, R), 1) // C == _iota((R, R), 0) // C)).astype(F32)
    cl = _dot(tri, ld, HI)
    c_last = jnp.concatenate(
        [jnp.broadcast_to(cl[(b + 1) * C - 1:(b + 1) * C, :], (C, RW_W)) for b in range(nb)], axis=0)
    e_neg = jnp.exp(-cl)
    e_end = jnp.exp(c_last - cl)
    w_rows = []
    for b in range(nb):
        w_b = jnp.exp(cl[(b + 1) * C - 1:(b + 1) * C, :])
        t0 = w_b.astype(BF16)
        r1 = w_b - t0.astype(F32)
        t1 = r1.astype(BF16)
        t2 = (r1 - t1.astype(F32)).astype(BF16)
        w_rows.append(jnp.concatenate([t0, t1, t2, jnp.zeros((SUBLANES - 3, RW_W), BF16)], axis=0))
    w_split = jnp.concatenate(w_rows, axis=0)
    rw_ = r * jnp.exp(cl)
    kkw = kk * jnp.exp(cl - ld)
    bd = bvec * e_neg
    kd = k2 * e_neg
    bd_end = bvec * e_end
    kd_end = k2 * e_end

    strict = _iota((C, C), 1) < _iota((C, C), 0)
    incl = _iota((C, C), 1) <= _iota((C, C), 0)
    eye = _eye(C)
    eye_b = eye.astype(BF16)
    chains = [(b, h) for b in range(nb) for h in range(RW_HEADS)]
    blk = lambda t, b, h: t[b * C:(b + 1) * C, h * D:(h + 1) * D]
    bf = lambda ts: [t.astype(BF16) for t in ts]
    kkw_c = [blk(kkw, b, h) for b, h in chains]
    rw_c = [blk(rw_, b, h) for b, h in chains]
    vb_c = bf([blk(v, b, h) for b, h in chains])
    gms = [_dot_nt(jnp.concatenate([kq, rq], axis=0).astype(BF16),
                   jnp.concatenate([blk(bd, b, h), blk(kd, b, h)], axis=0).astype(BF16))
           for (b, h), kq, rq in zip(chains, kkw_c, rw_c)]
    a_ab = [jnp.where(strict, g[:C, :C], 0.0) for g in gms]
    a_ak = bf([jnp.where(strict, g[:C, C:], 0.0) for g in gms])
    a_rb = bf([jnp.where(incl, g[C:, :C], 0.0) for g in gms])
    a_rk = bf([jnp.where(incl, g[C:, C:], 0.0) for g in gms])
    zts = [_dot_nt(eye_b, jnp.concatenate([blk(bd_end, b, h).astype(BF16), blk(kd_end, b, h).astype(BF16),
                                           w_split[b * SUBLANES:(b + 1) * SUBLANES, h * D:(h + 1) * D]], axis=0))
           for b, h in chains]
    x0s = [_dot(m, vb) for m, vb in zip(a_ak, vb_c)]
    y_loc = [_dot(m, vb) for m, vb in zip(a_rk, vb_c)]
    s_loc = [_dot(zt[:, C:2 * C].astype(BF16), vb) for zt, vb in zip(zts, vb_c)]
    invs = bf(_unit_lower_inverses(a_ab, eye, C))
    tws = [_dot(inv, jnp.concatenate([x0, kq], axis=1).astype(BF16))
           for inv, x0, kq in zip(invs, x0s, kkw_c)]
    sts = [st_s[b * RW_HEADS + h] for b, h in chains]
    xss = [_dot(jnp.concatenate([tw[:, D:], rq], axis=0).astype(BF16), st.astype(BF16))
           for tw, rq, st in zip(tws, rw_c, sts)]
    ubs = bf([-tw[:, :D] - xs[:C] for tw, xs in zip(tws, xss)])
    ys = [xs[C:] + _dot(m, ub) + yl for xs, m, ub, yl in zip(xss, a_rb, ubs, y_loc)]
    for (b, h), zt, st, ub, sl_ in zip(chains, zts, sts, ubs, s_loc):
        w_col = zt[:, 2 * C:2 * C + 1] + zt[:, 2 * C + 1:2 * C + 2] + zt[:, 2 * C + 2:2 * C + 3]
        st_s[b * RW_HEADS + h] = w_col * st + _dot(zt[:, :C].astype(BF16), ub) + sl_

    y = jnp.concatenate([jnp.concatenate(ys[b * RW_HEADS:(b + 1) * RW_HEADS], axis=1) for b in range(nb)], axis=0)
    mu = head_sum(y) * (1.0 / D)
    var = head_sum(jnp.square(y - mu)) * (1.0 / D)
    yn = (y - mu) * lax.rsqrt(var + RW_LN_EPS) * lnw_ref[...] + lnb_ref[...]
    bonus = head_sum(r * k2 * rk_ref[...]) * v
    y_ref[...] = ((yn + bonus) * gate).astype(BF16).reshape(nb, C, RW_W)


def _rwkv(rw, consts, batch, seq):
    C = RW_CHUNK
    nc = seq // C
    per8 = C // SUBLANES
    rw3 = rw.reshape(batch, seq, RW_IN)
    y = pl.pallas_call(
        functools.partial(_rwkv_kernel, nb=batch),
        grid=(nc,),
        in_specs=[pl.BlockSpec((batch, C, RW_IN), lambda c: (0, c, 0)),
                  pl.BlockSpec((batch, SUBLANES, RW_IN), lambda c: (0, jnp.maximum(c * per8 - 1, 0), 0))]
                 + [_const_spec(a.shape) for a in consts],
        out_specs=pl.BlockSpec((batch, C, RW_W), lambda c: (0, c, 0)),
        out_shape=jax.ShapeDtypeStruct((batch, seq, RW_W), BF16),
        scratch_shapes=[pltpu.VMEM((batch * RW_HEADS, RW_HEAD_DIM, RW_HEAD_DIM), F32)],
        compiler_params=_params("arbitrary"),
        name="rwkv7",
    )(rw3, rw3, *consts)
    return y.reshape(batch * seq, RW_W)


def _rope_tables(seq):
    half = ROPE_DIM // 2
    inv_freq = ROPE_THETA ** (-jnp.arange(half, dtype=F32) * 2.0 / ROPE_DIM)
    ang = jnp.arange(seq, dtype=F32)[:, None] * inv_freq[None, :]
    cos, sin = jnp.cos(ang), jnp.sin(ang)
    ones = jnp.ones((seq, ATT_HEAD_DIM - ROPE_DIM), F32)
    zeros = jnp.zeros((seq, ATT_HEAD_DIM - ROPE_DIM), F32)
    zh = jnp.zeros((seq, half), F32)
    reps = LANES // ATT_HEAD_DIM
    cos_t = jnp.concatenate([cos, cos, ones] * reps, axis=1)
    sin_lo = jnp.concatenate([-sin, zh, zeros] * reps, axis=1)
    sin_hi = jnp.concatenate([zh, sin, zeros] * reps, axis=1)
    return cos_t, sin_lo, sin_hi


def _row(v):
    return v.reshape(1, -1).astype(F32)


def _col(v):
    return v.reshape(-1, 1).astype(F32)


def kernel(x, mix_norm, mlp_norm, w_up, w_down, e_w_in, e_w_out, att_sinks, ml_conv_w, ml_conv_b, ml_i_bias,
           ml_f_bias, ml_norm, o_w_in, o_w_out, ssm_conv_w, ssm_conv_b, ssm_dt_bias, ssm_a_log, ssm_d, ssm_norm,
           rw_mu, rw_w0, rw_w2, rw_a0, rw_a2, rw_g2, rw_k_k, rw_k_a, rw_r_k, rw_ln_w, rw_ln_b, final_norm):
    batch, seq, _ = x.shape
    h = x.reshape(batch * seq, D_MODEL)
    fg = _row(final_norm)

    w = e_w_in[0].astype(BF16)
    o = 0
    parts = []
    for width in (ATT_Q_W, ATT_KV_W, ATT_KV_W, 2 * ML_QK_W, ML_V_W, ML_V_W):
        parts.append(w[:, o:o + width])
        o += width
    parts.append(w[:, o:o + 2 * ML_HEADS].T)
    qa, ka, va, qkm, vm, om, gt = _inproj_even(h, _row(mix_norm[0]), _rope_tables(seq), parts, seq)
    y_att = _attention(qa, ka, va, _row(att_sinks[0]), batch, seq)
    gate_bias = _col(jnp.concatenate([ml_i_bias[0], ml_f_bias[0]]))
    y_ml = _mlstm(qkm, vm, om, gt, ml_conv_w[0].astype(F32), _row(ml_conv_b[0]), gate_bias, _row(ml_norm[0]),
                  batch, seq)
    wo = e_w_out[0].astype(BF16)
    h = _outproj_mlp(y_att, y_ml, h, wo[:ATT_Q_W], wo[ATT_Q_W:], _row(mlp_norm[0]), w_up[0].astype(BF16),
                     w_down[0].astype(BF16), fg, final_norm=False)

    w = o_w_in[0].astype(BF16)
    o_x = SSM_INNER
    o_dt = o_x + SSM_CONV_W
    o_rw = o_dt + SSM_HEADS
    parts = [w[:, :o_x], w[:, o_x:o_dt], w[:, o_dt:o_rw].T, w[:, o_rw:]]
    z, xbc, dt_t, rw = _inproj_odd(h, _row(mix_norm[1]), parts)
    y_ssm = _ssd(xbc, z, dt_t, ssm_conv_w[0].astype(F32), _row(ssm_conv_b[0]), _col(ssm_dt_bias[0]),
                 _col(ssm_a_log[0]), _row(jnp.repeat(ssm_d[0], SSM_HEAD_DIM)), _row(ssm_norm[0]), batch, seq)
    rw_consts = (_row(rw_mu[0]), _row(rw_w0[0]), rw_w2[0].astype(F32), _row(rw_a0[0]), rw_a2[0].astype(F32),
                 rw_g2[0].astype(F32), _row(rw_k_k[0]), _row(rw_k_a[0]), _row(rw_r_k[0]), _row(rw_ln_w[0]),
                 _row(rw_ln_b[0]))
    y_rw = _rwkv(rw, rw_consts, batch, seq)
    wo = o_w_out[0].astype(BF16)
    h = _outproj_mlp(y_ssm, y_rw, h, wo[:SSM_INNER], wo[SSM_INNER:], _row(mlp_norm[1]), w_up[1].astype(BF16),
                     w_down[1].astype(BF16), fg, final_norm=True)
    return h.reshape(batch, seq, D_MODEL)
```

```python
import functools

import jax
import jax.numpy as jnp
from jax import lax
from jax.experimental import pallas as pl
from jax.experimental.pallas import tpu as pltpu

F32 = jnp.float32
BF16 = jnp.bfloat16
HI = lax.Precision.HIGHEST

D_MODEL = 1024
D_FF = 4 * D_MODEL
NORM_EPS = 1e-5
CONV_TAPS = 4

ATT_HEAD_DIM = 64
ATT_Q_HEADS = 8
ATT_KV_HEADS = 2
ATT_GROUP = ATT_Q_HEADS // ATT_KV_HEADS
ATT_WINDOW = 128
ROPE_THETA = 500000.0
ROPE_DIM = 16
ATT_Q_W = ATT_Q_HEADS * ATT_HEAD_DIM
ATT_KV_W = ATT_KV_HEADS * ATT_HEAD_DIM

ML_HEADS = 4
ML_V_DIM = 128
ML_QK_DIM = 64
ML_CHUNK = 128
ML_NORM_EPS = 1e-6
ML_QK_W = ML_HEADS * ML_QK_DIM
ML_V_W = ML_HEADS * ML_V_DIM

SSM_HEAD_DIM = 64
SSM_HEADS = 8
SSM_GROUPS = 2
SSM_HEADS_PER_GROUP = SSM_HEADS // SSM_GROUPS
SSM_STATE = 128
SSM_CHUNK = 128
SSM_INNER = SSM_HEADS * SSM_HEAD_DIM
SSM_BC = SSM_GROUPS * SSM_STATE
SSM_CONV_W = SSM_INNER + 2 * SSM_BC

RW_HEAD_DIM = 64
RW_HEADS = 8
RW_W = RW_HEADS * RW_HEAD_DIM
RW_DECAY_LORA = 64
RW_AAA_LORA = 64
RW_GATE_LORA = 128
RW_IN = 3 * RW_W + RW_DECAY_LORA + RW_AAA_LORA + RW_GATE_LORA
RW_LN_EPS = 64e-5
RW_CHUNK = 64

V7X_VMEM_BYTES = 64 * 1024 * 1024
VMEM_LIMIT_BYTES = V7X_VMEM_BYTES - 16 * 1024 * 1024
SUBLANES = 8
LANES = 128

PROJ_ROWS = 512
FF_CHUNK = 512


def _params(*sem):
    return pltpu.CompilerParams(dimension_semantics=sem, vmem_limit_bytes=VMEM_LIMIT_BYTES)


def _const_spec(shape):
    nd = len(shape)
    return pl.BlockSpec(shape, lambda *_: (0,) * nd, pipeline_mode=pl.Buffered(1))


def _iota(shape, dim):
    return lax.broadcasted_iota(jnp.int32, shape, dim)


def _tri_lower(n):
    return (_iota((n, n), 1) <= _iota((n, n), 0)).astype(F32)


def _tri_upper(n):
    return (_iota((n, n), 0) <= _iota((n, n), 1)).astype(F32)


def _eye(n):
    return (_iota((n, n), 0) == _iota((n, n), 1)).astype(F32)


def _dot(a, b, precision=None):
    return jnp.dot(a, b, preferred_element_type=F32, precision=precision)


def _dot_nt(a, b, precision=None):
    return lax.dot_general(a, b, (((1,), (1,)), ((), ())), preferred_element_type=F32, precision=precision)


def _sigmoid(x):
    return 1.0 / (1.0 + jnp.exp(-x))


def _softplus(x):
    return jnp.maximum(x, 0.0) + jnp.log1p(jnp.exp(-jnp.abs(x)))


def _silu(x):
    return x * _sigmoid(x)


def _rmsnorm(x, g):
    y = x * lax.rsqrt(jnp.mean(x * x, axis=-1, keepdims=True) + NORM_EPS)
    return y * g


def _shifted_rows(x, tail, k):
    rolled = pltpu.roll(x, k, axis=0)
    tail_rolled = pltpu.roll(tail, k, axis=0)
    first = jnp.where(_iota((SUBLANES, 1), 0) < k, tail_rolled, rolled[:SUBLANES])
    return jnp.concatenate([first, rolled[SUBLANES:]], axis=0)


def _causal_conv_silu(x, tail, w, b):
    out = b + x * w[CONV_TAPS - 1:CONV_TAPS, :]
    for k in range(1, CONV_TAPS):
        out = out + _shifted_rows(x, tail, k) * w[CONV_TAPS - 1 - k:CONV_TAPS - k, :]
    return _silu(out)


def _rope(x, cos, sin_lo, sin_hi, reps):
    w = x.shape[1]
    c = jnp.concatenate([cos] * reps, axis=1)
    s_lo = jnp.concatenate([sin_lo] * reps, axis=1)
    s_hi = jnp.concatenate([sin_hi] * reps, axis=1)
    half = ROPE_DIM // 2
    nxt = pltpu.roll(x, w - half, axis=1)
    prv = pltpu.roll(x, half, axis=1)
    return x * c + nxt * s_lo + prv * s_hi


def _inproj_even_kernel(x_ref, g_ref, cos_ref, slo_ref, shi_ref, wq_ref, wk_ref, wv_ref, wqk_ref, wvm_ref,
                        wom_ref, wgt_ref, qa_ref, ka_ref, va_ref, qkm_ref, vm_ref, om_ref, gt_ref):
    hb = _rmsnorm(x_ref[...], g_ref[...]).astype(BF16)
    cos, s_lo, s_hi = cos_ref[...], slo_ref[...], shi_ref[...]
    q = _rope(_dot(hb, wq_ref[...]), cos, s_lo, s_hi, ATT_Q_W // LANES)
    qa_ref[...] = (q * (ATT_HEAD_DIM ** -0.5)).astype(BF16)
    ka_ref[...] = _rope(_dot(hb, wk_ref[...]), cos, s_lo, s_hi, ATT_KV_W // LANES).astype(BF16)
    va_ref[...] = _dot(hb, wv_ref[...]).astype(BF16)
    qkm_ref[...] = _dot(hb, wqk_ref[...])
    vm_ref[...] = _dot(hb, wvm_ref[...]).astype(BF16)
    om_ref[...] = _dot(hb, wom_ref[...])
    gt_ref[...] = _dot_nt(wgt_ref[...], hb)


def _inproj_even(x2, g, tables, w, seq):
    n = x2.shape[0]
    tm = PROJ_ROWS
    per_seq = seq // tm
    row = lambda width: pl.BlockSpec((tm, width), lambda i: (i, 0))
    tab = pl.BlockSpec((tm, LANES), lambda i: (i % per_seq, 0))
    wq, wk, wv, wqk, wvm, wom, wgt = w
    out_shapes = (
        jax.ShapeDtypeStruct((n, ATT_Q_W), BF16), jax.ShapeDtypeStruct((n, ATT_KV_W), BF16),
        jax.ShapeDtypeStruct((n, ATT_KV_W), BF16), jax.ShapeDtypeStruct((n, 2 * ML_QK_W), F32),
        jax.ShapeDtypeStruct((n, ML_V_W), BF16), jax.ShapeDtypeStruct((n, ML_V_W), F32),
        jax.ShapeDtypeStruct((n // seq, 2 * ML_HEADS, seq), F32))
    return pl.pallas_call(
        _inproj_even_kernel,
        grid=(n // tm,),
        in_specs=[row(D_MODEL), _const_spec((1, D_MODEL)), tab, tab, tab] + [_const_spec(a.shape) for a in w],
        out_specs=(row(ATT_Q_W), row(ATT_KV_W), row(ATT_KV_W), row(2 * ML_QK_W), row(ML_V_W), row(ML_V_W),
                   pl.BlockSpec((None, 2 * ML_HEADS, tm), lambda i: (i // per_seq, 0, i % per_seq))),
        out_shape=out_shapes,
        compiler_params=_params("arbitrary"),
        name="inproj_even",
    )(x2, g, *tables, wq, wk, wv, wqk, wvm, wom, wgt)


def _attention_kernel(q_ref, kc_ref, kp_ref, vc_ref, vp_ref, sink_ref, o_ref, *, nb):
    blk = pl.program_id(0)
    w = ATT_WINDOW
    dh = ATT_HEAD_DIM
    G = ATT_GROUP
    i = _iota((G * w, 2 * w), 0) % w
    j = _iota((G * w, 2 * w), 1)
    mask = (j > i) & (j <= i + w) & ((j >= w) | (blk > 0))
    sinks = sink_ref[...]
    problems = [(b, g) for b in range(nb) for g in range(ATT_KV_HEADS)]
    scores, values, sink_cols = [], [], []
    for b, g in problems:
        q = q_ref[b]
        qs = jnp.concatenate([q[:, (g * G + a) * dh:(g * G + a + 1) * dh] for a in range(G)], axis=0)
        ks = jnp.concatenate([kp_ref[b][:, g * dh:(g + 1) * dh], kc_ref[b][:, g * dh:(g + 1) * dh]], axis=0)
        values.append(jnp.concatenate([vp_ref[b][:, g * dh:(g + 1) * dh], vc_ref[b][:, g * dh:(g + 1) * dh]], axis=0))
        scores.append(_dot_nt(qs, ks))
        sink_cols.append(jnp.concatenate(
            [jnp.broadcast_to(sinks[:, g * G + a:g * G + a + 1], (w, 1)) for a in range(G)], axis=0))
    probs, dens = [], []
    for s, sink in zip(scores, sink_cols):
        s = jnp.where(mask, s, -jnp.inf)
        m = jnp.maximum(jnp.max(s, axis=-1, keepdims=True), sink)
        p = jnp.exp(s - m)
        dens.append(jnp.sum(p, axis=-1, keepdims=True) + jnp.exp(sink - m))
        probs.append(p.astype(BF16))
    outs = [_dot(p, v) / d for p, v, d in zip(probs, values, dens)]
    for b in range(nb):
        heads = [outs[b * ATT_KV_HEADS + g][a * w:(a + 1) * w] for g in range(ATT_KV_HEADS) for a in range(G)]
        o_ref[b] = jnp.concatenate(heads, axis=1).astype(BF16)


def _attention(qa, ka, va, sinks, batch, seq):
    w = ATT_WINDOW
    nblk = seq // w
    three = lambda t: t.reshape(batch, seq, t.shape[-1])
    cur = lambda width: pl.BlockSpec((batch, w, width), lambda j: (0, j, 0))
    prev = lambda width: pl.BlockSpec((batch, w, width), lambda j: (0, jnp.maximum(j - 1, 0), 0))
    q3, k3, v3 = three(qa), three(ka), three(va)
    y = pl.pallas_call(
        functools.partial(_attention_kernel, nb=batch),
        grid=(nblk,),
        in_specs=[cur(ATT_Q_W), cur(ATT_KV_W), prev(ATT_KV_W), cur(ATT_KV_W), prev(ATT_KV_W),
                  _const_spec((1, ATT_Q_HEADS))],
        out_specs=cur(ATT_Q_W),
        out_shape=jax.ShapeDtypeStruct((batch, seq, ATT_Q_W), BF16),
        compiler_params=_params("arbitrary"),
        name="swa_attention",
    )(q3, k3, k3, v3, v3, sinks)
    return y.reshape(batch * seq, ATT_Q_W)


def _mlstm_kernel(qk_ref, tail_ref, v_ref, o_ref, gt_ref, cw_ref, cb_ref, gb_ref, nw_ref, y_ref, c_s, n_s, m_s,
                  *, nb):
    chunk = pl.program_id(0)
    L = ML_CHUNK
    H = ML_HEADS
    NG = 2 * H

    @pl.when(chunk == 0)
    def _():
        c_s[...] = jnp.zeros_like(c_s)
        n_s[...] = jnp.zeros_like(n_s)
        m_s[...] = jnp.zeros_like(m_s)

    cw, cb = cw_ref[...], cb_ref[...]
    qk = [_causal_conv_silu(qk_ref[b], jnp.where(chunk == 0, 0.0, tail_ref[b]), cw, cb) for b in range(nb)]

    g = jnp.concatenate([gt_ref[b] + gb_ref[...] for b in range(nb)], axis=0)
    lg = jnp.where(_iota(g.shape, 0) % NG < H, g, -_softplus(-g))
    tri_l, tri_u, eye = _tri_lower(L), _tri_upper(L), _eye(L)
    cum_row = _dot(lg, tri_u, HI)
    lg_col = _dot_nt(eye, lg, HI)
    cum_col = _dot_nt(tri_l, lg, HI)
    causal = _iota((L, L), 1) <= _iota((L, L), 0)

    chains = [(b, h) for b in range(nb) for h in range(H)]
    idx = lambda b, h: b * H + h
    q_c = [qk[b][:, h * ML_QK_DIM:(h + 1) * ML_QK_DIM] for b, h in chains]
    k_c = [qk[b][:, ML_QK_W + h * ML_QK_DIM:ML_QK_W + (h + 1) * ML_QK_DIM] * (ML_QK_DIM ** -0.5) for b, h in chains]
    v_c = [v_ref[b][:, h * ML_V_DIM:(h + 1) * ML_V_DIM] for b, h in chains]
    qb_c = [q.astype(BF16) for q in q_c]
    c_prev = [c_s[idx(b, h)] for b, h in chains]
    n_prev = [n_s[idx(b, h):idx(b, h) + 1, :] for b, h in chains]
    m_prev = [m_s[idx(b, h):idx(b, h) + 1, 0:1] for b, h in chains]

    s_qk = [_dot_nt(qb, k.astype(BF16)) for qb, k in zip(qb_c, k_c)]
    inter = [_dot(qb, c.astype(BF16)) for qb, c in zip(qb_c, c_prev)]

    gi_c = [b * NG + h for b, h in chains]
    b_col = [cum_col[:, gi + H:gi + H + 1] for gi in gi_c]
    b_last = [bc[L - 1:L, :] for bc in b_col]
    dlog = [jnp.where(causal, bc - cum_row[gi + H:gi + H + 1, :] + lg[gi:gi + 1, :], -jnp.inf)
            for gi, bc in zip(gi_c, b_col)]
    row_max = [jnp.max(d, axis=1, keepdims=True) for d in dlog]
    a_loc = [bl - bc + lg_col[:, gi:gi + 1] for gi, bc, bl in zip(gi_c, b_col, b_last)]
    m_loc = [jnp.max(a, axis=0, keepdims=True) for a in a_loc]
    inter_log = [bc + mp for bc, mp in zip(b_col, m_prev)]
    m_t = [jnp.maximum(il, rm) for il, rm in zip(inter_log, row_max)]
    qk_w = [s * jnp.exp(d - mt) for s, d, mt in zip(s_qk, dlog, m_t)]
    w_inter = [jnp.exp(il - mt) for il, mt in zip(inter_log, m_t)]
    kw = [k * jnp.exp(a - ml) for k, a, ml in zip(k_c, a_loc, m_loc)]

    intra = [_dot(w.astype(BF16), v) for w, v in zip(qk_w, v_c)]
    c_loc = [lax.dot_general(w.astype(BF16), v, (((0,), (0,)), ((), ())), preferred_element_type=F32)
             for w, v in zip(kw, v_c)]

    n_ch = len(chains)
    sum_qk = [jnp.sum(w, axis=1, keepdims=True) for w in qk_w]
    sum_qn = [jnp.sum(q * n, axis=1, keepdims=True) for q, n in zip(q_c, n_prev)]
    xs = [(intra[i] + w_inter[i] * inter[i])
          / jnp.maximum(jnp.abs(sum_qk[i] + w_inter[i] * sum_qn[i]), jnp.exp(-m_t[i])) for i in range(n_ch)]
    mus = [jnp.mean(x, axis=-1, keepdims=True) for x in xs]
    cen = [x - mu for x, mu in zip(xs, mus)]
    var = [jnp.mean(jnp.square(c), axis=-1, keepdims=True) for c in cen]
    outs = [c * lax.rsqrt(vr + ML_NORM_EPS) for c, vr in zip(cen, var)]
    for i, (b, h) in enumerate(chains):
        m_new = jnp.maximum(b_last[i] + m_prev[i], m_loc[i])
        fa = jnp.exp(b_last[i] + m_prev[i] - m_new)
        fb = jnp.exp(m_loc[i] - m_new)
        c_s[idx(b, h)] = fa * c_prev[i] + fb * c_loc[i]
        n_s[idx(b, h):idx(b, h) + 1, :] = fa * n_prev[i] + fb * jnp.sum(kw[i], axis=0, keepdims=True)
        m_s[idx(b, h):idx(b, h) + 1, 0:1] = m_new

    nw = nw_ref[...]
    for b in range(nb):
        hm = jnp.concatenate(outs[b * H:(b + 1) * H], axis=1) * nw
        y_ref[b] = (hm * _sigmoid(o_ref[b])).astype(BF16)


def _mlstm(qkm, vm, om, gt, conv_w, conv_b, gate_bias, norm_w, batch, seq):
    L = ML_CHUNK
    nc = seq // L
    per8 = L // SUBLANES
    three = lambda t: t.reshape(batch, seq, t.shape[-1])
    cur = lambda width: pl.BlockSpec((batch, L, width), lambda c: (0, c, 0))
    tail = pl.BlockSpec((batch, SUBLANES, 2 * ML_QK_W), lambda c: (0, jnp.maximum(c * per8 - 1, 0), 0))
    qk3 = three(qkm)
    y = pl.pallas_call(
        functools.partial(_mlstm_kernel, nb=batch),
        grid=(nc,),
        in_specs=[cur(2 * ML_QK_W), tail, cur(ML_V_W), cur(ML_V_W),
                  pl.BlockSpec((batch, 2 * ML_HEADS, L), lambda c: (0, 0, c)),
                  _const_spec(conv_w.shape), _const_spec(conv_b.shape), _const_spec(gate_bias.shape),
                  _const_spec(norm_w.shape)],
        out_specs=cur(ML_V_W),
        out_shape=jax.ShapeDtypeStruct((batch, seq, ML_V_W), BF16),
        scratch_shapes=[pltpu.VMEM((batch * ML_HEADS, ML_QK_DIM, ML_V_DIM), F32),
                        pltpu.VMEM((batch * ML_HEADS, ML_QK_DIM), F32),
                        pltpu.VMEM((batch * ML_HEADS, LANES), F32)],
        compiler_params=_params("arbitrary"),
        name="mlstm",
    )(qk3, qk3, three(vm), three(om), gt, conv_w, conv_b, gate_bias, norm_w)
    return y.reshape(batch * seq, ML_V_W)


def _outproj_mlp_kernel(ya_ref, yb_ref, h_ref, wa_ref, wb_ref, g_ref, wu_ref, wd_ref, fg_ref, o_ref,
                        *, final_norm):
    h1 = h_ref[...] + _dot(ya_ref[...], wa_ref[...]) + _dot(yb_ref[...], wb_ref[...])
    hb = _rmsnorm(h1, g_ref[...]).astype(BF16)
    mlp = None
    for c in range(D_FF // FF_CHUNK):
        u = _dot(hb, wu_ref[:, c * FF_CHUNK:(c + 1) * FF_CHUNK])
        a = jnp.square(jnp.maximum(u, 0.0)).astype(BF16)
        d = _dot(a, wd_ref[c * FF_CHUNK:(c + 1) * FF_CHUNK, :])
        mlp = d if mlp is None else mlp + d
    h2 = h1 + mlp
    if final_norm:
        h2 = _rmsnorm(h2, fg_ref[...])
    o_ref[...] = h2


def _outproj_mlp(ya, yb, h, wa, wb, g, wu, wd, fg, final_norm):
    n = h.shape[0]
    tm = PROJ_ROWS
    row = lambda width: pl.BlockSpec((tm, width), lambda i: (i, 0))
    return pl.pallas_call(
        functools.partial(_outproj_mlp_kernel, final_norm=final_norm),
        grid=(n // tm,),
        in_specs=[row(ya.shape[1]), row(yb.shape[1]), row(D_MODEL)]
                 + [_const_spec(a.shape) for a in (wa, wb, g, wu, wd, fg)],
        out_specs=row(D_MODEL),
        out_shape=jax.ShapeDtypeStruct((n, D_MODEL), F32),
        compiler_params=_params("arbitrary"),
        name="outproj_mlp_final" if final_norm else "outproj_mlp",
    )(ya, yb, h, wa, wb, g, wu, wd, fg)


def _inproj_odd_kernel(x_ref, g_ref, wz_ref, wx_ref, wdt_ref, wrw_ref, z_ref, xbc_ref, dt_ref, rw_ref):
    hb = _rmsnorm(x_ref[...], g_ref[...]).astype(BF16)
    z_ref[...] = _dot(hb, wz_ref[...])
    xbc_ref[...] = _dot(hb, wx_ref[...])
    dt_ref[...] = _dot_nt(wdt_ref[...], hb)
    rw_ref[...] = _dot(hb, wrw_ref[...])


def _inproj_odd(x2, g, w, seq):
    n = x2.shape[0]
    tm = PROJ_ROWS
    per_seq = seq // tm
    row = lambda width: pl.BlockSpec((tm, width), lambda i: (i, 0))
    out_shapes = (jax.ShapeDtypeStruct((n, SSM_INNER), F32), jax.ShapeDtypeStruct((n, SSM_CONV_W), F32),
                  jax.ShapeDtypeStruct((n // seq, SSM_HEADS, seq), F32), jax.ShapeDtypeStruct((n, RW_IN), F32))
    return pl.pallas_call(
        _inproj_odd_kernel,
        grid=(n // tm,),
        in_specs=[row(D_MODEL), _const_spec((1, D_MODEL))] + [_const_spec(a.shape) for a in w],
        out_specs=(row(SSM_INNER), row(SSM_CONV_W),
                   pl.BlockSpec((None, SSM_HEADS, tm), lambda i: (i // per_seq, 0, i % per_seq)), row(RW_IN)),
        out_shape=out_shapes,
        compiler_params=_params("arbitrary"),
        name="inproj_odd",
    )(x2, g, *w)


def _ssd_kernel(xbc_ref, tail_ref, z_ref, dt_ref, cw_ref, cb_ref, dtb_ref, alog_ref, d_ref, nw_ref, y_ref, s_s,
                *, nb):
    chunk = pl.program_id(0)
    L = SSM_CHUNK
    P = SSM_HEAD_DIM
    HG = SSM_HEADS_PER_GROUP
    NH = SSM_HEADS

    @pl.when(chunk == 0)
    def _():
        s_s[...] = jnp.zeros_like(s_s)

    cw, cb_ = cw_ref[...], cb_ref[...]
    xbc = [_causal_conv_silu(xbc_ref[b], jnp.where(chunk == 0, 0.0, tail_ref[b]), cw, cb_) for b in range(nb)]

    dt_row = _softplus(jnp.concatenate([dt_ref[b] + dtb_ref[...] for b in range(nb)], axis=0))
    neg_a = -jnp.exp(jnp.concatenate([alog_ref[...]] * nb, axis=0))
    a_row = dt_row * neg_a
    tri_l, tri_u, eye = _tri_lower(L), _tri_upper(L), _eye(L)
    acum_row = _dot(a_row, tri_u, HI)
    acum_col = _dot_nt(tri_l, a_row, HI)
    dt_col = _dot_nt(eye, dt_row, HI)
    causal = _iota((L, L), 1) <= _iota((L, L), 0)
    eye_b = eye.astype(BF16)

    groups = [(b, g) for b in range(nb) for g in range(SSM_GROUPS)]
    gidx = lambda b, g: b * SSM_GROUPS + g
    bg = [xbc[b][:, SSM_INNER + g * SSM_STATE:SSM_INNER + (g + 1) * SSM_STATE].astype(BF16) for b, g in groups]
    cg = [xbc[b][:, SSM_INNER + SSM_BC + g * SSM_STATE:SSM_INNER + SSM_BC + (g + 1) * SSM_STATE].astype(BF16)
          for b, g in groups]
    s_prev = [s_s[gidx(b, g)] for b, g in groups]

    cb = [_dot_nt(c, bq) for c, bq in zip(cg, bg)]
    bg_t = [_dot_nt(eye_b, bq).astype(BF16) for bq in bg]
    y_off = [_dot(c, sp.astype(BF16)) for c, sp in zip(cg, s_prev)]

    heads = [(b, g, j) for b, g in groups for j in range(HG)]
    scores, xcs, e_col, xws, cds = [], [], [], [], []
    for b, g, j in heads:
        h = g * HG + j
        col = b * NH + h
        a_col = acum_col[:, col:col + 1]
        a_last = a_col[L - 1:L, :]
        decay = jnp.exp(jnp.where(causal, a_col - acum_row[col:col + 1, :], -jnp.inf))
        xc = xbc[b][:, h * P:(h + 1) * P] * dt_col[:, col:col + 1]
        scores.append((cb[gidx(b, g)] * decay).astype(BF16))
        xcs.append(xc.astype(BF16))
        e_col.append(jnp.exp(a_col))
        xws.append(xc * jnp.exp(a_last - a_col))
        cds.append(jnp.broadcast_to(jnp.exp(a_last), (1, P)))

    y_diag = [_dot(sc, xc) for sc, xc in zip(scores, xcs)]
    states = [_dot(bg_t[i], jnp.concatenate(xws[i * HG:(i + 1) * HG], axis=1).astype(BF16))
              for i in range(len(groups))]

    for i, (b, g) in enumerate(groups):
        s_s[gidx(b, g)] = s_prev[i] * jnp.concatenate(cds[i * HG:(i + 1) * HG], axis=1) + states[i]

    gw = SSM_INNER // SSM_GROUPS
    for b in range(nb):
        ys = []
        for g in range(SSM_GROUPS):
            for j in range(HG):
                i = (b * SSM_GROUPS + g) * HG + j
                ys.append(y_diag[i] + y_off[gidx(b, g)][:, j * P:(j + 1) * P] * e_col[i])
        xs = xbc[b][:, :SSM_INNER]
        y = jnp.concatenate(ys, axis=1) + d_ref[...] * xs
        y = y * _silu(z_ref[b])
        normed = []
        for g in range(SSM_GROUPS):
            yg = y[:, g * gw:(g + 1) * gw]
            normed.append(yg * lax.rsqrt(jnp.mean(yg * yg, axis=-1, keepdims=True) + NORM_EPS))
        y_ref[b] = (jnp.concatenate(normed, axis=1) * nw_ref[...]).astype(BF16)


def _ssd(xbc, z, dt_t, conv_w, conv_b, dt_bias, a_log, d_row, norm_w, batch, seq):
    L = SSM_CHUNK
    nc = seq // L
    per8 = L // SUBLANES
    three = lambda t: t.reshape(batch, seq, t.shape[-1])
    cur = lambda width: pl.BlockSpec((batch, L, width), lambda c: (0, c, 0))
    tail = pl.BlockSpec((batch, SUBLANES, SSM_CONV_W), lambda c: (0, jnp.maximum(c * per8 - 1, 0), 0))
    consts = (conv_w, conv_b, dt_bias, a_log, d_row, norm_w)
    x3 = three(xbc)
    y = pl.pallas_call(
        functools.partial(_ssd_kernel, nb=batch),
        grid=(nc,),
        in_specs=[cur(SSM_CONV_W), tail, cur(SSM_INNER),
                  pl.BlockSpec((batch, SSM_HEADS, L), lambda c: (0, 0, c))]
                 + [_const_spec(a.shape) for a in consts],
        out_specs=cur(SSM_INNER),
        out_shape=jax.ShapeDtypeStruct((batch, seq, SSM_INNER), BF16),
        scratch_shapes=[pltpu.VMEM((batch * SSM_GROUPS, SSM_STATE, SSM_HEADS_PER_GROUP * SSM_HEAD_DIM), F32)],
        compiler_params=_params("arbitrary"),
        name="ssd",
    )(x3, x3, three(z), dt_t, *consts)
    return y.reshape(batch * seq, SSM_INNER)


def _split_bf16(x):
    hi = x.astype(BF16)
    lo = (x - hi.astype(F32)).astype(BF16)
    return hi, lo


def _dot_ones_exact(t, ones_matrix):
    hi, lo = _split_bf16(t)
    return _dot(hi, ones_matrix) + _dot(lo, ones_matrix)


RW_NEWTON_EXACT_STEPS = 1


def _unit_lower_inverses(mats, eye, n):
    splits = [_split_bf16(a) for a in mats]
    xbs = [(eye - a).astype(BF16) for a in mats]
    xs = None
    steps = 0
    terms = 2
    while terms < n:
        terms *= 2
        steps += 1
    for it in range(steps):
        if it < steps - RW_NEWTON_EXACT_STEPS:
            prods = [_dot(a_hi, xb) for (a_hi, _), xb in zip(splits, xbs)]
        else:
            prods = [_dot(a_hi, xb) + _dot(a_lo, xb) for (a_hi, a_lo), xb in zip(splits, xbs)]
        resids = [(eye - xb.astype(F32) - p).astype(BF16) for xb, p in zip(xbs, prods)]
        xs = [xb.astype(F32) + _dot(xb, e) for xb, e in zip(xbs, resids)]
        xbs = [x.astype(BF16) for x in xs]
    return xs


def _rwkv_kernel(rw_ref, tail_ref, mu_ref, w0_ref, w2_ref, a0_ref, a2_ref, g2_ref, kk_ref, ka_ref, rk_ref,
                 lnw_ref, lnb_ref, y_ref, st_s, *, nb):
    chunk = pl.program_id(0)
    C = RW_CHUNK
    D = RW_HEAD_DIM
    R = nb * C

    @pl.when(chunk == 0)
    def _():
        st_s[...] = jnp.zeros_like(st_s)

    x = rw_ref[...].reshape(R, RW_IN)
    rolled = pltpu.roll(x, 1, axis=0)
    first_row = _iota((C, 1), 0) == 0
    prev = []
    for b in range(nb):
        before = jnp.where(chunk == 0, 0.0, tail_ref[b, SUBLANES - 1:SUBLANES, :])
        prev.append(jnp.where(first_row, before, rolled[b * C:(b + 1) * C]))
    prev = jnp.concatenate(prev, axis=0)
    xm = x + (prev - x) * mu_ref[...]
    r = xm[:, 0:RW_W]
    k = xm[:, RW_W:2 * RW_W]
    v = xm[:, 2 * RW_W:3 * RW_W]
    o = 3 * RW_W
    wd = xm[:, o:o + RW_DECAY_LORA]
    ad = xm[:, o + RW_DECAY_LORA:o + RW_DECAY_LORA + RW_AAA_LORA]
    gd = xm[:, o + RW_DECAY_LORA + RW_AAA_LORA:]

    w_log = -_softplus(-(w0_ref[...] + _dot(jnp.tanh(wd), w2_ref[...], HI))) - 0.5
    ld = -jnp.exp(w_log)
    a = _sigmoid(a0_ref[...] + _dot(ad, a2_ref[...], HI))
    gate = _dot(_sigmoid(gd), g2_ref[...], HI)

    same_head = (_iota((RW_W, RW_W), 0) // D == _iota((RW_W, RW_W), 1) // D).astype(BF16)
    head_sum = lambda t: _dot_ones_exact(t, same_head)

    kk = k * kk_ref[...]
    kk = kk / jnp.maximum(jnp.sqrt(head_sum(kk * kk)), 1e-12)
    k2 = k * (1.0 + (a - 1.0) * ka_ref[...])
    bvec = kk * a

    tri = ((_iota((R, R), 1) <= _iota((R, R), 0)) & (_iota((R, R), 1) // C == _iota((R, R), 0) // C)).astype(F32)
    cl = _dot(tri, ld, HI)
    c_last = jnp.concatenate(
        [jnp.broadcast_to(cl[(b + 1) * C - 1:(b + 1) * C, :], (C, RW_W)) for b in range(nb)], axis=0)
    e_neg = jnp.exp(-cl)
    e_end = jnp.exp(c_last - cl)
    w_rows = []
    for b in range(nb):
        w_b = jnp.exp(cl[(b + 1) * C - 1:(b + 1) * C, :])
        t0 = w_b.astype(BF16)
        r1 = w_b - t0.astype(F32)
        t1 = r1.astype(BF16)
        t2 = (r1 - t1.astype(F32)).astype(BF16)
        w_rows.append(jnp.concatenate([t0, t1, t2, jnp.zeros((SUBLANES - 3, RW_W), BF16)], axis=0))
    w_split = jnp.concatenate(w_rows, axis=0)
    rw_ = r * jnp.exp(cl)
    kkw = kk * jnp.exp(cl - ld)
    bd = bvec * e_neg
    kd = k2 * e_neg
    bd_end = bvec * e_end
    kd_end = k2 * e_end

    strict = _iota((C, C), 1) < _iota((C, C), 0)
    incl = _iota((C, C), 1) <= _iota((C, C), 0)
    eye = _eye(C)
    eye_b = eye.astype(BF16)
    chains = [(b, h) for b in range(nb) for h in range(RW_HEADS)]
    blk = lambda t, b, h: t[b * C:(b + 1) * C, h * D:(h + 1) * D]
    bf = lambda ts: [t.astype(BF16) for t in ts]
    kkw_c = [blk(kkw, b, h) for b, h in chains]
    rw_c = [blk(rw_, b, h) for b, h in chains]
    vb_c = bf([blk(v, b, h) for b, h in chains])
    gms = [_dot_nt(jnp.concatenate([kq, rq], axis=0).astype(BF16),
                   jnp.concatenate([blk(bd, b, h), blk(kd, b, h)], axis=0).astype(BF16))
           for (b, h), kq, rq in zip(chains, kkw_c, rw_c)]
    a_ab = [jnp.where(strict, g[:C, :C], 0.0) for g in gms]
    a_ak = bf([jnp.where(strict, g[:C, C:], 0.0) for g in gms])
    a_rb = bf([jnp.where(incl, g[C:, :C], 0.0) for g in gms])
    a_rk = bf([jnp.where(incl, g[C:, C:], 0.0) for g in gms])
    zts = [_dot_nt(eye_b, jnp.concatenate([blk(bd_end, b, h).astype(BF16), blk(kd_end, b, h).astype(BF16),
                                           w_split[b * SUBLANES:(b + 1) * SUBLANES, h * D:(h + 1) * D]], axis=0))
           for b, h in chains]
    x0s = [_dot(m, vb) for m, vb in zip(a_ak, vb_c)]
    y_loc = [_dot(m, vb) for m, vb in zip(a_rk, vb_c)]
    s_loc = [_dot(zt[:, C:2 * C].astype(BF16), vb) for zt, vb in zip(zts, vb_c)]
    invs = bf(_unit_lower_inverses(a_ab, eye, C))
    tws = [_dot(inv, jnp.concatenate([x0, kq], axis=1).astype(BF16))
           for inv, x0, kq in zip(invs, x0s, kkw_c)]
    sts = [st_s[b * RW_HEADS + h] for b, h in chains]
    xss = [_dot(jnp.concatenate([tw[:, D:], rq], axis=0).astype(BF16), st.astype(BF16))
           for tw, rq, st in zip(tws, rw_c, sts)]
    ubs = bf([-tw[:, :D] - xs[:C] for tw, xs in zip(tws, xss)])
    ys = [xs[C:] + _dot(m, ub) + yl for xs, m, ub, yl in zip(xss, a_rb, ubs, y_loc)]
    for (b, h), zt, st, ub, sl_ in zip(chains, zts, sts, ubs, s_loc):
        w_col = zt[:, 2 * C:2 * C + 1] + zt[:, 2 * C + 1:2 * C + 2] + zt[:, 2 * C + 2:2 * C + 3]
        st_s[b * RW_HEADS + h] = w_col * st + _dot(zt[:, :C].astype(BF16), ub) + sl_

    y = jnp.concatenate([jnp.concatenate(ys[b * RW_HEADS:(b + 1) * RW_HEADS], axis=1) for b in range(nb)], axis=0)
    mu = head_sum(y) * (1.0 / D)
    var = head_sum(jnp.square(y - mu)) * (1.0 / D)
    yn = (y - mu) * lax.rsqrt(var + RW_LN_EPS) * lnw_ref[...] + lnb_ref[...]
    bonus = head_sum(r * k2 * rk_ref[...]) * v
    y_ref[...] = ((yn + bonus) * gate).astype(BF16).reshape(nb, C, RW_W)


def _rwkv(rw, consts, batch, seq):
    C = RW_CHUNK
    nc = seq // C
    per8 = C // SUBLANES
    rw3 = rw.reshape(batch, seq, RW_IN)
    y = pl.pallas_call(
        functools.partial(_rwkv_kernel, nb=batch),
        grid=(nc,),
        in_specs=[pl.BlockSpec((batch, C, RW_IN), lambda c: (0, c, 0)),
                  pl.BlockSpec((batch, SUBLANES, RW_IN), lambda c: (0, jnp.maximum(c * per8 - 1, 0), 0))]
                 + [_const_spec(a.shape) for a in consts],
        out_specs=pl.BlockSpec((batch, C, RW_W), lambda c: (0, c, 0)),
        out_shape=jax.ShapeDtypeStruct((batch, seq, RW_W), BF16),
        scratch_shapes=[pltpu.VMEM((batch * RW_HEADS, RW_HEAD_DIM, RW_HEAD_DIM), F32)],
        compiler_params=_params("arbitrary"),
        name="rwkv7",
    )(rw3, rw3, *consts)
    return y.reshape(batch * seq, RW_W)


def _rope_tables(seq):
    half = ROPE_DIM // 2
    inv_freq = ROPE_THETA ** (-jnp.arange(half, dtype=F32) * 2.0 / ROPE_DIM)
    ang = jnp.arange(seq, dtype=F32)[:, None] * inv_freq[None, :]
    cos, sin = jnp.cos(ang), jnp.sin(ang)
    ones = jnp.ones((seq, ATT_HEAD_DIM - ROPE_DIM), F32)
    zeros = jnp.zeros((seq, ATT_HEAD_DIM - ROPE_DIM), F32)
    zh = jnp.zeros((seq, half), F32)
    reps = LANES // ATT_HEAD_DIM
    cos_t = jnp.concatenate([cos, cos, ones] * reps, axis=1)
    sin_lo = jnp.concatenate([-sin, zh, zeros] * reps, axis=1)
    sin_hi = jnp.concatenate([zh, sin, zeros] * reps, axis=1)
    return cos_t, sin_lo, sin_hi


def _row(v):
    return v.reshape(1, -1).astype(F32)


def _col(v):
    return v.reshape(-1, 1).astype(F32)


def kernel(x, mix_norm, mlp_norm, w_up, w_down, e_w_in, e_w_out, att_sinks, ml_conv_w, ml_conv_b, ml_i_bias,
           ml_f_bias, ml_norm, o_w_in, o_w_out, ssm_conv_w, ssm_conv_b, ssm_dt_bias, ssm_a_log, ssm_d, ssm_norm,
           rw_mu, rw_w0, rw_w2, rw_a0, rw_a2, rw_g2, rw_k_k, rw_k_a, rw_r_k, rw_ln_w, rw_ln_b, final_norm):
    batch, seq, _ = x.shape
    h = x.reshape(batch * seq, D_MODEL)
    fg = _row(final_norm)

    w = e_w_in[0].astype(BF16)
    o = 0
    parts = []
    for width in (ATT_Q_W, ATT_KV_W, ATT_KV_W, 2 * ML_QK_W, ML_V_W, ML_V_W):
        parts.append(w[:, o:o + width])
        o += width
    parts.append(w[:, o:o + 2 * ML_HEADS].T)
    qa, ka, va, qkm, vm, om, gt = _inproj_even(h, _row(mix_norm[0]), _rope_tables(seq), parts, seq)
    y_att = _attention(qa, ka, va, _row(att_sinks[0]), batch, seq)
    gate_bias = _col(jnp.concatenate([ml_i_bias[0], ml_f_bias[0]]))
    y_ml = _mlstm(qkm, vm, om, gt, ml_conv_w[0].astype(F32), _row(ml_conv_b[0]), gate_bias, _row(ml_norm[0]),
                  batch, seq)
    wo = e_w_out[0].astype(BF16)
    h = _outproj_mlp(y_att, y_ml, h, wo[:ATT_Q_W], wo[ATT_Q_W:], _row(mlp_norm[0]), w_up[0].astype(BF16),
                     w_down[0].astype(BF16), fg, final_norm=False)

    w = o_w_in[0].astype(BF16)
    o_x = SSM_INNER
    o_dt = o_x + SSM_CONV_W
    o_rw = o_dt + SSM_HEADS
    parts = [w[:, :o_x], w[:, o_x:o_dt], w[:, o_dt:o_rw].T, w[:, o_rw:]]
    z, xbc, dt_t, rw = _inproj_odd(h, _row(mix_norm[1]), parts, seq)
    y_ssm = _ssd(xbc, z, dt_t, ssm_conv_w[0].astype(F32), _row(ssm_conv_b[0]), _col(ssm_dt_bias[0]),
                 _col(ssm_a_log[0]), _row(jnp.repeat(ssm_d[0], SSM_HEAD_DIM)), _row(ssm_norm[0]), batch, seq)
    rw_consts = (_row(rw_mu[0]), _row(rw_w0[0]), rw_w2[0].astype(F32), _row(rw_a0[0]), rw_a2[0].astype(F32),
                 rw_g2[0].astype(F32), _row(rw_k_k[0]), _row(rw_k_a[0]), _row(rw_r_k[0]), _row(rw_ln_w[0]),
                 _row(rw_ln_b[0]))
    y_rw = _rwkv(rw, rw_consts, batch, seq)
    wo = o_w_out[0].astype(BF16)
    h = _outproj_mlp(y_ssm, y_rw, h, wo[:SSM_INNER], wo[SSM_INNER:], _row(mlp_norm[1]), w_up[1].astype(BF16),
                     w_down[1].astype(BF16), fg, final_norm=True)
    return h.reshape(batch, seq, D_MODEL)
```

```python
import functools

import numpy as np

import jax
import jax.numpy as jnp
from jax import lax
from jax.experimental import pallas as pl
from jax.experimental.pallas import tpu as pltpu

F32 = jnp.float32
BF16 = jnp.bfloat16
HI = lax.Precision.HIGHEST

D_MODEL = 1024
D_FF = 4 * D_MODEL
NORM_EPS = 1e-5
CONV_TAPS = 4

ATT_HEAD_DIM = 64
ATT_Q_HEADS = 8
ATT_KV_HEADS = 2
ATT_GROUP = ATT_Q_HEADS // ATT_KV_HEADS
ATT_WINDOW = 128
ROPE_THETA = 500000.0
ROPE_DIM = 16
ATT_Q_W = ATT_Q_HEADS * ATT_HEAD_DIM
ATT_KV_W = ATT_KV_HEADS * ATT_HEAD_DIM

ML_HEADS = 4
ML_V_DIM = 128
ML_QK_DIM = 64
ML_CHUNK = 128
ML_NORM_EPS = 1e-6
ML_QK_W = ML_HEADS * ML_QK_DIM
ML_V_W = ML_HEADS * ML_V_DIM

SSM_HEAD_DIM = 64
SSM_HEADS = 8
SSM_GROUPS = 2
SSM_HEADS_PER_GROUP = SSM_HEADS // SSM_GROUPS
SSM_STATE = 128
SSM_CHUNK = 128
SSM_INNER = SSM_HEADS * SSM_HEAD_DIM
SSM_BC = SSM_GROUPS * SSM_STATE
SSM_CONV_W = SSM_INNER + 2 * SSM_BC

RW_HEAD_DIM = 64
RW_HEADS = 8
RW_W = RW_HEADS * RW_HEAD_DIM
RW_DECAY_LORA = 64
RW_AAA_LORA = 64
RW_GATE_LORA = 128
RW_IN = 3 * RW_W + RW_DECAY_LORA + RW_AAA_LORA + RW_GATE_LORA
RW_LN_EPS = 64e-5
RW_CHUNK = 64

V7X_VMEM_BYTES = 64 * 1024 * 1024
VMEM_LIMIT_BYTES = V7X_VMEM_BYTES - 16 * 1024 * 1024
SUBLANES = 8
LANES = 128

PROJ_ROWS = 512
FF_CHUNK = 512


def _params(*sem):
    return pltpu.CompilerParams(dimension_semantics=sem, vmem_limit_bytes=VMEM_LIMIT_BYTES)


def _const_spec(shape):
    nd = len(shape)
    return pl.BlockSpec(shape, lambda *_: (0,) * nd, pipeline_mode=pl.Buffered(1))


def _iota(shape, dim):
    return lax.broadcasted_iota(jnp.int32, shape, dim)


def _tri_lower(n):
    return (_iota((n, n), 1) <= _iota((n, n), 0)).astype(F32)


def _tri_upper(n):
    return (_iota((n, n), 0) <= _iota((n, n), 1)).astype(F32)


def _eye(n):
    return (_iota((n, n), 0) == _iota((n, n), 1)).astype(F32)


def _dot(a, b, precision=None):
    return jnp.dot(a, b, preferred_element_type=F32, precision=precision)


def _dot_nt(a, b, precision=None):
    return lax.dot_general(a, b, (((1,), (1,)), ((), ())), preferred_element_type=F32, precision=precision)


def _sigmoid(x):
    return 1.0 / (1.0 + jnp.exp(-x))


def _softplus(x):
    return jnp.maximum(x, 0.0) + jnp.log1p(jnp.exp(-jnp.abs(x)))


def _silu(x):
    return x * _sigmoid(x)


def _rmsnorm(x, g):
    y = x * lax.rsqrt(jnp.mean(x * x, axis=-1, keepdims=True) + NORM_EPS)
    return y * g


def _shifted_rows(x, tail, k):
    rolled = pltpu.roll(x, k, axis=0)
    tail_rolled = pltpu.roll(tail, k, axis=0)
    first = jnp.where(_iota((SUBLANES, 1), 0) < k, tail_rolled, rolled[:SUBLANES])
    return jnp.concatenate([first, rolled[SUBLANES:]], axis=0)


def _causal_conv_silu(x, tail, w, b):
    out = b + x * w[CONV_TAPS - 1:CONV_TAPS, :]
    for k in range(1, CONV_TAPS):
        out = out + _shifted_rows(x, tail, k) * w[CONV_TAPS - 1 - k:CONV_TAPS - k, :]
    return _silu(out)


def _rope(x, cos, sin_lo, sin_hi, reps):
    w = x.shape[1]
    c = jnp.concatenate([cos] * reps, axis=1)
    s_lo = jnp.concatenate([sin_lo] * reps, axis=1)
    s_hi = jnp.concatenate([sin_hi] * reps, axis=1)
    half = ROPE_DIM // 2
    nxt = pltpu.roll(x, w - half, axis=1)
    prv = pltpu.roll(x, half, axis=1)
    return x * c + nxt * s_lo + prv * s_hi


def _inproj_even_kernel(x_ref, g_ref, cos_ref, slo_ref, shi_ref, wq_ref, wk_ref, wv_ref, wqk_ref, wvm_ref,
                        wom_ref, wgt_ref, qa_ref, ka_ref, va_ref, qkm_ref, vm_ref, om_ref, gt_ref):
    hb = _rmsnorm(x_ref[...], g_ref[...]).astype(BF16)
    cos, s_lo, s_hi = cos_ref[...], slo_ref[...], shi_ref[...]
    q = _rope(_dot(hb, wq_ref[...]), cos, s_lo, s_hi, ATT_Q_W // LANES)
    qa_ref[...] = (q * (ATT_HEAD_DIM ** -0.5)).astype(BF16)
    ka_ref[...] = _rope(_dot(hb, wk_ref[...]), cos, s_lo, s_hi, ATT_KV_W // LANES).astype(BF16)
    va_ref[...] = _dot(hb, wv_ref[...]).astype(BF16)
    qkm_ref[...] = _dot(hb, wqk_ref[...])
    vm_ref[...] = _dot(hb, wvm_ref[...]).astype(BF16)
    om_ref[...] = _dot(hb, wom_ref[...])
    gt_ref[...] = _dot_nt(wgt_ref[...], hb)


def _inproj_even(x2, g, tables, w, seq):
    n = x2.shape[0]
    tm = PROJ_ROWS
    per_seq = seq // tm
    row = lambda width: pl.BlockSpec((tm, width), lambda i: (i, 0))
    tab = pl.BlockSpec((tm, LANES), lambda i: (i % per_seq, 0))
    wq, wk, wv, wqk, wvm, wom, wgt = w
    out_shapes = (
        jax.ShapeDtypeStruct((n, ATT_Q_W), BF16), jax.ShapeDtypeStruct((n, ATT_KV_W), BF16),
        jax.ShapeDtypeStruct((n, ATT_KV_W), BF16), jax.ShapeDtypeStruct((n, 2 * ML_QK_W), F32),
        jax.ShapeDtypeStruct((n, ML_V_W), BF16), jax.ShapeDtypeStruct((n, ML_V_W), F32),
        jax.ShapeDtypeStruct((n // seq, 2 * ML_HEADS, seq), F32))
    return pl.pallas_call(
        _inproj_even_kernel,
        grid=(n // tm,),
        in_specs=[row(D_MODEL), _const_spec((1, D_MODEL)), tab, tab, tab] + [_const_spec(a.shape) for a in w],
        out_specs=(row(ATT_Q_W), row(ATT_KV_W), row(ATT_KV_W), row(2 * ML_QK_W), row(ML_V_W), row(ML_V_W),
                   pl.BlockSpec((None, 2 * ML_HEADS, tm), lambda i: (i // per_seq, 0, i % per_seq))),
        out_shape=out_shapes,
        compiler_params=_params("arbitrary"),
        name="inproj_even",
    )(x2, g, *tables, wq, wk, wv, wqk, wvm, wom, wgt)


def _attention_kernel(q_ref, kc_ref, kp_ref, vc_ref, vp_ref, sink_ref, o_ref, *, nb):
    blk = pl.program_id(0)
    w = ATT_WINDOW
    dh = ATT_HEAD_DIM
    G = ATT_GROUP
    i = _iota((G * w, 2 * w), 0) % w
    j = _iota((G * w, 2 * w), 1)
    mask = (j > i) & (j <= i + w) & ((j >= w) | (blk > 0))
    sinks = sink_ref[...]
    problems = [(b, g) for b in range(nb) for g in range(ATT_KV_HEADS)]
    scores, values, sink_cols = [], [], []
    for b, g in problems:
        q = q_ref[b]
        qs = jnp.concatenate([q[:, (g * G + a) * dh:(g * G + a + 1) * dh] for a in range(G)], axis=0)
        ks = jnp.concatenate([kp_ref[b][:, g * dh:(g + 1) * dh], kc_ref[b][:, g * dh:(g + 1) * dh]], axis=0)
        values.append(jnp.concatenate([vp_ref[b][:, g * dh:(g + 1) * dh], vc_ref[b][:, g * dh:(g + 1) * dh]], axis=0))
        scores.append(_dot_nt(qs, ks))
        sink_cols.append(jnp.concatenate(
            [jnp.broadcast_to(sinks[:, g * G + a:g * G + a + 1], (w, 1)) for a in range(G)], axis=0))
    probs, dens = [], []
    for s, sink in zip(scores, sink_cols):
        s = jnp.where(mask, s, -jnp.inf)
        m = jnp.maximum(jnp.max(s, axis=-1, keepdims=True), sink)
        p = jnp.exp(s - m)
        dens.append(jnp.sum(p, axis=-1, keepdims=True) + jnp.exp(sink - m))
        probs.append(p.astype(BF16))
    outs = [_dot(p, v) / d for p, v, d in zip(probs, values, dens)]
    for b in range(nb):
        heads = [outs[b * ATT_KV_HEADS + g][a * w:(a + 1) * w] for g in range(ATT_KV_HEADS) for a in range(G)]
        o_ref[b] = jnp.concatenate(heads, axis=1).astype(BF16)


def _attention(qa, ka, va, sinks, batch, seq):
    w = ATT_WINDOW
    nblk = seq // w
    three = lambda t: t.reshape(batch, seq, t.shape[-1])
    cur = lambda width: pl.BlockSpec((batch, w, width), lambda j: (0, j, 0))
    prev = lambda width: pl.BlockSpec((batch, w, width), lambda j: (0, jnp.maximum(j - 1, 0), 0))
    q3, k3, v3 = three(qa), three(ka), three(va)
    y = pl.pallas_call(
        functools.partial(_attention_kernel, nb=batch),
        grid=(nblk,),
        in_specs=[cur(ATT_Q_W), cur(ATT_KV_W), prev(ATT_KV_W), cur(ATT_KV_W), prev(ATT_KV_W),
                  _const_spec((1, ATT_Q_HEADS))],
        out_specs=cur(ATT_Q_W),
        out_shape=jax.ShapeDtypeStruct((batch, seq, ATT_Q_W), BF16),
        compiler_params=_params("arbitrary"),
        name="swa_attention",
    )(q3, k3, k3, v3, v3, sinks)
    return y.reshape(batch * seq, ATT_Q_W)


def _mlstm_kernel(qk_ref, tail_ref, v_ref, o_ref, gt_ref, cw_ref, cb_ref, gb_ref, nw_ref, y_ref, c_s, m_s,
                  *, nb):
    chunk = pl.program_id(0)
    L = ML_CHUNK
    H = ML_HEADS
    DV = ML_V_DIM
    NG = 2 * H

    @pl.when(chunk == 0)
    def _():
        c_s[...] = jnp.zeros_like(c_s)
        m_s[...] = jnp.zeros_like(m_s)

    cw, cb = cw_ref[...], cb_ref[...]
    qk = [_causal_conv_silu(qk_ref[b], jnp.where(chunk == 0, 0.0, tail_ref[b]), cw, cb) for b in range(nb)]

    g = jnp.concatenate([gt_ref[b] + gb_ref[...] for b in range(nb)], axis=0)
    lg = jnp.where(_iota(g.shape, 0) % NG < H, g, -_softplus(-g))
    tri_l, tri_u, eye = _tri_lower(L), _tri_upper(L), _eye(L)
    cum_row = _dot(lg, tri_u, HI)
    lg_col = _dot_nt(eye, lg, HI)
    cum_col = _dot_nt(tri_l, lg, HI)
    causal = _iota((L, L), 1) <= _iota((L, L), 0)

    chains = [(b, h) for b in range(nb) for h in range(H)]
    idx = lambda b, h: b * H + h
    q_c = [qk[b][:, h * ML_QK_DIM:(h + 1) * ML_QK_DIM] for b, h in chains]
    k_c = [qk[b][:, ML_QK_W + h * ML_QK_DIM:ML_QK_W + (h + 1) * ML_QK_DIM] * (ML_QK_DIM ** -0.5) for b, h in chains]
    ones = jnp.ones((L, DV), BF16)
    v_c = [jnp.concatenate([v_ref[b][:, h * DV:(h + 1) * DV], ones], axis=1) for b, h in chains]
    qb_c = [q.astype(BF16) for q in q_c]
    c_prev = [c_s[idx(b, h)] for b, h in chains]
    m_prev = [m_s[idx(b, h):idx(b, h) + 1, 0:1] for b, h in chains]

    s_qk = [_dot_nt(qb, k.astype(BF16)) for qb, k in zip(qb_c, k_c)]
    inter = [_dot(qb, c.astype(BF16)) for qb, c in zip(qb_c, c_prev)]

    gi_c = [b * NG + h for b, h in chains]
    b_col = [cum_col[:, gi + H:gi + H + 1] for gi in gi_c]
    b_last = [bc[L - 1:L, :] for bc in b_col]
    dlog = [jnp.where(causal, bc - cum_row[gi + H:gi + H + 1, :] + lg[gi:gi + 1, :], -jnp.inf)
            for gi, bc in zip(gi_c, b_col)]
    row_max = [jnp.max(d, axis=1, keepdims=True) for d in dlog]
    a_loc = [bl - bc + lg_col[:, gi:gi + 1] for gi, bc, bl in zip(gi_c, b_col, b_last)]
    m_loc = [jnp.max(a, axis=0, keepdims=True) for a in a_loc]
    inter_log = [bc + mp for bc, mp in zip(b_col, m_prev)]
    m_t = [jnp.maximum(il, rm) for il, rm in zip(inter_log, row_max)]
    qk_w = [s * jnp.exp(d - mt) for s, d, mt in zip(s_qk, dlog, m_t)]
    w_inter = [jnp.exp(il - mt) for il, mt in zip(inter_log, m_t)]
    kw = [k * jnp.exp(a - ml) for k, a, ml in zip(k_c, a_loc, m_loc)]

    intra = [_dot(w.astype(BF16), v) for w, v in zip(qk_w, v_c)]
    c_loc = [lax.dot_general(w.astype(BF16), v, (((0,), (0,)), ((), ())), preferred_element_type=F32)
             for w, v in zip(kw, v_c)]

    both = [a + w * e for a, w, e in zip(intra, w_inter, inter)]
    xs = [nd[:, :DV] / jnp.maximum(jnp.abs(nd[:, DV:DV + 1]), jnp.exp(-mt)) for nd, mt in zip(both, m_t)]
    mus = [jnp.mean(x, axis=-1, keepdims=True) for x in xs]
    cen = [x - mu for x, mu in zip(xs, mus)]
    var = [jnp.mean(jnp.square(c), axis=-1, keepdims=True) for c in cen]
    outs = [c * lax.rsqrt(vr + ML_NORM_EPS) for c, vr in zip(cen, var)]
    for i, (b, h) in enumerate(chains):
        m_new = jnp.maximum(b_last[i] + m_prev[i], m_loc[i])
        fa = jnp.exp(b_last[i] + m_prev[i] - m_new)
        fb = jnp.exp(m_loc[i] - m_new)
        c_s[idx(b, h)] = fa * c_prev[i] + fb * c_loc[i]
        m_s[idx(b, h):idx(b, h) + 1, 0:1] = m_new

    nw = nw_ref[...]
    for b in range(nb):
        hm = jnp.concatenate(outs[b * H:(b + 1) * H], axis=1) * nw
        y_ref[b] = (hm * _sigmoid(o_ref[b])).astype(BF16)


def _mlstm(qkm, vm, om, gt, conv_w, conv_b, gate_bias, norm_w, batch, seq):
    L = ML_CHUNK
    nc = seq // L
    per8 = L // SUBLANES
    three = lambda t: t.reshape(batch, seq, t.shape[-1])
    cur = lambda width: pl.BlockSpec((batch, L, width), lambda c: (0, c, 0))
    tail = pl.BlockSpec((batch, SUBLANES, 2 * ML_QK_W), lambda c: (0, jnp.maximum(c * per8 - 1, 0), 0))
    qk3 = three(qkm)
    y = pl.pallas_call(
        functools.partial(_mlstm_kernel, nb=batch),
        grid=(nc,),
        in_specs=[cur(2 * ML_QK_W), tail, cur(ML_V_W), cur(ML_V_W),
                  pl.BlockSpec((batch, 2 * ML_HEADS, L), lambda c: (0, 0, c)),
                  _const_spec(conv_w.shape), _const_spec(conv_b.shape), _const_spec(gate_bias.shape),
                  _const_spec(norm_w.shape)],
        out_specs=cur(ML_V_W),
        out_shape=jax.ShapeDtypeStruct((batch, seq, ML_V_W), BF16),
        scratch_shapes=[pltpu.VMEM((batch * ML_HEADS, ML_QK_DIM, 2 * ML_V_DIM), F32),
                        pltpu.VMEM((batch * ML_HEADS, LANES), F32)],
        compiler_params=_params("arbitrary"),
        name="mlstm",
    )(qk3, qk3, three(vm), three(om), gt, conv_w, conv_b, gate_bias, norm_w)
    return y.reshape(batch * seq, ML_V_W)


def _outproj_mlp_kernel(ya_ref, yb_ref, h_ref, wa_ref, wb_ref, g_ref, wu_ref, wd_ref, fg_ref, o_ref,
                        *, final_norm):
    h1 = h_ref[...] + _dot(ya_ref[...], wa_ref[...]) + _dot(yb_ref[...], wb_ref[...])
    hb = _rmsnorm(h1, g_ref[...]).astype(BF16)
    mlp = None
    for c in range(D_FF // FF_CHUNK):
        u = _dot(hb, wu_ref[:, c * FF_CHUNK:(c + 1) * FF_CHUNK])
        a = jnp.square(jnp.maximum(u, 0.0)).astype(BF16)
        d = _dot(a, wd_ref[c * FF_CHUNK:(c + 1) * FF_CHUNK, :])
        mlp = d if mlp is None else mlp + d
    h2 = h1 + mlp
    if final_norm:
        h2 = _rmsnorm(h2, fg_ref[...])
    o_ref[...] = h2


def _outproj_mlp(ya, yb, h, wa, wb, g, wu, wd, fg, final_norm):
    n = h.shape[0]
    tm = PROJ_ROWS
    row = lambda width: pl.BlockSpec((tm, width), lambda i: (i, 0))
    return pl.pallas_call(
        functools.partial(_outproj_mlp_kernel, final_norm=final_norm),
        grid=(n // tm,),
        in_specs=[row(ya.shape[1]), row(yb.shape[1]), row(D_MODEL)]
                 + [_const_spec(a.shape) for a in (wa, wb, g, wu, wd, fg)],
        out_specs=row(D_MODEL),
        out_shape=jax.ShapeDtypeStruct((n, D_MODEL), F32),
        compiler_params=_params("arbitrary"),
        name="outproj_mlp_final" if final_norm else "outproj_mlp",
    )(ya, yb, h, wa, wb, g, wu, wd, fg)


def _inproj_odd_kernel(x_ref, g_ref, wz_ref, wx_ref, wdt_ref, wrw_ref, z_ref, xbc_ref, dt_ref, rw_ref):
    hb = _rmsnorm(x_ref[...], g_ref[...]).astype(BF16)
    z_ref[...] = _dot(hb, wz_ref[...])
    xbc_ref[...] = _dot(hb, wx_ref[...])
    dt_ref[...] = _dot_nt(wdt_ref[...], hb)
    rw_ref[...] = _dot(hb, wrw_ref[...])


def _inproj_odd(x2, g, w, seq):
    n = x2.shape[0]
    tm = PROJ_ROWS
    per_seq = seq // tm
    row = lambda width: pl.BlockSpec((tm, width), lambda i: (i, 0))
    out_shapes = (jax.ShapeDtypeStruct((n, SSM_INNER), F32), jax.ShapeDtypeStruct((n, SSM_CONV_W), F32),
                  jax.ShapeDtypeStruct((n // seq, SSM_HEADS, seq), F32), jax.ShapeDtypeStruct((n, RW_IN), F32))
    return pl.pallas_call(
        _inproj_odd_kernel,
        grid=(n // tm,),
        in_specs=[row(D_MODEL), _const_spec((1, D_MODEL))] + [_const_spec(a.shape) for a in w],
        out_specs=(row(SSM_INNER), row(SSM_CONV_W),
                   pl.BlockSpec((None, SSM_HEADS, tm), lambda i: (i // per_seq, 0, i % per_seq)), row(RW_IN)),
        out_shape=out_shapes,
        compiler_params=_params("arbitrary"),
        name="inproj_odd",
    )(x2, g, *w)


def _ssd_kernel(xbc_ref, tail_ref, z_ref, dt_ref, cw_ref, cb_ref, dtb_ref, alog_ref, d_ref, nw_ref, y_ref, s_s,
                *, nb):
    chunk = pl.program_id(0)
    L = SSM_CHUNK
    P = SSM_HEAD_DIM
    HG = SSM_HEADS_PER_GROUP
    NH = SSM_HEADS

    @pl.when(chunk == 0)
    def _():
        s_s[...] = jnp.zeros_like(s_s)

    cw, cb_ = cw_ref[...], cb_ref[...]
    xbc = [_causal_conv_silu(xbc_ref[b], jnp.where(chunk == 0, 0.0, tail_ref[b]), cw, cb_) for b in range(nb)]

    dt_row = _softplus(jnp.concatenate([dt_ref[b] + dtb_ref[...] for b in range(nb)], axis=0))
    neg_a = -jnp.exp(jnp.concatenate([alog_ref[...]] * nb, axis=0))
    a_row = dt_row * neg_a
    tri_l, tri_u, eye = _tri_lower(L), _tri_upper(L), _eye(L)
    acum_row = _dot(a_row, tri_u, HI)
    acum_col = _dot_nt(tri_l, a_row, HI)
    dt_col = _dot_nt(eye, dt_row, HI)
    causal = _iota((L, L), 1) <= _iota((L, L), 0)
    eye_b = eye.astype(BF16)

    groups = [(b, g) for b in range(nb) for g in range(SSM_GROUPS)]
    gidx = lambda b, g: b * SSM_GROUPS + g
    bg = [xbc[b][:, SSM_INNER + g * SSM_STATE:SSM_INNER + (g + 1) * SSM_STATE].astype(BF16) for b, g in groups]
    cg = [xbc[b][:, SSM_INNER + SSM_BC + g * SSM_STATE:SSM_INNER + SSM_BC + (g + 1) * SSM_STATE].astype(BF16)
          for b, g in groups]
    s_prev = [s_s[gidx(b, g)] for b, g in groups]

    cb = [_dot_nt(c, bq) for c, bq in zip(cg, bg)]
    bg_t = [_dot_nt(eye_b, bq).astype(BF16) for bq in bg]
    y_off = [_dot(c, sp.astype(BF16)) for c, sp in zip(cg, s_prev)]

    heads = [(b, g, j) for b, g in groups for j in range(HG)]
    scores, xcs, e_col, xws, cds = [], [], [], [], []
    for b, g, j in heads:
        h = g * HG + j
        col = b * NH + h
        a_col = acum_col[:, col:col + 1]
        a_last = a_col[L - 1:L, :]
        decay = jnp.exp(jnp.where(causal, a_col - acum_row[col:col + 1, :], -jnp.inf))
        xc = xbc[b][:, h * P:(h + 1) * P] * dt_col[:, col:col + 1]
        scores.append((cb[gidx(b, g)] * decay).astype(BF16))
        xcs.append(xc.astype(BF16))
        e_col.append(jnp.exp(a_col))
        xws.append(xc * jnp.exp(a_last - a_col))
        cds.append(jnp.broadcast_to(jnp.exp(a_last), (1, P)))

    y_diag = [_dot(sc, xc) for sc, xc in zip(scores, xcs)]
    states = [_dot(bg_t[i], jnp.concatenate(xws[i * HG:(i + 1) * HG], axis=1).astype(BF16))
              for i in range(len(groups))]

    for i, (b, g) in enumerate(groups):
        s_s[gidx(b, g)] = s_prev[i] * jnp.concatenate(cds[i * HG:(i + 1) * HG], axis=1) + states[i]

    gw = SSM_INNER // SSM_GROUPS
    for b in range(nb):
        ys = []
        for g in range(SSM_GROUPS):
            for j in range(HG):
                i = (b * SSM_GROUPS + g) * HG + j
                ys.append(y_diag[i] + y_off[gidx(b, g)][:, j * P:(j + 1) * P] * e_col[i])
        xs = xbc[b][:, :SSM_INNER]
        y = jnp.concatenate(ys, axis=1) + d_ref[...] * xs
        y = y * _silu(z_ref[b])
        normed = []
        for g in range(SSM_GROUPS):
            yg = y[:, g * gw:(g + 1) * gw]
            normed.append(yg * lax.rsqrt(jnp.mean(yg * yg, axis=-1, keepdims=True) + NORM_EPS))
        y_ref[b] = (jnp.concatenate(normed, axis=1) * nw_ref[...]).astype(BF16)


def _ssd(xbc, z, dt_t, conv_w, conv_b, dt_bias, a_log, d_row, norm_w, batch, seq):
    L = SSM_CHUNK
    nc = seq // L
    per8 = L // SUBLANES
    three = lambda t: t.reshape(batch, seq, t.shape[-1])
    cur = lambda width: pl.BlockSpec((batch, L, width), lambda c: (0, c, 0))
    tail = pl.BlockSpec((batch, SUBLANES, SSM_CONV_W), lambda c: (0, jnp.maximum(c * per8 - 1, 0), 0))
    consts = (conv_w, conv_b, dt_bias, a_log, d_row, norm_w)
    x3 = three(xbc)
    y = pl.pallas_call(
        functools.partial(_ssd_kernel, nb=batch),
        grid=(nc,),
        in_specs=[cur(SSM_CONV_W), tail, cur(SSM_INNER),
                  pl.BlockSpec((batch, SSM_HEADS, L), lambda c: (0, 0, c))]
                 + [_const_spec(a.shape) for a in consts],
        out_specs=cur(SSM_INNER),
        out_shape=jax.ShapeDtypeStruct((batch, seq, SSM_INNER), BF16),
        scratch_shapes=[pltpu.VMEM((batch * SSM_GROUPS, SSM_STATE, SSM_HEADS_PER_GROUP * SSM_HEAD_DIM), F32)],
        compiler_params=_params("arbitrary"),
        name="ssd",
    )(x3, x3, three(z), dt_t, *consts)
    return y.reshape(batch * seq, SSM_INNER)


def _split_bf16(x):
    hi = x.astype(BF16)
    lo = (x - hi.astype(F32)).astype(BF16)
    return hi, lo


def _dot_ones_exact(t, ones_matrix):
    hi, lo = _split_bf16(t)
    return _dot(hi, ones_matrix) + _dot(lo, ones_matrix)


def _unit_lower_inverses(mats, eye, n):
    xbs = [(eye - a.astype(F32)).astype(BF16) for a in mats]
    steps = 0
    terms = 2
    while terms < n:
        terms *= 2
        steps += 1
    for _ in range(steps):
        xfs = [xb.astype(F32) for xb in xbs]
        resids = [(eye - xf - _dot(a, xb)).astype(BF16) for a, xb, xf in zip(mats, xbs, xfs)]
        xbs = [(xf + _dot(xb, e)).astype(BF16) for xb, xf, e in zip(xbs, xfs, resids)]
    return xbs


def _rwkv_kernel(rw_ref, tail_ref, mu_ref, w0_ref, w2_ref, a0_ref, a2_ref, g2_ref, kk_ref, ka_ref, rk_ref,
                 lnw_ref, lnb_ref, y_ref, st_s, *, nb):
    chunk = pl.program_id(0)
    C = RW_CHUNK
    D = RW_HEAD_DIM
    R = nb * C

    @pl.when(chunk == 0)
    def _():
        st_s[...] = jnp.zeros_like(st_s)

    x = rw_ref[...].reshape(R, RW_IN)
    rolled = pltpu.roll(x, 1, axis=0)
    first_row = _iota((C, 1), 0) == 0
    prev = []
    for b in range(nb):
        before = jnp.where(chunk == 0, 0.0, tail_ref[b, SUBLANES - 1:SUBLANES, :])
        prev.append(jnp.where(first_row, before, rolled[b * C:(b + 1) * C]))
    prev = jnp.concatenate(prev, axis=0)
    xm = x + (prev - x) * mu_ref[...]
    r = xm[:, 0:RW_W]
    k = xm[:, RW_W:2 * RW_W]
    v = xm[:, 2 * RW_W:3 * RW_W]
    o = 3 * RW_W
    wd = xm[:, o:o + RW_DECAY_LORA]
    ad = xm[:, o + RW_DECAY_LORA:o + RW_DECAY_LORA + RW_AAA_LORA]
    gd = xm[:, o + RW_DECAY_LORA + RW_AAA_LORA:]

    w_log = -_softplus(-(w0_ref[...] + _dot(jnp.tanh(wd).astype(BF16), w2_ref[...]))) - 0.5
    ld = -jnp.exp(w_log)
    a = _sigmoid(a0_ref[...] + _dot(ad.astype(BF16), a2_ref[...]))
    gate = _dot(_sigmoid(gd).astype(BF16), g2_ref[...])

    half = RW_W // 2
    same_head = (_iota((half, half), 0) // D == _iota((half, half), 1) // D).astype(BF16)
    head_sum = lambda t: jnp.concatenate(
        [_dot_ones_exact(t[:, :half], same_head), _dot_ones_exact(t[:, half:], same_head)], axis=1)

    kk = k * kk_ref[...]
    kk = kk / jnp.maximum(jnp.sqrt(head_sum(kk * kk)), 1e-12)
    k2 = k * (1.0 + (a - 1.0) * ka_ref[...])
    bvec = kk * a

    tri = ((_iota((R, R), 1) <= _iota((R, R), 0)) & (_iota((R, R), 1) // C == _iota((R, R), 0) // C)).astype(BF16)
    ld_hi, ld_rest = _split_bf16(ld)
    ld_mid = ld - ld_hi.astype(F32)
    ld_lo = (ld_mid - ld_rest.astype(F32)).astype(BF16)
    cl = _dot(tri, ld_hi) + (_dot(tri, ld_rest) + _dot(tri, ld_lo))
    c_last = jnp.concatenate(
        [jnp.broadcast_to(cl[(b + 1) * C - 1:(b + 1) * C, :], (C, RW_W)) for b in range(nb)], axis=0)
    e_neg = jnp.exp(-cl)
    e_end = jnp.exp(c_last - cl)
    w_end = [jnp.exp(cl[(b + 1) * C - 1:(b + 1) * C, :]) for b in range(nb)]
    rw_ = r * jnp.exp(cl)
    kkw = kk * jnp.exp(cl - ld)
    bd = bvec * e_neg
    kd = k2 * e_neg
    bd_end = bvec * e_end
    kd_end = k2 * e_end

    g_row = _iota((2 * C, 2 * C), 0)
    g_col = _iota((2 * C, 2 * C), 1) % C
    keep = g_col < jnp.where(g_row < C, g_row, g_row - C + 1)
    eye = _eye(C)
    chains = [(b, h) for b in range(nb) for h in range(RW_HEADS)]
    blk = lambda t, b, h: t[b * C:(b + 1) * C, h * D:(h + 1) * D]
    kkw_c = [blk(kkw, b, h).astype(BF16) for b, h in chains]
    rw_c = [blk(rw_, b, h).astype(BF16) for b, h in chains]
    vb_c = [blk(v, b, h).astype(BF16) for b, h in chains]
    gms = [jnp.where(keep, _dot_nt(jnp.concatenate([kq, rq], axis=0),
                                   jnp.concatenate([blk(bd, b, h), blk(kd, b, h)], axis=0).astype(BF16)),
                     0.0).astype(BF16)
           for (b, h), kq, rq in zip(chains, kkw_c, rw_c)]
    a_v = [_dot(jnp.concatenate([g[:C, C:], g[C:, C:]], axis=0), vb) for g, vb in zip(gms, vb_c)]
    invs = _unit_lower_inverses([g[:C, :C] for g in gms], eye, C)
    tws = [_dot(inv, jnp.concatenate([av[:C].astype(BF16), kq], axis=1))
           for inv, av, kq in zip(invs, a_v, kkw_c)]
    sts = [st_s[b * RW_HEADS + h] for b, h in chains]
    xss = [_dot_nt(jnp.concatenate([tw[:, D:].astype(BF16), rq], axis=0), st.astype(BF16))
           for tw, rq, st in zip(tws, rw_c, sts)]
    ubs = [(-tw[:, :D] - xs[:C]).astype(BF16) for tw, xs in zip(tws, xss)]
    ys = [xs[C:] + _dot(g[C:, :C], ub) + av[C:] for xs, g, ub, av in zip(xss, gms, ubs, a_v)]
    for (b, h), st, ub, vb in zip(chains, sts, ubs, vb_c):
        ends = jnp.concatenate([blk(bd_end, b, h), blk(kd_end, b, h)], axis=0).astype(BF16)
        upd = lax.dot_general(jnp.concatenate([ub, vb], axis=0), ends, (((0,), (0,)), ((), ())),
                              preferred_element_type=F32)
        st_s[b * RW_HEADS + h] = st * w_end[b][:, h * D:(h + 1) * D] + upd

    y = jnp.concatenate([jnp.concatenate(ys[b * RW_HEADS:(b + 1) * RW_HEADS], axis=1) for b in range(nb)], axis=0)
    mu = head_sum(y) * (1.0 / D)
    var = head_sum(jnp.square(y - mu)) * (1.0 / D)
    yn = (y - mu) * lax.rsqrt(var + RW_LN_EPS) * lnw_ref[...] + lnb_ref[...]
    bonus = head_sum(r * k2 * rk_ref[...]) * v
    y_ref[...] = ((yn + bonus) * gate).astype(BF16).reshape(nb, C, RW_W)


def _rwkv(rw, consts, batch, seq):
    C = RW_CHUNK
    nc = seq // C
    per8 = C // SUBLANES
    rw3 = rw.reshape(batch, seq, RW_IN)
    y = pl.pallas_call(
        functools.partial(_rwkv_kernel, nb=batch),
        grid=(nc,),
        in_specs=[pl.BlockSpec((batch, C, RW_IN), lambda c: (0, c, 0)),
                  pl.BlockSpec((batch, SUBLANES, RW_IN), lambda c: (0, jnp.maximum(c * per8 - 1, 0), 0))]
                 + [_const_spec(a.shape) for a in consts],
        out_specs=pl.BlockSpec((batch, C, RW_W), lambda c: (0, c, 0)),
        out_shape=jax.ShapeDtypeStruct((batch, seq, RW_W), BF16),
        scratch_shapes=[pltpu.VMEM((batch * RW_HEADS, RW_HEAD_DIM, RW_HEAD_DIM), F32)],
        compiler_params=_params("arbitrary"),
        name="rwkv7",
    )(rw3, rw3, *consts)
    return y.reshape(batch * seq, RW_W)


@functools.lru_cache(maxsize=None)
def _rope_tables(seq):
    half = ROPE_DIM // 2
    inv_freq = ROPE_THETA ** (-np.arange(half, dtype=np.float64) * 2.0 / ROPE_DIM)
    ang = np.arange(seq, dtype=np.float64)[:, None] * inv_freq[None, :]
    cos, sin = np.cos(ang), np.sin(ang)
    ones = np.ones((seq, ATT_HEAD_DIM - ROPE_DIM))
    zeros = np.zeros((seq, ATT_HEAD_DIM - ROPE_DIM))
    zh = np.zeros((seq, half))
    reps = LANES // ATT_HEAD_DIM
    cos_t = np.concatenate([cos, cos, ones] * reps, axis=1).astype(np.float32)
    sin_lo = np.concatenate([-sin, zh, zeros] * reps, axis=1).astype(np.float32)
    sin_hi = np.concatenate([zh, sin, zeros] * reps, axis=1).astype(np.float32)
    return cos_t, sin_lo, sin_hi


def _row(v):
    return v.reshape(1, -1).astype(F32)


def _col(v):
    return v.reshape(-1, 1).astype(F32)


def kernel(x, mix_norm, mlp_norm, w_up, w_down, e_w_in, e_w_out, att_sinks, ml_conv_w, ml_conv_b, ml_i_bias,
           ml_f_bias, ml_norm, o_w_in, o_w_out, ssm_conv_w, ssm_conv_b, ssm_dt_bias, ssm_a_log, ssm_d, ssm_norm,
           rw_mu, rw_w0, rw_w2, rw_a0, rw_a2, rw_g2, rw_k_k, rw_k_a, rw_r_k, rw_ln_w, rw_ln_b, final_norm):
    batch, seq, _ = x.shape
    h = x.reshape(batch * seq, D_MODEL)
    fg = _row(final_norm)

    w = e_w_in[0].astype(BF16)
    o = 0
    parts = []
    for width in (ATT_Q_W, ATT_KV_W, ATT_KV_W, 2 * ML_QK_W, ML_V_W, ML_V_W):
        parts.append(w[:, o:o + width])
        o += width
    parts.append(w[:, o:o + 2 * ML_HEADS].T)
    qa, ka, va, qkm, vm, om, gt = _inproj_even(h, _row(mix_norm[0]), _rope_tables(seq), parts, seq)
    y_att = _attention(qa, ka, va, _row(att_sinks[0]), batch, seq)
    gate_bias = _col(jnp.concatenate([ml_i_bias[0], ml_f_bias[0]]))
    y_ml = _mlstm(qkm, vm, om, gt, ml_conv_w[0].astype(F32), _row(ml_conv_b[0]), gate_bias, _row(ml_norm[0]),
                  batch, seq)
    wo = e_w_out[0].astype(BF16)
    h = _outproj_mlp(y_att, y_ml, h, wo[:ATT_Q_W], wo[ATT_Q_W:], _row(mlp_norm[0]), w_up[0].astype(BF16),
                     w_down[0].astype(BF16), fg, final_norm=False)

    w = o_w_in[0].astype(BF16)
    o_x = SSM_INNER
    o_dt = o_x + SSM_CONV_W
    o_rw = o_dt + SSM_HEADS
    parts = [w[:, :o_x], w[:, o_x:o_dt], w[:, o_dt:o_rw].T, w[:, o_rw:]]
    z, xbc, dt_t, rw = _inproj_odd(h, _row(mix_norm[1]), parts, seq)
    y_ssm = _ssd(xbc, z, dt_t, ssm_conv_w[0].astype(F32), _row(ssm_conv_b[0]), _col(ssm_dt_bias[0]),
                 _col(ssm_a_log[0]), _row(jnp.repeat(ssm_d[0], SSM_HEAD_DIM)), _row(ssm_norm[0]), batch, seq)
    rw_consts = (_row(rw_mu[0]), _row(rw_w0[0]), rw_w2[0].astype(BF16), _row(rw_a0[0]), rw_a2[0].astype(BF16),
                 rw_g2[0].astype(BF16),_row(rw_k_k[0]), _row(rw_k_a[0]), _row(rw_r_k[0]), _row(rw_ln_w[0]),
                 _row(rw_ln_b[0]))
    y_rw = _rwkv(rw, rw_consts, batch, seq)
    wo = o_w_out[0].astype(BF16)
    h = _outproj_mlp(y_ssm, y_rw, h, wo[:SSM_INNER], wo[SSM_INNER:], _row(mlp_norm[1]), w_up[1].astype(BF16),
                     w_down[1].astype(BF16), fg, final_norm=True)
    return h.reshape(batch, seq, D_MODEL)
```

```python
import functools

import numpy as np

import jax
import jax.numpy as jnp
from jax import lax
from jax.experimental import pallas as pl
from jax.experimental.pallas import tpu as pltpu

F32 = jnp.float32
BF16 = jnp.bfloat16
HI = lax.Precision.HIGHEST

D_MODEL = 1024
D_FF = 4 * D_MODEL
NORM_EPS = 1e-5
CONV_TAPS = 4

ATT_HEAD_DIM = 64
ATT_Q_HEADS = 8
ATT_KV_HEADS = 2
ATT_GROUP = ATT_Q_HEADS // ATT_KV_HEADS
ATT_WINDOW = 128
ROPE_THETA = 500000.0
ROPE_DIM = 16
ATT_Q_W = ATT_Q_HEADS * ATT_HEAD_DIM
ATT_KV_W = ATT_KV_HEADS * ATT_HEAD_DIM

ML_HEADS = 4
ML_V_DIM = 128
ML_QK_DIM = 64
ML_CHUNK = 128
ML_NORM_EPS = 1e-6
ML_QK_W = ML_HEADS * ML_QK_DIM
ML_V_W = ML_HEADS * ML_V_DIM

SSM_HEAD_DIM = 64
SSM_HEADS = 8
SSM_GROUPS = 2
SSM_HEADS_PER_GROUP = SSM_HEADS // SSM_GROUPS
SSM_STATE = 128
SSM_CHUNK = 128
SSM_INNER = SSM_HEADS * SSM_HEAD_DIM
SSM_BC = SSM_GROUPS * SSM_STATE
SSM_CONV_W = SSM_INNER + 2 * SSM_BC

RW_HEAD_DIM = 64
RW_HEADS = 8
RW_W = RW_HEADS * RW_HEAD_DIM
RW_DECAY_LORA = 64
RW_AAA_LORA = 64
RW_GATE_LORA = 128
RW_IN = 3 * RW_W + RW_DECAY_LORA + RW_AAA_LORA + RW_GATE_LORA
RW_LN_EPS = 64e-5
RW_CHUNK = 64

V7X_VMEM_BYTES = 64 * 1024 * 1024
VMEM_LIMIT_BYTES = V7X_VMEM_BYTES - 6 * 1024 * 1024
SUBLANES = 8
LANES = 128

PROJ_ROWS = 512
FF_CHUNK = 512


def _params(*sem):
    return pltpu.CompilerParams(dimension_semantics=sem, vmem_limit_bytes=VMEM_LIMIT_BYTES)


def _const_spec(shape):
    nd = len(shape)
    return pl.BlockSpec(shape, lambda *_: (0,) * nd, pipeline_mode=pl.Buffered(1))


def _iota(shape, dim):
    return lax.broadcasted_iota(jnp.int32, shape, dim)


def _tri_lower(n):
    return (_iota((n, n), 1) <= _iota((n, n), 0)).astype(F32)


def _tri_upper(n):
    return (_iota((n, n), 0) <= _iota((n, n), 1)).astype(F32)


def _eye(n):
    return (_iota((n, n), 0) == _iota((n, n), 1)).astype(F32)


def _dot(a, b, precision=None):
    return jnp.dot(a, b, preferred_element_type=F32, precision=precision)


def _dot_nt(a, b, precision=None):
    return lax.dot_general(a, b, (((1,), (1,)), ((), ())), preferred_element_type=F32, precision=precision)


def _sigmoid(x):
    return 1.0 / (1.0 + jnp.exp(-x))


def _softplus(x):
    return jnp.maximum(x, 0.0) + jnp.log1p(jnp.exp(-jnp.abs(x)))


def _silu(x):
    return x * _sigmoid(x)


def _rmsnorm(x, g):
    y = x * lax.rsqrt(jnp.mean(x * x, axis=-1, keepdims=True) + NORM_EPS)
    return y * g


def _shifted_rows(x, tail, k):
    rolled = pltpu.roll(x, k, axis=0)
    tail_rolled = pltpu.roll(tail, k, axis=0)
    first = jnp.where(_iota((SUBLANES, 1), 0) < k, tail_rolled, rolled[:SUBLANES])
    return jnp.concatenate([first, rolled[SUBLANES:]], axis=0)


def _causal_conv_silu(x, tail, w, b):
    out = b + x * w[CONV_TAPS - 1:CONV_TAPS, :]
    for k in range(1, CONV_TAPS):
        out = out + _shifted_rows(x, tail, k) * w[CONV_TAPS - 1 - k:CONV_TAPS - k, :]
    return _silu(out)


def _rope(x, cos, sin_lo, sin_hi, reps):
    w = x.shape[1]
    c = jnp.concatenate([cos] * reps, axis=1)
    s_lo = jnp.concatenate([sin_lo] * reps, axis=1)
    s_hi = jnp.concatenate([sin_hi] * reps, axis=1)
    half = ROPE_DIM // 2
    nxt = pltpu.roll(x, w - half, axis=1)
    prv = pltpu.roll(x, half, axis=1)
    return x * c + nxt * s_lo + prv * s_hi


def _inproj_even_kernel(x_ref, g_ref, cos_ref, slo_ref, shi_ref, wq_ref, wk_ref, wv_ref, wqk_ref, wvm_ref,
                        wom_ref, wgt_ref, qa_ref, ka_ref, va_ref, qkm_ref, vm_ref, om_ref, gt_ref):
    hb = _rmsnorm(x_ref[...], g_ref[...]).astype(BF16)
    cos, s_lo, s_hi = cos_ref[...], slo_ref[...], shi_ref[...]
    q = _rope(_dot(hb, wq_ref[...]), cos, s_lo, s_hi, ATT_Q_W // LANES)
    qa_ref[...] = (q * (ATT_HEAD_DIM ** -0.5)).astype(BF16)
    ka_ref[...] = _rope(_dot(hb, wk_ref[...]), cos, s_lo, s_hi, ATT_KV_W // LANES).astype(BF16)
    va_ref[...] = _dot(hb, wv_ref[...]).astype(BF16)
    qkm_ref[...] = _dot(hb, wqk_ref[...])
    vm_ref[...] = _dot(hb, wvm_ref[...]).astype(BF16)
    om_ref[...] = _dot(hb, wom_ref[...])
    gt_ref[...] = _dot_nt(wgt_ref[...], hb)


def _inproj_even(x2, g, tables, w, seq):
    n = x2.shape[0]
    tm = PROJ_ROWS
    per_seq = seq // tm
    row = lambda width: pl.BlockSpec((tm, width), lambda i: (i, 0))
    tab = pl.BlockSpec((tm, LANES), lambda i: (i % per_seq, 0))
    wq, wk, wv, wqk, wvm, wom, wgt = w
    out_shapes = (
        jax.ShapeDtypeStruct((n, ATT_Q_W), BF16), jax.ShapeDtypeStruct((n, ATT_KV_W), BF16),
        jax.ShapeDtypeStruct((n, ATT_KV_W), BF16), jax.ShapeDtypeStruct((n, 2 * ML_QK_W), F32),
        jax.ShapeDtypeStruct((n, ML_V_W), BF16), jax.ShapeDtypeStruct((n, ML_V_W), F32),
        jax.ShapeDtypeStruct((n // seq, 2 * ML_HEADS, seq), F32))
    return pl.pallas_call(
        _inproj_even_kernel,
        grid=(n // tm,),
        in_specs=[row(D_MODEL), _const_spec((1, D_MODEL)), tab, tab, tab] + [_const_spec(a.shape) for a in w],
        out_specs=(row(ATT_Q_W), row(ATT_KV_W), row(ATT_KV_W), row(2 * ML_QK_W), row(ML_V_W), row(ML_V_W),
                   pl.BlockSpec((None, 2 * ML_HEADS, tm), lambda i: (i // per_seq, 0, i % per_seq))),
        out_shape=out_shapes,
        compiler_params=_params("arbitrary"),
        name="inproj_even",
    )(x2, g, *tables, wq, wk, wv, wqk, wvm, wom, wgt)


def _attention_stages(q_ref, kc_ref, kp_ref, vc_ref, vp_ref, sink_ref, blk, nb, emit):
    w = ATT_WINDOW
    dh = ATT_HEAD_DIM
    G = ATT_GROUP
    i = _iota((G * w, 2 * w), 0) % w
    j = _iota((G * w, 2 * w), 1)
    mask = (j > i) & (j <= i + w) & ((j >= w) | (blk > 0))
    sinks = sink_ref[...]
    problems = [(b, g) for b in range(nb) for g in range(ATT_KV_HEADS)]
    scores, values, sink_cols = [], [], []
    for b, g in problems:
        q = q_ref[b]
        qs = jnp.concatenate([q[:, (g * G + a) * dh:(g * G + a + 1) * dh] for a in range(G)], axis=0)
        ks = jnp.concatenate([kp_ref[b][:, g * dh:(g + 1) * dh], kc_ref[b][:, g * dh:(g + 1) * dh]], axis=0)
        values.append(jnp.concatenate([vp_ref[b][:, g * dh:(g + 1) * dh], vc_ref[b][:, g * dh:(g + 1) * dh]], axis=0))
        scores.append(_dot_nt(qs, ks))
        sink_cols.append(jnp.concatenate(
            [jnp.broadcast_to(sinks[:, g * G + a:g * G + a + 1], (w, 1)) for a in range(G)], axis=0))
    yield
    probs, dens = [], []
    for s, sink in zip(scores, sink_cols):
        s = jnp.where(mask, s, -jnp.inf)
        m = jnp.maximum(jnp.max(s, axis=-1, keepdims=True), sink)
        p = jnp.exp(s - m)
        dens.append(jnp.sum(p, axis=-1, keepdims=True) + jnp.exp(sink - m))
        probs.append(p.astype(BF16))
    pv = [_dot(p, v) for p, v in zip(probs, values)]
    yield
    outs = [o / d for o, d in zip(pv, dens)]
    for b in range(nb):
        heads = [outs[b * ATT_KV_HEADS + g][a * w:(a + 1) * w] for g in range(ATT_KV_HEADS) for a in range(G)]
        emit(b, jnp.concatenate(heads, axis=1).astype(BF16))


def _mlstm_stages(qk_ref, tail_ref, v_ref, o_ref, gt_ref, cw_ref, cb_ref, gb_ref, nw_ref, c_s, m_s, chunk, nb, emit):
    L = ML_CHUNK
    H = ML_HEADS
    DV = ML_V_DIM
    NG = 2 * H

    cw, cb = cw_ref[...], cb_ref[...]
    qk = [_causal_conv_silu(qk_ref[b], jnp.where(chunk == 0, 0.0, tail_ref[b]), cw, cb) for b in range(nb)]

    g = jnp.concatenate([gt_ref[b] + gb_ref[...] for b in range(nb)], axis=0)
    lg = jnp.where(_iota(g.shape, 0) % NG < H, g, -_softplus(-g))
    tri_l, tri_u, eye = _tri_lower(L), _tri_upper(L), _eye(L)
    cum_row = _dot(lg, tri_u, HI)
    lg_col = _dot_nt(eye, lg, HI)
    cum_col = _dot_nt(tri_l, lg, HI)
    causal = _iota((L, L), 1) <= _iota((L, L), 0)

    chains = [(b, h) for b in range(nb) for h in range(H)]
    idx = lambda b, h: b * H + h
    q_c = [qk[b][:, h * ML_QK_DIM:(h + 1) * ML_QK_DIM] for b, h in chains]
    k_c = [qk[b][:, ML_QK_W + h * ML_QK_DIM:ML_QK_W + (h + 1) * ML_QK_DIM] * (ML_QK_DIM ** -0.5) for b, h in chains]
    ones = jnp.ones((L, DV), BF16)
    v_c = [jnp.concatenate([v_ref[b][:, h * DV:(h + 1) * DV], ones], axis=1) for b, h in chains]
    qb_c = [q.astype(BF16) for q in q_c]
    c_prev = [c_s[idx(b, h)] for b, h in chains]
    m_prev = [m_s[idx(b, h):idx(b, h) + 1, 0:1] for b, h in chains]

    s_qk = [_dot_nt(qb, k.astype(BF16)) for qb, k in zip(qb_c, k_c)]
    inter = [_dot(qb, c.astype(BF16)) for qb, c in zip(qb_c, c_prev)]

    yield
    gi_c = [b * NG + h for b, h in chains]
    b_col = [cum_col[:, gi + H:gi + H + 1] for gi in gi_c]
    b_last = [bc[L - 1:L, :] for bc in b_col]
    dlog = [jnp.where(causal, bc - cum_row[gi + H:gi + H + 1, :] + lg[gi:gi + 1, :], -jnp.inf)
            for gi, bc in zip(gi_c, b_col)]
    row_max = [jnp.max(d, axis=1, keepdims=True) for d in dlog]
    a_loc = [bl - bc + lg_col[:, gi:gi + 1] for gi, bc, bl in zip(gi_c, b_col, b_last)]
    m_loc = [jnp.max(a, axis=0, keepdims=True) for a in a_loc]
    inter_log = [bc + mp for bc, mp in zip(b_col, m_prev)]
    m_t = [jnp.maximum(il, rm) for il, rm in zip(inter_log, row_max)]
    qk_w = [s * jnp.exp(d - mt) for s, d, mt in zip(s_qk, dlog, m_t)]
    w_inter = [jnp.exp(il - mt) for il, mt in zip(inter_log, m_t)]
    kw = [k * jnp.exp(a - ml) for k, a, ml in zip(k_c, a_loc, m_loc)]

    intra = [_dot(w.astype(BF16), v) for w, v in zip(qk_w, v_c)]
    c_loc = [lax.dot_general(w.astype(BF16), v, (((0,), (0,)), ((), ())), preferred_element_type=F32)
             for w, v in zip(kw, v_c)]

    yield
    both = [a + w * e for a, w, e in zip(intra, w_inter, inter)]
    xs = [nd[:, :DV] / jnp.maximum(jnp.abs(nd[:, DV:DV + 1]), jnp.exp(-mt)) for nd, mt in zip(both, m_t)]
    mus = [jnp.mean(x, axis=-1, keepdims=True) for x in xs]
    cen = [x - mu for x, mu in zip(xs, mus)]
    var = [jnp.mean(jnp.square(c), axis=-1, keepdims=True) for c in cen]
    outs = [c * lax.rsqrt(vr + ML_NORM_EPS) for c, vr in zip(cen, var)]
    for i, (b, h) in enumerate(chains):
        m_new = jnp.maximum(b_last[i] + m_prev[i], m_loc[i])
        fa = jnp.exp(b_last[i] + m_prev[i] - m_new)
        fb = jnp.exp(m_loc[i] - m_new)
        c_s[idx(b, h)] = fa * c_prev[i] + fb * c_loc[i]
        m_s[idx(b, h):idx(b, h) + 1, 0:1] = m_new

    nw = nw_ref[...]
    for b in range(nb):
        hm = jnp.concatenate(outs[b * H:(b + 1) * H], axis=1) * nw
        emit(b, (hm * _sigmoid(o_ref[b])).astype(BF16))


def _outproj_mlp_stages(y_parts, h, g_ref, wu_ref, wd_ref, fg_ref, final_norm, emit):
    h1 = h
    for y, w_ref in y_parts:
        h1 = h1 + _dot(y, w_ref[...])
    hb = _rmsnorm(h1, g_ref[...]).astype(BF16)
    n_chunks = D_FF // FF_CHUNK
    mlp = None
    for c in range(n_chunks):
        u = _dot(hb, wu_ref[:, c * FF_CHUNK:(c + 1) * FF_CHUNK])
        a = jnp.square(jnp.maximum(u, 0.0)).astype(BF16)
        d = _dot(a, wd_ref[c * FF_CHUNK:(c + 1) * FF_CHUNK, :])
        mlp = d if mlp is None else mlp + d
        if c == n_chunks // 2 - 1:
            yield
    h2 = h1 + mlp
    if final_norm:
        h2 = _rmsnorm(h2, fg_ref[...])
    emit(h2)


def _drive(mixers, mlp):
    for m in mixers:
        next(m)
    next(mlp)
    for m in mixers:
        next(m)
    for _ in mlp:
        pass
    for m in mixers:
        for _ in m:
            pass


def _layer0_kernel(q_ref, kc_ref, kp_ref, vc_ref, vp_ref, sink_ref,
                   qk_ref, tail_ref, vm_ref, om_ref, gt_ref, cw_ref, cb_ref, gb_ref, nw_ref,
                   h_ref, wo_ref, g_ref, wu_ref, wd_ref, fg_ref,
                   o_ref, c_s, m_s, y_s, *, nb, nc):
    step = pl.program_id(0)
    chunk = jnp.minimum(step, nc - 1)
    L = ML_CHUNK

    @pl.when(step == 0)
    def _():
        c_s[...] = jnp.zeros_like(c_s)
        m_s[...] = jnp.zeros_like(m_s)
        y_s[...] = jnp.zeros_like(y_s)

    wr = step % 2
    rd = 1 - wr

    def emit_att(b, block):
        y_s[wr, b * L:(b + 1) * L, 0:ATT_Q_W] = block

    def emit_ml(b, block):
        y_s[wr, b * L:(b + 1) * L, ATT_Q_W:ATT_Q_W + ML_V_W] = block

    def emit_out(block):
        o_ref[...] = block.reshape(nb, L, D_MODEL)

    mixers = [
        _attention_stages(q_ref, kc_ref, kp_ref, vc_ref, vp_ref, sink_ref, chunk, nb, emit_att),
        _mlstm_stages(qk_ref, tail_ref, vm_ref, om_ref, gt_ref, cw_ref, cb_ref, gb_ref, nw_ref, c_s, m_s, chunk, nb,
                      emit_ml),
    ]
    mlp = _outproj_mlp_stages([(y_s[rd], wo_ref)], h_ref[...].reshape(nb * L, D_MODEL), g_ref, wu_ref, wd_ref,
                              fg_ref, False, emit_out)
    _drive(mixers, mlp)


def _layer0(qa, ka, va, sinks, qkm, vm, om, gt, conv_w, conv_b, gate_bias, norm_w, h, wo, g, wu, wd, fg, batch, seq):
    L = ML_CHUNK
    nc = seq // L
    per8 = L // SUBLANES
    three = lambda t: t.reshape(batch, seq, t.shape[-1])
    mix = lambda c: jnp.minimum(c, nc - 1)
    cur = lambda width: pl.BlockSpec((batch, L, width), lambda c: (0, mix(c), 0))
    prev = lambda width: pl.BlockSpec((batch, L, width), lambda c: (0, jnp.maximum(mix(c) - 1, 0), 0))
    tail = pl.BlockSpec((batch, SUBLANES, 2 * ML_QK_W), lambda c: (0, jnp.maximum(mix(c) * per8 - 1, 0), 0))
    late = pl.BlockSpec((batch, L, D_MODEL), lambda c: (0, jnp.maximum(c - 1, 0), 0))
    q3, k3, v3, qk3 = three(qa), three(ka), three(va), three(qkm)
    consts = (conv_w, conv_b, gate_bias, norm_w)
    weights = (wo, g, wu, wd, fg)
    out = pl.pallas_call(
        functools.partial(_layer0_kernel, nb=batch, nc=nc),
        grid=(nc + 1,),
        in_specs=[cur(ATT_Q_W), cur(ATT_KV_W), prev(ATT_KV_W), cur(ATT_KV_W), prev(ATT_KV_W),
                  _const_spec((1, ATT_Q_HEADS)),
                  cur(2 * ML_QK_W), tail, cur(ML_V_W), cur(ML_V_W),
                  pl.BlockSpec((batch, 2 * ML_HEADS, L), lambda c: (0, 0, mix(c)))]
                 + [_const_spec(a.shape) for a in consts] + [late] + [_const_spec(a.shape) for a in weights],
        out_specs=late,
        out_shape=jax.ShapeDtypeStruct((batch, seq, D_MODEL), F32),
        scratch_shapes=[pltpu.VMEM((batch * ML_HEADS, ML_QK_DIM, 2 * ML_V_DIM), F32),
                        pltpu.VMEM((batch * ML_HEADS, LANES), F32),
                        pltpu.VMEM((2, batch * L, ATT_Q_W + ML_V_W), BF16)],
        compiler_params=_params("arbitrary"),
        name="layer0_mix_mlp",
    )(q3, k3, k3, v3, v3, sinks, qk3, qk3, three(vm), three(om), gt, *consts, three(h), *weights)
    return out.reshape(batch * seq, D_MODEL)


def _inproj_odd_kernel(x_ref, g_ref, wz_ref, wx_ref, wdt_ref, wrw_ref, z_ref, xbc_ref, dt_ref, rw_ref):
    hb = _rmsnorm(x_ref[...], g_ref[...]).astype(BF16)
    z_ref[...] = _dot(hb, wz_ref[...])
    xbc_ref[...] = _dot(hb, wx_ref[...])
    dt_ref[...] = _dot_nt(wdt_ref[...], hb)
    rw_ref[...] = _dot(hb, wrw_ref[...])


def _inproj_odd(x2, g, w, seq):
    n = x2.shape[0]
    tm = PROJ_ROWS
    per_seq = seq // tm
    row = lambda width: pl.BlockSpec((tm, width), lambda i: (i, 0))
    out_shapes = (jax.ShapeDtypeStruct((n, SSM_INNER), F32), jax.ShapeDtypeStruct((n, SSM_CONV_W), F32),
                  jax.ShapeDtypeStruct((n // seq, SSM_HEADS, seq), F32), jax.ShapeDtypeStruct((n, RW_IN), F32))
    return pl.pallas_call(
        _inproj_odd_kernel,
        grid=(n // tm,),
        in_specs=[row(D_MODEL), _const_spec((1, D_MODEL))] + [_const_spec(a.shape) for a in w],
        out_specs=(row(SSM_INNER), row(SSM_CONV_W),
                   pl.BlockSpec((None, SSM_HEADS, tm), lambda i: (i // per_seq, 0, i % per_seq)), row(RW_IN)),
        out_shape=out_shapes,
        compiler_params=_params("arbitrary"),
        name="inproj_odd",
    )(x2, g, *w)


def _ssd_stages(xbc_ref, tail_ref, z_ref, dt_ref, cw_ref, cb_ref, dtb_ref, alog_ref, d_ref, nw_ref, s_s, chunk, nb,
                emit):
    L = SSM_CHUNK
    P = SSM_HEAD_DIM
    HG = SSM_HEADS_PER_GROUP
    NH = SSM_HEADS

    cw, cb_ = cw_ref[...], cb_ref[...]
    xbc = [_causal_conv_silu(xbc_ref[b], jnp.where(chunk == 0, 0.0, tail_ref[b]), cw, cb_) for b in range(nb)]

    dt_row = _softplus(jnp.concatenate([dt_ref[b] + dtb_ref[...] for b in range(nb)], axis=0))
    neg_a = -jnp.exp(jnp.concatenate([alog_ref[...]] * nb, axis=0))
    a_row = dt_row * neg_a
    tri_l, tri_u, eye = _tri_lower(L), _tri_upper(L), _eye(L)
    acum_row = _dot(a_row, tri_u, HI)
    acum_col = _dot_nt(tri_l, a_row, HI)
    dt_col = _dot_nt(eye, dt_row, HI)
    causal = _iota((L, L), 1) <= _iota((L, L), 0)
    eye_b = eye.astype(BF16)

    groups = [(b, g) for b in range(nb) for g in range(SSM_GROUPS)]
    gidx = lambda b, g: b * SSM_GROUPS + g
    bg = [xbc[b][:, SSM_INNER + g * SSM_STATE:SSM_INNER + (g + 1) * SSM_STATE].astype(BF16) for b, g in groups]
    cg = [xbc[b][:, SSM_INNER + SSM_BC + g * SSM_STATE:SSM_INNER + SSM_BC + (g + 1) * SSM_STATE].astype(BF16)
          for b, g in groups]
    s_prev = [s_s[gidx(b, g)] for b, g in groups]

    cb = [_dot_nt(c, bq) for c, bq in zip(cg, bg)]
    bg_t = [_dot_nt(eye_b, bq).astype(BF16) for bq in bg]
    y_off = [_dot(c, sp.astype(BF16)) for c, sp in zip(cg, s_prev)]

    yield
    heads = [(b, g, j) for b, g in groups for j in range(HG)]
    scores, xcs, e_col, xws, cds = [], [], [], [], []
    for b, g, j in heads:
        h = g * HG + j
        col = b * NH + h
        a_col = acum_col[:, col:col + 1]
        a_last = a_col[L - 1:L, :]
        decay = jnp.exp(jnp.where(causal, a_col - acum_row[col:col + 1, :], -jnp.inf))
        xc = xbc[b][:, h * P:(h + 1) * P] * dt_col[:, col:col + 1]
        scores.append((cb[gidx(b, g)] * decay).astype(BF16))
        xcs.append(xc.astype(BF16))
        e_col.append(jnp.exp(a_col))
        xws.append(xc * jnp.exp(a_last - a_col))
        cds.append(jnp.broadcast_to(jnp.exp(a_last), (1, P)))

    y_diag = [_dot(sc, xc) for sc, xc in zip(scores, xcs)]
    states = [_dot(bg_t[i], jnp.concatenate(xws[i * HG:(i + 1) * HG], axis=1).astype(BF16))
              for i in range(len(groups))]

    yield
    for i, (b, g) in enumerate(groups):
        s_s[gidx(b, g)] = s_prev[i] * jnp.concatenate(cds[i * HG:(i + 1) * HG], axis=1) + states[i]

    gw = SSM_INNER // SSM_GROUPS
    for b in range(nb):
        ys = []
        for g in range(SSM_GROUPS):
            for j in range(HG):
                i = (b * SSM_GROUPS + g) * HG + j
                ys.append(y_diag[i] + y_off[gidx(b, g)][:, j * P:(j + 1) * P] * e_col[i])
        xs = xbc[b][:, :SSM_INNER]
        y = jnp.concatenate(ys, axis=1) + d_ref[...] * xs
        y = y * _silu(z_ref[b])
        normed = []
        for g in range(SSM_GROUPS):
            yg = y[:, g * gw:(g + 1) * gw]
            normed.append(yg * lax.rsqrt(jnp.mean(yg * yg, axis=-1, keepdims=True) + NORM_EPS))
        emit(b, (jnp.concatenate(normed, axis=1) * nw_ref[...]).astype(BF16))


def _layer1_kernel(xbc_ref, tail_ref, z_ref, dt_ref, cw_ref, cb_ref, dtb_ref, alog_ref, d_ref, nw_ref,
                   yrw_ref, h_ref, wa_ref, wb_ref, g_ref, wu_ref, wd_ref, fg_ref,
                   o_ref, s_s, y_s, *, nb, nc):
    step = pl.program_id(0)
    chunk = jnp.minimum(step, nc - 1)
    L = SSM_CHUNK

    @pl.when(step == 0)
    def _():
        s_s[...] = jnp.zeros_like(s_s)
        y_s[...] = jnp.zeros_like(y_s)

    wr = step % 2
    rd = 1 - wr

    def emit_ssm(b, block):
        y_s[wr, b * L:(b + 1) * L, :] = block

    def emit_out(block):
        o_ref[...] = block.reshape(nb, L, D_MODEL)

    mixers = [_ssd_stages(xbc_ref, tail_ref, z_ref, dt_ref, cw_ref, cb_ref, dtb_ref, alog_ref, d_ref, nw_ref, s_s,
                          chunk, nb, emit_ssm)]
    mlp = _outproj_mlp_stages([(y_s[rd], wa_ref), (yrw_ref[...].reshape(nb * L, RW_W), wb_ref)],
                              h_ref[...].reshape(nb * L, D_MODEL), g_ref, wu_ref, wd_ref, fg_ref, True, emit_out)
    _drive(mixers, mlp)


def _layer1(xbc, z, dt_t, conv_w, conv_b, dt_bias, a_log, d_row, norm_w, y_rw, h, wa, wb, g, wu, wd, fg, batch, seq):
    L = SSM_CHUNK
    nc = seq // L
    per8 = L // SUBLANES
    three = lambda t: t.reshape(batch, seq, t.shape[-1])
    mix = lambda c: jnp.minimum(c, nc - 1)
    cur = lambda width: pl.BlockSpec((batch, L, width), lambda c: (0, mix(c), 0))
    tail = pl.BlockSpec((batch, SUBLANES, SSM_CONV_W), lambda c: (0, jnp.maximum(mix(c) * per8 - 1, 0), 0))
    late = lambda width: pl.BlockSpec((batch, L, width), lambda c: (0, jnp.maximum(c - 1, 0), 0))
    consts = (conv_w, conv_b, dt_bias, a_log, d_row, norm_w)
    weights = (wa, wb, g, wu, wd, fg)
    x3 = three(xbc)
    out = pl.pallas_call(
        functools.partial(_layer1_kernel, nb=batch, nc=nc),
        grid=(nc + 1,),
        in_specs=[cur(SSM_CONV_W), tail, cur(SSM_INNER),
                  pl.BlockSpec((batch, SSM_HEADS, L), lambda c: (0, 0, mix(c)))]
                 + [_const_spec(a.shape) for a in consts] + [late(RW_W), late(D_MODEL)]
                 + [_const_spec(a.shape) for a in weights],
        out_specs=late(D_MODEL),
        out_shape=jax.ShapeDtypeStruct((batch, seq, D_MODEL), F32),
        scratch_shapes=[pltpu.VMEM((batch * SSM_GROUPS, SSM_STATE, SSM_HEADS_PER_GROUP * SSM_HEAD_DIM), F32),
                        pltpu.VMEM((2, batch * L, SSM_INNER), BF16)],
        compiler_params=_params("arbitrary"),
        name="layer1_mix_mlp",
    )(x3, x3, three(z), dt_t, *consts, three(y_rw), three(h), *weights)
    return out.reshape(batch * seq, D_MODEL)


def _split_bf16(x):
    hi = x.astype(BF16)
    lo = (x - hi.astype(F32)).astype(BF16)
    return hi, lo


def _dot_ones_exact(t, ones_matrix):
    hi, lo = _split_bf16(t)
    return _dot(hi, ones_matrix) + _dot(lo, ones_matrix)


def _unit_lower_inverses(mats, eye, n):
    xbs = [(eye - a.astype(F32)).astype(BF16) for a in mats]
    steps = 0
    terms = 2
    while terms < n:
        terms *= 2
        steps += 1
    for _ in range(steps):
        xfs = [xb.astype(F32) for xb in xbs]
        resids = [(eye - xf - _dot(a, xb)).astype(BF16) for a, xb, xf in zip(mats, xbs, xfs)]
        xbs = [(xf + _dot(xb, e)).astype(BF16) for xb, xf, e in zip(xbs, xfs, resids)]
    return xbs


def _rwkv_kernel(rw_ref, tail_ref, mu_ref, w0_ref, w2_ref, a0_ref, a2_ref, g2_ref, kk_ref, ka_ref, rk_ref,
                 lnw_ref, lnb_ref, y_ref, st_s, *, nb):
    chunk = pl.program_id(0)
    C = RW_CHUNK
    D = RW_HEAD_DIM
    R = nb * C

    @pl.when(chunk == 0)
    def _():
        st_s[...] = jnp.zeros_like(st_s)

    x = rw_ref[...].reshape(R, RW_IN)
    rolled = pltpu.roll(x, 1, axis=0)
    first_row = _iota((C, 1), 0) == 0
    prev = []
    for b in range(nb):
        before = jnp.where(chunk == 0, 0.0, tail_ref[b, SUBLANES - 1:SUBLANES, :])
        prev.append(jnp.where(first_row, before, rolled[b * C:(b + 1) * C]))
    prev = jnp.concatenate(prev, axis=0)
    xm = x + (prev - x) * mu_ref[...]
    r = xm[:, 0:RW_W]
    k = xm[:, RW_W:2 * RW_W]
    v = xm[:, 2 * RW_W:3 * RW_W]
    o = 3 * RW_W
    wd = xm[:, o:o + RW_DECAY_LORA]
    ad = xm[:, o + RW_DECAY_LORA:o + RW_DECAY_LORA + RW_AAA_LORA]
    gd = xm[:, o + RW_DECAY_LORA + RW_AAA_LORA:]

    w_log = -_softplus(-(w0_ref[...] + _dot(jnp.tanh(wd).astype(BF16), w2_ref[...]))) - 0.5
    ld = -jnp.exp(w_log)
    a = _sigmoid(a0_ref[...] + _dot(ad.astype(BF16), a2_ref[...]))
    gate = _dot(_sigmoid(gd).astype(BF16), g2_ref[...])

    half = RW_W // 2
    same_head = (_iota((half, half), 0) // D == _iota((half, half), 1) // D).astype(BF16)
    head_sum = lambda t: jnp.concatenate(
        [_dot_ones_exact(t[:, :half], same_head), _dot_ones_exact(t[:, half:], same_head)], axis=1)

    kk = k * kk_ref[...]
    kk = kk / jnp.maximum(jnp.sqrt(head_sum(kk * kk)), 1e-12)
    k2 = k * (1.0 + (a - 1.0) * ka_ref[...])
    bvec = kk * a

    tri = ((_iota((R, R), 1) <= _iota((R, R), 0)) & (_iota((R, R), 1) // C == _iota((R, R), 0) // C)).astype(BF16)
    ld_hi, ld_rest = _split_bf16(ld)
    ld_mid = ld - ld_hi.astype(F32)
    ld_lo = (ld_mid - ld_rest.astype(F32)).astype(BF16)
    cl = _dot(tri, ld_hi) + (_dot(tri, ld_rest) + _dot(tri, ld_lo))
    c_last = jnp.concatenate(
        [jnp.broadcast_to(cl[(b + 1) * C - 1:(b + 1) * C, :], (C, RW_W)) for b in range(nb)], axis=0)
    e_neg = jnp.exp(-cl)
    e_end = jnp.exp(c_last - cl)
    w_end = [jnp.exp(cl[(b + 1) * C - 1:(b + 1) * C, :]) for b in range(nb)]
    rw_ = r * jnp.exp(cl)
    kkw = kk * jnp.exp(cl - ld)
    bd = bvec * e_neg
    kd = k2 * e_neg
    bd_end = bvec * e_end
    kd_end = k2 * e_end

    g_row = _iota((2 * C, 2 * C), 0)
    g_col = _iota((2 * C, 2 * C), 1) % C
    keep = g_col < jnp.where(g_row < C, g_row, g_row - C + 1)
    eye = _eye(C)
    chains = [(b, h) for b in range(nb) for h in range(RW_HEADS)]
    blk = lambda t, b, h: t[b * C:(b + 1) * C, h * D:(h + 1) * D]
    kkw_c = [blk(kkw, b, h).astype(BF16) for b, h in chains]
    rw_c = [blk(rw_, b, h).astype(BF16) for b, h in chains]
    vb_c = [blk(v, b, h).astype(BF16) for b, h in chains]
    gms = [jnp.where(keep, _dot_nt(jnp.concatenate([kq, rq], axis=0),
                                   jnp.concatenate([blk(bd, b, h), blk(kd, b, h)], axis=0).astype(BF16)),
                     0.0).astype(BF16)
           for (b, h), kq, rq in zip(chains, kkw_c, rw_c)]
    a_v = [_dot(jnp.concatenate([g[:C, C:], g[C:, C:]], axis=0), vb) for g, vb in zip(gms, vb_c)]
    invs = _unit_lower_inverses([g[:C, :C] for g in gms], eye, C)
    tws = [_dot(inv, jnp.concatenate([av[:C].astype(BF16), kq], axis=1))
           for inv, av, kq in zip(invs, a_v, kkw_c)]
    sts = [st_s[b * RW_HEADS + h] for b, h in chains]
    xss = [_dot_nt(jnp.concatenate([tw[:, D:].astype(BF16), rq], axis=0), st.astype(BF16))
           for tw, rq, st in zip(tws, rw_c, sts)]
    ubs = [(-tw[:, :D] - xs[:C]).astype(BF16) for tw, xs in zip(tws, xss)]
    ys = [xs[C:] + _dot(g[C:, :C], ub) + av[C:] for xs, g, ub, av in zip(xss, gms, ubs, a_v)]
    for (b, h), st, ub, vb in zip(chains, sts, ubs, vb_c):
        ends = jnp.concatenate([blk(bd_end, b, h), blk(kd_end, b, h)], axis=0).astype(BF16)
        upd = lax.dot_general(jnp.concatenate([ub, vb], axis=0), ends, (((0,), (0,)), ((), ())),
                              preferred_element_type=F32)
        st_s[b * RW_HEADS + h] = st * w_end[b][:, h * D:(h + 1) * D] + upd

    y = jnp.concatenate([jnp.concatenate(ys[b * RW_HEADS:(b + 1) * RW_HEADS], axis=1) for b in range(nb)], axis=0)
    mu = head_sum(y) * (1.0 / D)
    var = head_sum(jnp.square(y - mu)) * (1.0 / D)
    yn = (y - mu) * lax.rsqrt(var + RW_LN_EPS) * lnw_ref[...] + lnb_ref[...]
    bonus = head_sum(r * k2 * rk_ref[...]) * v
    y_ref[...] = ((yn + bonus) * gate).astype(BF16).reshape(nb, C, RW_W)


def _rwkv(rw, consts, batch, seq):
    C = RW_CHUNK
    nc = seq // C
    per8 = C // SUBLANES
    rw3 = rw.reshape(batch, seq, RW_IN)
    y = pl.pallas_call(
        functools.partial(_rwkv_kernel, nb=batch),
        grid=(nc,),
        in_specs=[pl.BlockSpec((batch, C, RW_IN), lambda c: (0, c, 0)),
                  pl.BlockSpec((batch, SUBLANES, RW_IN), lambda c: (0, jnp.maximum(c * per8 - 1, 0), 0))]
                 + [_const_spec(a.shape) for a in consts],
        out_specs=pl.BlockSpec((batch, C, RW_W), lambda c: (0, c, 0)),
        out_shape=jax.ShapeDtypeStruct((batch, seq, RW_W), BF16),
        scratch_shapes=[pltpu.VMEM((batch * RW_HEADS, RW_HEAD_DIM, RW_HEAD_DIM), F32)],
        compiler_params=_params("arbitrary"),
        name="rwkv7",
    )(rw3, rw3, *consts)
    return y.reshape(batch * seq, RW_W)


@functools.lru_cache(maxsize=None)
def _rope_tables(seq):
    half = ROPE_DIM // 2
    inv_freq = ROPE_THETA ** (-np.arange(half, dtype=np.float64) * 2.0 / ROPE_DIM)
    ang = np.arange(seq, dtype=np.float64)[:, None] * inv_freq[None, :]
    cos, sin = np.cos(ang), np.sin(ang)
    ones = np.ones((seq, ATT_HEAD_DIM - ROPE_DIM))
    zeros = np.zeros((seq, ATT_HEAD_DIM - ROPE_DIM))
    zh = np.zeros((seq, half))
    reps = LANES // ATT_HEAD_DIM
    cos_t = np.concatenate([cos, cos, ones] * reps, axis=1).astype(np.float32)
    sin_lo = np.concatenate([-sin, zh, zeros] * reps, axis=1).astype(np.float32)
    sin_hi = np.concatenate([zh, sin, zeros] * reps, axis=1).astype(np.float32)
    return cos_t, sin_lo, sin_hi


def _row(v):
    return v.reshape(1, -1).astype(F32)


def _col(v):
    return v.reshape(-1, 1).astype(F32)


def kernel(x, mix_norm, mlp_norm, w_up, w_down, e_w_in, e_w_out, att_sinks, ml_conv_w, ml_conv_b, ml_i_bias,
           ml_f_bias, ml_norm, o_w_in, o_w_out, ssm_conv_w, ssm_conv_b, ssm_dt_bias, ssm_a_log, ssm_d, ssm_norm,
           rw_mu, rw_w0, rw_w2, rw_a0, rw_a2, rw_g2, rw_k_k, rw_k_a, rw_r_k, rw_ln_w, rw_ln_b, final_norm):
    batch, seq, _ = x.shape
    h = x.reshape(batch * seq, D_MODEL)
    fg = _row(final_norm)

    w = e_w_in[0].astype(BF16)
    o = 0
    parts = []
    for width in (ATT_Q_W, ATT_KV_W, ATT_KV_W, 2 * ML_QK_W, ML_V_W, ML_V_W):
        parts.append(w[:, o:o + width])
        o += width
    parts.append(w[:, o:o + 2 * ML_HEADS].T)
    qa, ka, va, qkm, vm, om, gt = _inproj_even(h, _row(mix_norm[0]), _rope_tables(seq), parts, seq)
    gate_bias = _col(jnp.concatenate([ml_i_bias[0], ml_f_bias[0]]))
    h = _layer0(qa, ka, va, _row(att_sinks[0]), qkm, vm, om, gt, ml_conv_w[0].astype(F32), _row(ml_conv_b[0]),
                gate_bias, _row(ml_norm[0]), h, e_w_out[0].astype(BF16), _row(mlp_norm[0]), w_up[0].astype(BF16),
                w_down[0].astype(BF16), fg, batch, seq)

    w = o_w_in[0].astype(BF16)
    o_x = SSM_INNER
    o_dt = o_x + SSM_CONV_W
    o_rw = o_dt + SSM_HEADS
    parts = [w[:, :o_x], w[:, o_x:o_dt], w[:, o_dt:o_rw].T, w[:, o_rw:]]
    z, xbc, dt_t, rw = _inproj_odd(h, _row(mix_norm[1]), parts, seq)
    rw_consts = (_row(rw_mu[0]), _row(rw_w0[0]), rw_w2[0].astype(BF16), _row(rw_a0[0]), rw_a2[0].astype(BF16),
                 rw_g2[0].astype(BF16), _row(rw_k_k[0]), _row(rw_k_a[0]), _row(rw_r_k[0]), _row(rw_ln_w[0]),
                 _row(rw_ln_b[0]))
    y_rw = _rwkv(rw, rw_consts, batch, seq)
    wo = o_w_out[0].astype(BF16)
    h = _layer1(xbc, z, dt_t, ssm_conv_w[0].astype(F32), _row(ssm_conv_b[0]), _col(ssm_dt_bias[0]),
                _col(ssm_a_log[0]), _row(jnp.repeat(ssm_d[0], SSM_HEAD_DIM)), _row(ssm_norm[0]), y_rw, h,
                wo[:SSM_INNER], wo[SSM_INNER:], _row(mlp_norm[1]), w_up[1].astype(BF16), w_down[1].astype(BF16), fg,
                batch, seq)
    return h.reshape(batch, seq, D_MODEL)
```

```python
import functools

import numpy as np

import jax
import jax.numpy as jnp
from jax import lax
from jax.experimental import pallas as pl
from jax.experimental.pallas import tpu as pltpu

F32 = jnp.float32
BF16 = jnp.bfloat16
HI = lax.Precision.HIGHEST

D_MODEL = 1024
D_FF = 4 * D_MODEL
NORM_EPS = 1e-5
CONV_TAPS = 4

ATT_HEAD_DIM = 64
ATT_Q_HEADS = 8
ATT_KV_HEADS = 2
ATT_GROUP = ATT_Q_HEADS // ATT_KV_HEADS
ATT_WINDOW = 128
ROPE_THETA = 500000.0
ROPE_DIM = 16
ATT_Q_W = ATT_Q_HEADS * ATT_HEAD_DIM
ATT_KV_W = ATT_KV_HEADS * ATT_HEAD_DIM

ML_HEADS = 4
ML_V_DIM = 128
ML_QK_DIM = 64
ML_CHUNK = 128
ML_NORM_EPS = 1e-6
ML_QK_W = ML_HEADS * ML_QK_DIM
ML_V_W = ML_HEADS * ML_V_DIM

SSM_HEAD_DIM = 64
SSM_HEADS = 8
SSM_GROUPS = 2
SSM_HEADS_PER_GROUP = SSM_HEADS // SSM_GROUPS
SSM_STATE = 128
SSM_CHUNK = 128
SSM_INNER = SSM_HEADS * SSM_HEAD_DIM
SSM_BC = SSM_GROUPS * SSM_STATE
SSM_CONV_W = SSM_INNER + 2 * SSM_BC

RW_HEAD_DIM = 64
RW_HEADS = 8
RW_W = RW_HEADS * RW_HEAD_DIM
RW_DECAY_LORA = 64
RW_AAA_LORA = 64
RW_GATE_LORA = 128
RW_IN = 3 * RW_W + RW_DECAY_LORA + RW_AAA_LORA + RW_GATE_LORA
RW_LN_EPS = 64e-5
RW_CHUNK = 64
RW_EXP_MINUS_HALF = float(np.exp(-0.5))

V7X_VMEM_BYTES = 64 * 1024 * 1024
VMEM_LIMIT_BYTES = V7X_VMEM_BYTES - 6 * 1024 * 1024
SUBLANES = 8
LANES = 128

PROJ_ROWS = 512
FF_CHUNK = 512


def _params(*sem):
    return pltpu.CompilerParams(dimension_semantics=sem, vmem_limit_bytes=VMEM_LIMIT_BYTES)


def _const_spec(shape):
    nd = len(shape)
    return pl.BlockSpec(shape, lambda *_: (0,) * nd, pipeline_mode=pl.Buffered(1))


def _iota(shape, dim):
    return lax.broadcasted_iota(jnp.int32, shape, dim)


def _tri_lower(n):
    return (_iota((n, n), 1) <= _iota((n, n), 0)).astype(F32)


def _tri_upper(n):
    return (_iota((n, n), 0) <= _iota((n, n), 1)).astype(F32)


def _eye(n):
    return (_iota((n, n), 0) == _iota((n, n), 1)).astype(F32)


def _dot(a, b, precision=None):
    return jnp.dot(a, b, preferred_element_type=F32, precision=precision)


def _dot_nt(a, b, precision=None):
    return lax.dot_general(a, b, (((1,), (1,)), ((), ())), preferred_element_type=F32, precision=precision)


def _sigmoid(x):
    return 1.0 / (1.0 + jnp.exp(-x))


def _softplus(x):
    return jnp.maximum(x, 0.0) + jnp.log1p(jnp.exp(-jnp.abs(x)))


def _silu(x):
    return x * _sigmoid(x)


def _rmsnorm(x, g):
    y = x * lax.rsqrt(jnp.mean(x * x, axis=-1, keepdims=True) + NORM_EPS)
    return y * g


def _shifted_rows(x, tail, k):
    rolled = pltpu.roll(x, k, axis=0)
    tail_rolled = pltpu.roll(tail, k, axis=0)
    first = jnp.where(_iota((SUBLANES, 1), 0) < k, tail_rolled, rolled[:SUBLANES])
    return jnp.concatenate([first, rolled[SUBLANES:]], axis=0)


def _causal_conv_silu(x, tail, w, b):
    out = b + x * w[CONV_TAPS - 1:CONV_TAPS, :]
    for k in range(1, CONV_TAPS):
        out = out + _shifted_rows(x, tail, k) * w[CONV_TAPS - 1 - k:CONV_TAPS - k, :]
    return _silu(out)


def _rope(x, cos, sin_lo, sin_hi, reps):
    w = x.shape[1]
    c = jnp.concatenate([cos] * reps, axis=1)
    s_lo = jnp.concatenate([sin_lo] * reps, axis=1)
    s_hi = jnp.concatenate([sin_hi] * reps, axis=1)
    half = ROPE_DIM // 2
    nxt = pltpu.roll(x, w - half, axis=1)
    prv = pltpu.roll(x, half, axis=1)
    return x * c + nxt * s_lo + prv * s_hi


def _inproj_even_kernel(x_ref, g_ref, cos_ref, slo_ref, shi_ref, wq_ref, wk_ref, wv_ref, wqk_ref, wvm_ref,
                        wom_ref, wgt_ref, qa_ref, ka_ref, va_ref, qkm_ref, vm_ref, om_ref, gt_ref):
    hb = _rmsnorm(x_ref[...], g_ref[...]).astype(BF16)
    cos, s_lo, s_hi = cos_ref[...], slo_ref[...], shi_ref[...]
    q = _rope(_dot(hb, wq_ref[...]), cos, s_lo, s_hi, ATT_Q_W // LANES)
    qa_ref[...] = (q * (ATT_HEAD_DIM ** -0.5)).astype(BF16)
    ka_ref[...] = _rope(_dot(hb, wk_ref[...]), cos, s_lo, s_hi, ATT_KV_W // LANES).astype(BF16)
    va_ref[...] = _dot(hb, wv_ref[...]).astype(BF16)
    qkm_ref[...] = _dot(hb, wqk_ref[...])
    vm_ref[...] = _dot(hb, wvm_ref[...]).astype(BF16)
    om_ref[...] = _dot(hb, wom_ref[...])
    gt_ref[...] = _dot_nt(wgt_ref[...], hb)


def _inproj_even(x2, g, tables, w, seq):
    n = x2.shape[0]
    tm = PROJ_ROWS
    per_seq = seq // tm
    row = lambda width: pl.BlockSpec((tm, width), lambda i: (i, 0))
    tab = pl.BlockSpec((tm, LANES), lambda i: (i % per_seq, 0))
    wq, wk, wv, wqk, wvm, wom, wgt = w
    out_shapes = (
        jax.ShapeDtypeStruct((n, ATT_Q_W), BF16), jax.ShapeDtypeStruct((n, ATT_KV_W), BF16),
        jax.ShapeDtypeStruct((n, ATT_KV_W), BF16), jax.ShapeDtypeStruct((n, 2 * ML_QK_W), F32),
        jax.ShapeDtypeStruct((n, ML_V_W), BF16), jax.ShapeDtypeStruct((n, ML_V_W), F32),
        jax.ShapeDtypeStruct((n // seq, 2 * ML_HEADS, seq), F32))
    return pl.pallas_call(
        _inproj_even_kernel,
        grid=(n // tm,),
        in_specs=[row(D_MODEL), _const_spec((1, D_MODEL)), tab, tab, tab] + [_const_spec(a.shape) for a in w],
        out_specs=(row(ATT_Q_W), row(ATT_KV_W), row(ATT_KV_W), row(2 * ML_QK_W), row(ML_V_W), row(ML_V_W),
                   pl.BlockSpec((None, 2 * ML_HEADS, tm), lambda i: (i // per_seq, 0, i % per_seq))),
        out_shape=out_shapes,
        compiler_params=_params("arbitrary"),
        name="inproj_even",
    )(x2, g, *tables, wq, wk, wv, wqk, wvm, wom, wgt)


def _attention_stages(q_ref, kc_ref, kp_ref, vc_ref, vp_ref, sink_ref, blk, nb, emit):
    w = ATT_WINDOW
    dh = ATT_HEAD_DIM
    G = ATT_GROUP
    i = _iota((G * w, 2 * w), 0) % w
    j = _iota((G * w, 2 * w), 1)
    mask = (j > i) & (j <= i + w) & ((j >= w) | (blk > 0))
    sinks = sink_ref[...]
    problems = [(b, g) for b in range(nb) for g in range(ATT_KV_HEADS)]
    scores, values, sink_cols = [], [], []
    for b, g in problems:
        q = q_ref[b]
        qs = jnp.concatenate([q[:, (g * G + a) * dh:(g * G + a + 1) * dh] for a in range(G)], axis=0)
        ks = jnp.concatenate([kp_ref[b][:, g * dh:(g + 1) * dh], kc_ref[b][:, g * dh:(g + 1) * dh]], axis=0)
        values.append(jnp.concatenate([vp_ref[b][:, g * dh:(g + 1) * dh], vc_ref[b][:, g * dh:(g + 1) * dh]], axis=0))
        scores.append(_dot_nt(qs, ks))
        sink_cols.append(jnp.concatenate(
            [jnp.broadcast_to(sinks[:, g * G + a:g * G + a + 1], (w, 1)) for a in range(G)], axis=0))
    yield
    probs, dens = [], []
    for s, sink in zip(scores, sink_cols):
        s = jnp.where(mask, s, -jnp.inf)
        m = jnp.maximum(jnp.max(s, axis=-1, keepdims=True), sink)
        p = jnp.exp(s - m)
        dens.append(jnp.sum(p, axis=-1, keepdims=True) + jnp.exp(sink - m))
        probs.append(p.astype(BF16))
    pv = [_dot(p, v) for p, v in zip(probs, values)]
    yield
    outs = [o / d for o, d in zip(pv, dens)]
    for b in range(nb):
        heads = [outs[b * ATT_KV_HEADS + g][a * w:(a + 1) * w] for g in range(ATT_KV_HEADS) for a in range(G)]
        emit(b, jnp.concatenate(heads, axis=1).astype(BF16))


def _mlstm_stages(qk_ref, tail_ref, v_ref, o_ref, gt_ref, cw_ref, cb_ref, gb_ref, nw_ref, c_s, m_s, chunk, nb, emit):
    L = ML_CHUNK
    H = ML_HEADS
    DV = ML_V_DIM
    NG = 2 * H

    cw, cb = cw_ref[...], cb_ref[...]
    qk = [_causal_conv_silu(qk_ref[b], jnp.where(chunk == 0, 0.0, tail_ref[b]), cw, cb) for b in range(nb)]

    g = jnp.concatenate([gt_ref[b] + gb_ref[...] for b in range(nb)], axis=0)
    lg = jnp.where(_iota(g.shape, 0) % NG < H, g, -_softplus(-g))
    tri_l, tri_u, eye = _tri_lower(L), _tri_upper(L), _eye(L)
    cum_row = _dot(lg, tri_u, HI)
    lg_col = _dot_nt(eye, lg, HI)
    cum_col = _dot_nt(tri_l, lg, HI)
    causal = _iota((L, L), 1) <= _iota((L, L), 0)

    chains = [(b, h) for b in range(nb) for h in range(H)]
    idx = lambda b, h: b * H + h
    q_c = [qk[b][:, h * ML_QK_DIM:(h + 1) * ML_QK_DIM] for b, h in chains]
    k_c = [qk[b][:, ML_QK_W + h * ML_QK_DIM:ML_QK_W + (h + 1) * ML_QK_DIM] * (ML_QK_DIM ** -0.5) for b, h in chains]
    ones = jnp.ones((L, DV), BF16)
    v_c = [jnp.concatenate([v_ref[b][:, h * DV:(h + 1) * DV], ones], axis=1) for b, h in chains]
    qb_c = [q.astype(BF16) for q in q_c]
    c_prev = [c_s[idx(b, h)] for b, h in chains]
    m_prev = [m_s[idx(b, h):idx(b, h) + 1, 0:1] for b, h in chains]

    s_qk = [_dot_nt(qb, k.astype(BF16)) for qb, k in zip(qb_c, k_c)]
    inter = [_dot(qb, c.astype(BF16)) for qb, c in zip(qb_c, c_prev)]

    yield
    gi_c = [b * NG + h for b, h in chains]
    b_col = [cum_col[:, gi + H:gi + H + 1] for gi in gi_c]
    b_last = [bc[L - 1:L, :] for bc in b_col]
    dlog = [jnp.where(causal, bc - cum_row[gi + H:gi + H + 1, :] + lg[gi:gi + 1, :], -jnp.inf)
            for gi, bc in zip(gi_c, b_col)]
    row_max = [jnp.max(d, axis=1, keepdims=True) for d in dlog]
    a_loc = [bl - bc + lg_col[:, gi:gi + 1] for gi, bc, bl in zip(gi_c, b_col, b_last)]
    m_loc = [jnp.max(a, axis=0, keepdims=True) for a in a_loc]
    inter_log = [bc + mp for bc, mp in zip(b_col, m_prev)]
    m_t = [jnp.maximum(il, rm) for il, rm in zip(inter_log, row_max)]
    qk_w = [s * jnp.exp(d - mt) for s, d, mt in zip(s_qk, dlog, m_t)]
    w_inter = [jnp.exp(il - mt) for il, mt in zip(inter_log, m_t)]
    kw = [k * jnp.exp(a - ml) for k, a, ml in zip(k_c, a_loc, m_loc)]

    intra = [_dot(w.astype(BF16), v) for w, v in zip(qk_w, v_c)]
    c_loc = [lax.dot_general(w.astype(BF16), v, (((0,), (0,)), ((), ())), preferred_element_type=F32)
             for w, v in zip(kw, v_c)]

    yield
    both = [a + w * e for a, w, e in zip(intra, w_inter, inter)]
    xs = [nd[:, :DV] / jnp.maximum(jnp.abs(nd[:, DV:DV + 1]), jnp.exp(-mt)) for nd, mt in zip(both, m_t)]
    mus = [jnp.mean(x, axis=-1, keepdims=True) for x in xs]
    cen = [x - mu for x, mu in zip(xs, mus)]
    var = [jnp.mean(jnp.square(c), axis=-1, keepdims=True) for c in cen]
    outs = [c * lax.rsqrt(vr + ML_NORM_EPS) for c, vr in zip(cen, var)]
    for i, (b, h) in enumerate(chains):
        m_new = jnp.maximum(b_last[i] + m_prev[i], m_loc[i])
        fa = jnp.exp(b_last[i] + m_prev[i] - m_new)
        fb = jnp.exp(m_loc[i] - m_new)
        c_s[idx(b, h)] = fa * c_prev[i] + fb * c_loc[i]
        m_s[idx(b, h):idx(b, h) + 1, 0:1] = m_new

    nw = nw_ref[...]
    for b in range(nb):
        hm = jnp.concatenate(outs[b * H:(b + 1) * H], axis=1) * nw
        emit(b, (hm * _sigmoid(o_ref[b])).astype(BF16))


def _outproj_mlp_stages(y_parts, h, g_ref, wu_ref, wd_ref, fg_ref, final_norm, emit):
    h1 = h
    for y, w_ref in y_parts:
        h1 = h1 + _dot(y, w_ref[...])
    hb = _rmsnorm(h1, g_ref[...]).astype(BF16)
    n_chunks = D_FF // FF_CHUNK
    mlp = None
    for c in range(n_chunks):
        u = _dot(hb, wu_ref[:, c * FF_CHUNK:(c + 1) * FF_CHUNK])
        a = jnp.square(jnp.maximum(u, 0.0)).astype(BF16)
        d = _dot(a, wd_ref[c * FF_CHUNK:(c + 1) * FF_CHUNK, :])
        mlp = d if mlp is None else mlp + d
        if c + 1 in MLP_SPLIT:
            yield
    h2 = h1 + mlp
    if final_norm:
        h2 = _rmsnorm(h2, fg_ref[...])
    emit(h2)


MLP_SPLIT = (4,)


def _drive(mixers, mlp):
    for m in mixers:
        next(m)
    next(mlp)
    for m in mixers:
        next(m)
    for _ in mlp:
        pass
    for m in mixers:
        for _ in m:
            pass


def _layer0_kernel(q_ref, kc_ref, kp_ref, vc_ref, vp_ref, sink_ref,
                   qk_ref, tail_ref, vm_ref, om_ref, gt_ref, cw_ref, cb_ref, gb_ref, nw_ref,
                   h_ref, wo_ref, g_ref, wu_ref, wd_ref, fg_ref,
                   o_ref, c_s, m_s, y_s, *, nb, nc):
    step = pl.program_id(0)
    chunk = jnp.minimum(step, nc - 1)
    L = ML_CHUNK

    @pl.when(step == 0)
    def _():
        c_s[...] = jnp.zeros_like(c_s)
        m_s[...] = jnp.zeros_like(m_s)
        y_s[...] = jnp.zeros_like(y_s)

    wr = step % 2
    rd = 1 - wr

    def emit_att(b, block):
        y_s[wr, b * L:(b + 1) * L, 0:ATT_Q_W] = block

    def emit_ml(b, block):
        y_s[wr, b * L:(b + 1) * L, ATT_Q_W:ATT_Q_W + ML_V_W] = block

    def emit_out(block):
        o_ref[...] = block.reshape(nb, L, D_MODEL)

    mixers = [
        _attention_stages(q_ref, kc_ref, kp_ref, vc_ref, vp_ref, sink_ref, chunk, nb, emit_att),
        _mlstm_stages(qk_ref, tail_ref, vm_ref, om_ref, gt_ref, cw_ref, cb_ref, gb_ref, nw_ref, c_s, m_s, chunk, nb,
                      emit_ml),
    ]
    mlp = _outproj_mlp_stages([(y_s[rd], wo_ref)], h_ref[...].reshape(nb * L, D_MODEL), g_ref, wu_ref, wd_ref,
                              fg_ref, False, emit_out)
    _drive(mixers, mlp)


def _layer0(qa, ka, va, sinks, qkm, vm, om, gt, conv_w, conv_b, gate_bias, norm_w, h, wo, g, wu, wd, fg, batch, seq):
    L = ML_CHUNK
    nc = seq // L
    per8 = L // SUBLANES
    three = lambda t: t.reshape(batch, seq, t.shape[-1])
    mix = lambda c: jnp.minimum(c, nc - 1)
    cur = lambda width: pl.BlockSpec((batch, L, width), lambda c: (0, mix(c), 0))
    prev = lambda width: pl.BlockSpec((batch, L, width), lambda c: (0, jnp.maximum(mix(c) - 1, 0), 0))
    tail = pl.BlockSpec((batch, SUBLANES, 2 * ML_QK_W), lambda c: (0, jnp.maximum(mix(c) * per8 - 1, 0), 0))
    late = pl.BlockSpec((batch, L, D_MODEL), lambda c: (0, jnp.maximum(c - 1, 0), 0))
    q3, k3, v3, qk3 = three(qa), three(ka), three(va), three(qkm)
    consts = (conv_w, conv_b, gate_bias, norm_w)
    weights = (wo, g, wu, wd, fg)
    out = pl.pallas_call(
        functools.partial(_layer0_kernel, nb=batch, nc=nc),
        grid=(nc + 1,),
        in_specs=[cur(ATT_Q_W), cur(ATT_KV_W), prev(ATT_KV_W), cur(ATT_KV_W), prev(ATT_KV_W),
                  _const_spec((1, ATT_Q_HEADS)),
                  cur(2 * ML_QK_W), tail, cur(ML_V_W), cur(ML_V_W),
                  pl.BlockSpec((batch, 2 * ML_HEADS, L), lambda c: (0, 0, mix(c)))]
                 + [_const_spec(a.shape) for a in consts] + [late] + [_const_spec(a.shape) for a in weights],
        out_specs=late,
        out_shape=jax.ShapeDtypeStruct((batch, seq, D_MODEL), F32),
        scratch_shapes=[pltpu.VMEM((batch * ML_HEADS, ML_QK_DIM, 2 * ML_V_DIM), F32),
                        pltpu.VMEM((batch * ML_HEADS, LANES), F32),
                        pltpu.VMEM((2, batch * L, ATT_Q_W + ML_V_W), BF16)],
        compiler_params=_params("arbitrary"),
        name="layer0_mix_mlp",
    )(q3, k3, k3, v3, v3, sinks, qk3, qk3, three(vm), three(om), gt, *consts, three(h), *weights)
    return out.reshape(batch * seq, D_MODEL)


def _inproj_odd_kernel(x_ref, g_ref, wz_ref, wx_ref, wdt_ref, wrw_ref, z_ref, xbc_ref, dt_ref, rw_ref):
    hb = _rmsnorm(x_ref[...], g_ref[...]).astype(BF16)
    z_ref[...] = _dot(hb, wz_ref[...])
    xbc_ref[...] = _dot(hb, wx_ref[...])
    dt_ref[...] = _dot_nt(wdt_ref[...], hb)
    rw_ref[...] = _dot(hb, wrw_ref[...])


def _inproj_odd(x2, g, w, seq):
    n = x2.shape[0]
    tm = PROJ_ROWS
    per_seq = seq // tm
    row = lambda width: pl.BlockSpec((tm, width), lambda i: (i, 0))
    out_shapes = (jax.ShapeDtypeStruct((n, SSM_INNER), F32), jax.ShapeDtypeStruct((n, SSM_CONV_W), F32),
                  jax.ShapeDtypeStruct((n // seq, SSM_HEADS, seq), F32), jax.ShapeDtypeStruct((n, RW_IN), F32))
    return pl.pallas_call(
        _inproj_odd_kernel,
        grid=(n // tm,),
        in_specs=[row(D_MODEL), _const_spec((1, D_MODEL))] + [_const_spec(a.shape) for a in w],
        out_specs=(row(SSM_INNER), row(SSM_CONV_W),
                   pl.BlockSpec((None, SSM_HEADS, tm), lambda i: (i // per_seq, 0, i % per_seq)), row(RW_IN)),
        out_shape=out_shapes,
        compiler_params=_params("arbitrary"),
        name="inproj_odd",
    )(x2, g, *w)


def _ssd_stages(xbc_ref, tail_ref, z_ref, dt_ref, cw_ref, cb_ref, dtb_ref, alog_ref, d_ref, nw_ref, s_s, chunk, nb,
                emit):
    L = SSM_CHUNK
    P = SSM_HEAD_DIM
    HG = SSM_HEADS_PER_GROUP
    NH = SSM_HEADS

    cw, cb_ = cw_ref[...], cb_ref[...]
    xbc = [_causal_conv_silu(xbc_ref[b], jnp.where(chunk == 0, 0.0, tail_ref[b]), cw, cb_) for b in range(nb)]

    dt_row = _softplus(jnp.concatenate([dt_ref[b] + dtb_ref[...] for b in range(nb)], axis=0))
    neg_a = -jnp.exp(jnp.concatenate([alog_ref[...]] * nb, axis=0))
    a_row = dt_row * neg_a
    tri_l, tri_u, eye = _tri_lower(L), _tri_upper(L), _eye(L)
    acum_row = _dot(a_row, tri_u, HI)
    acum_col = _dot_nt(tri_l, a_row, HI)
    dt_col = _dot_nt(eye, dt_row, HI)
    causal = _iota((L, L), 1) <= _iota((L, L), 0)
    eye_b = eye.astype(BF16)

    groups = [(b, g) for b in range(nb) for g in range(SSM_GROUPS)]
    gidx = lambda b, g: b * SSM_GROUPS + g
    bg = [xbc[b][:, SSM_INNER + g * SSM_STATE:SSM_INNER + (g + 1) * SSM_STATE].astype(BF16) for b, g in groups]
    cg = [xbc[b][:, SSM_INNER + SSM_BC + g * SSM_STATE:SSM_INNER + SSM_BC + (g + 1) * SSM_STATE].astype(BF16)
          for b, g in groups]
    s_prev = [s_s[gidx(b, g)] for b, g in groups]

    cb = [_dot_nt(c, bq) for c, bq in zip(cg, bg)]
    bg_t = [_dot_nt(eye_b, bq).astype(BF16) for bq in bg]
    y_off = [_dot(c, sp.astype(BF16)) for c, sp in zip(cg, s_prev)]

    yield
    heads = [(b, g, j) for b, g in groups for j in range(HG)]
    scores, xcs, e_col, xws, cds = [], [], [], [], []
    for b, g, j in heads:
        h = g * HG + j
        col = b * NH + h
        a_col = acum_col[:, col:col + 1]
        a_last = a_col[L - 1:L, :]
        decay = jnp.exp(jnp.where(causal, a_col - acum_row[col:col + 1, :], -jnp.inf))
        xc = xbc[b][:, h * P:(h + 1) * P] * dt_col[:, col:col + 1]
        scores.append((cb[gidx(b, g)] * decay).astype(BF16))
        xcs.append(xc.astype(BF16))
        e_col.append(jnp.exp(a_col))
        xws.append(xc * jnp.exp(a_last - a_col))
        cds.append(jnp.broadcast_to(jnp.exp(a_last), (1, P)))

    y_diag = [_dot(sc, xc) for sc, xc in zip(scores, xcs)]
    states = [_dot(bg_t[i], jnp.concatenate(xws[i * HG:(i + 1) * HG], axis=1).astype(BF16))
              for i in range(len(groups))]

    yield
    for i, (b, g) in enumerate(groups):
        s_s[gidx(b, g)] = s_prev[i] * jnp.concatenate(cds[i * HG:(i + 1) * HG], axis=1) + states[i]

    gw = SSM_INNER // SSM_GROUPS
    for b in range(nb):
        ys = []
        for g in range(SSM_GROUPS):
            for j in range(HG):
                i = (b * SSM_GROUPS + g) * HG + j
                ys.append(y_diag[i] + y_off[gidx(b, g)][:, j * P:(j + 1) * P] * e_col[i])
        xs = xbc[b][:, :SSM_INNER]
        y = jnp.concatenate(ys, axis=1) + d_ref[...] * xs
        y = y * _silu(z_ref[b])
        normed = []
        for g in range(SSM_GROUPS):
            yg = y[:, g * gw:(g + 1) * gw]
            normed.append(yg * lax.rsqrt(jnp.mean(yg * yg, axis=-1, keepdims=True) + NORM_EPS))
        emit(b, (jnp.concatenate(normed, axis=1) * nw_ref[...]).astype(BF16))


def _layer1_kernel(xbc_ref, tail_ref, z_ref, dt_ref, cw_ref, cb_ref, dtb_ref, alog_ref, d_ref, nw_ref,
                   yrw_ref, h_ref, wa_ref, wb_ref, g_ref, wu_ref, wd_ref, fg_ref,
                   o_ref, s_s, y_s, *, nb, nc):
    step = pl.program_id(0)
    chunk = jnp.minimum(step, nc - 1)
    L = SSM_CHUNK

    @pl.when(step == 0)
    def _():
        s_s[...] = jnp.zeros_like(s_s)
        y_s[...] = jnp.zeros_like(y_s)

    wr = step % 2
    rd = 1 - wr

    def emit_ssm(b, block):
        y_s[wr, b * L:(b + 1) * L, :] = block

    def emit_out(block):
        o_ref[...] = block.reshape(nb, L, D_MODEL)

    mixers = [_ssd_stages(xbc_ref, tail_ref, z_ref, dt_ref, cw_ref, cb_ref, dtb_ref, alog_ref, d_ref, nw_ref, s_s,
                          chunk, nb, emit_ssm)]
    mlp = _outproj_mlp_stages([(y_s[rd], wa_ref), (yrw_ref[...].reshape(nb * L, RW_W), wb_ref)],
                              h_ref[...].reshape(nb * L, D_MODEL), g_ref, wu_ref, wd_ref, fg_ref, True, emit_out)
    _drive(mixers, mlp)


def _layer1(xbc, z, dt_t, conv_w, conv_b, dt_bias, a_log, d_row, norm_w, y_rw, h, wa, wb, g, wu, wd, fg, batch, seq):
    L = SSM_CHUNK
    nc = seq // L
    per8 = L // SUBLANES
    three = lambda t: t.reshape(batch, seq, t.shape[-1])
    mix = lambda c: jnp.minimum(c, nc - 1)
    cur = lambda width: pl.BlockSpec((batch, L, width), lambda c: (0, mix(c), 0))
    tail = pl.BlockSpec((batch, SUBLANES, SSM_CONV_W), lambda c: (0, jnp.maximum(mix(c) * per8 - 1, 0), 0))
    late = lambda width: pl.BlockSpec((batch, L, width), lambda c: (0, jnp.maximum(c - 1, 0), 0))
    consts = (conv_w, conv_b, dt_bias, a_log, d_row, norm_w)
    weights = (wa, wb, g, wu, wd, fg)
    x3 = three(xbc)
    out = pl.pallas_call(
        functools.partial(_layer1_kernel, nb=batch, nc=nc),
        grid=(nc + 1,),
        in_specs=[cur(SSM_CONV_W), tail, cur(SSM_INNER),
                  pl.BlockSpec((batch, SSM_HEADS, L), lambda c: (0, 0, mix(c)))]
                 + [_const_spec(a.shape) for a in consts] + [late(RW_W), late(D_MODEL)]
                 + [_const_spec(a.shape) for a in weights],
        out_specs=late(D_MODEL),
        out_shape=jax.ShapeDtypeStruct((batch, seq, D_MODEL), F32),
        scratch_shapes=[pltpu.VMEM((batch * SSM_GROUPS, SSM_STATE, SSM_HEADS_PER_GROUP * SSM_HEAD_DIM), F32),
                        pltpu.VMEM((2, batch * L, SSM_INNER), BF16)],
        compiler_params=_params("arbitrary"),
        name="layer1_mix_mlp",
    )(x3, x3, three(z), dt_t, *consts, three(y_rw), three(h), *weights)
    return out.reshape(batch * seq, D_MODEL)


def _split_bf16(x):
    hi = x.astype(BF16)
    lo = (x - hi.astype(F32)).astype(BF16)
    return hi, lo


def _dot_ones_exact(t, ones_matrix):
    hi, lo = _split_bf16(t)
    return _dot(hi, ones_matrix) + _dot(lo, ones_matrix)


def _unit_lower_inverses(mats, eye, n):
    xbs = [(eye - a).astype(BF16) for a in mats]
    mats = [a.astype(BF16) for a in mats]
    steps = 0
    terms = 2
    while terms < n:
        terms *= 2
        steps += 1
    for _ in range(steps):
        xfs = [xb.astype(F32) for xb in xbs]
        resids = [(eye - xf - _dot(a, xb)).astype(BF16) for a, xb, xf in zip(mats, xbs, xfs)]
        xbs = [(xf + _dot(xb, e)).astype(BF16) for xb, xf, e in zip(xbs, xfs, resids)]
    return xbs


def _rwkv_kernel(rw_ref, tail_ref, mu_ref, w0_ref, w2_ref, a0_ref, a2_ref, g2_ref, kk_ref, ka_ref, rk_ref,
                 lnw_ref, lnb_ref, y_ref, st_s, *, nb):
    chunk = pl.program_id(0)
    C = RW_CHUNK
    D = RW_HEAD_DIM
    R = nb * C

    @pl.when(chunk == 0)
    def _():
        st_s[...] = jnp.zeros_like(st_s)

    x = rw_ref[...].reshape(R, RW_IN)
    rolled = pltpu.roll(x, 1, axis=0)
    first_row = _iota((C, 1), 0) == 0
    prev = []
    for b in range(nb):
        before = jnp.where(chunk == 0, 0.0, tail_ref[b, SUBLANES - 1:SUBLANES, :])
        prev.append(jnp.where(first_row, before, rolled[b * C:(b + 1) * C]))
    prev = jnp.concatenate(prev, axis=0)
    xm = x + (prev - x) * mu_ref[...]
    r = xm[:, 0:RW_W]
    k = xm[:, RW_W:2 * RW_W]
    v = xm[:, 2 * RW_W:3 * RW_W]
    o = 3 * RW_W
    wd = xm[:, o:o + RW_DECAY_LORA]
    ad = xm[:, o + RW_DECAY_LORA:o + RW_DECAY_LORA + RW_AAA_LORA]
    gd = xm[:, o + RW_DECAY_LORA + RW_AAA_LORA:]

    z = w0_ref[...] + _dot(jnp.tanh(wd).astype(BF16), w2_ref[...])
    ld = -RW_EXP_MINUS_HALF * _sigmoid(z)
    a = _sigmoid(a0_ref[...] + _dot(ad.astype(BF16), a2_ref[...]))
    gate = _dot(_sigmoid(gd).astype(BF16), g2_ref[...])

    half = RW_W // 2
    same_head = (_iota((half, half), 0) // D == _iota((half, half), 1) // D).astype(BF16)
    head_sum = lambda t: jnp.concatenate(
        [_dot_ones_exact(t[:, :half], same_head), _dot_ones_exact(t[:, half:], same_head)], axis=1)

    kk = k * kk_ref[...]
    kk = kk * lax.rsqrt(jnp.maximum(head_sum(kk * kk), 1e-24))
    k2 = k * (1.0 + (a - 1.0) * ka_ref[...])
    bvec = kk * a

    tri = ((_iota((R, R), 1) <= _iota((R, R), 0)) & (_iota((R, R), 1) // C == _iota((R, R), 0) // C)).astype(BF16)
    ld_hi, ld_rest = _split_bf16(ld)
    ld_mid = ld - ld_hi.astype(F32)
    ld_lo = (ld_mid - ld_rest.astype(F32)).astype(BF16)
    cl = _dot(tri, ld_hi) + (_dot(tri, ld_rest) + _dot(tri, ld_lo))
    c_last = jnp.concatenate(
        [jnp.broadcast_to(cl[(b + 1) * C - 1:(b + 1) * C, :], (C, RW_W)) for b in range(nb)], axis=0)
    e_neg = jnp.exp(-cl)
    e_end = jnp.exp(c_last - cl)
    w_end = [jnp.exp(cl[(b + 1) * C - 1:(b + 1) * C, :]) for b in range(nb)]
    rw_ = r * jnp.exp(cl)
    kkw = kk * jnp.exp(cl - ld)
    bd = bvec * e_neg
    kd = k2 * e_neg
    bd_end = bvec * e_end
    kd_end = k2 * e_end

    g_row = _iota((2 * C, 2 * C), 0)
    g_col = _iota((2 * C, 2 * C), 1) % C
    keep = g_col < jnp.where(g_row < C, g_row, g_row - C + 1)
    eye = _eye(C)
    chains = [(b, h) for b in range(nb) for h in range(RW_HEADS)]
    blk = lambda t, b, h: t[b * C:(b + 1) * C, h * D:(h + 1) * D]
    kkw_c = [blk(kkw, b, h).astype(BF16) for b, h in chains]
    rw_c = [blk(rw_, b, h).astype(BF16) for b, h in chains]
    vb_c = [blk(v, b, h).astype(BF16) for b, h in chains]
    gms = [jnp.where(keep, _dot_nt(jnp.concatenate([kq, rq], axis=0),
                                   jnp.concatenate([blk(bd, b, h), blk(kd, b, h)], axis=0).astype(BF16)), 0.0)
           for (b, h), kq, rq in zip(chains, kkw_c, rw_c)]
    a_v = [_dot(g[:, C:].astype(BF16), vb) for g, vb in zip(gms, vb_c)]
    a_rb = [g[C:, :C].astype(BF16) for g in gms]
    invs = _unit_lower_inverses([g[:C, :C] for g in gms], eye, C)
    tws = [_dot(inv, jnp.concatenate([av[:C].astype(BF16), kq], axis=1))
           for inv, av, kq in zip(invs, a_v, kkw_c)]
    sts = [st_s[b * RW_HEADS + h] for b, h in chains]
    xss = [_dot_nt(jnp.concatenate([tw[:, D:].astype(BF16), rq], axis=0), st.astype(BF16))
           for tw, rq, st in zip(tws, rw_c, sts)]
    ubs = [(-tw[:, :D] - xs[:C]).astype(BF16) for tw, xs in zip(tws, xss)]
    ys = [xs[C:] + _dot(m, ub) + av[C:] for xs, m, ub, av in zip(xss, a_rb, ubs, a_v)]
    for (b, h), st, ub, vb in zip(chains, sts, ubs, vb_c):
        ends = jnp.concatenate([blk(bd_end, b, h), blk(kd_end, b, h)], axis=0).astype(BF16)
        upd = lax.dot_general(jnp.concatenate([ub, vb], axis=0), ends, (((0,), (0,)), ((), ())),
                              preferred_element_type=F32)
        st_s[b * RW_HEADS + h] = st * w_end[b][:, h * D:(h + 1) * D] + upd

    y = jnp.concatenate([jnp.concatenate(ys[b * RW_HEADS:(b + 1) * RW_HEADS], axis=1) for b in range(nb)], axis=0)
    mu = head_sum(y) * (1.0 / D)
    var = head_sum(jnp.square(y - mu)) * (1.0 / D)
    yn = (y - mu) * lax.rsqrt(var + RW_LN_EPS) * lnw_ref[...] + lnb_ref[...]
    bonus = head_sum(r * k2 * rk_ref[...]) * v
    y_ref[...] = ((yn + bonus) * gate).astype(BF16).reshape(nb, C, RW_W)


def _rwkv(rw, consts, batch, seq):
    C = RW_CHUNK
    nc = seq // C
    per8 = C // SUBLANES
    rw3 = rw.reshape(batch, seq, RW_IN)
    y = pl.pallas_call(
        functools.partial(_rwkv_kernel, nb=batch),
        grid=(nc,),
        in_specs=[pl.BlockSpec((batch, C, RW_IN), lambda c: (0, c, 0)),
                  pl.BlockSpec((batch, SUBLANES, RW_IN), lambda c: (0, jnp.maximum(c * per8 - 1, 0), 0))]
                 + [_const_spec(a.shape) for a in consts],
        out_specs=pl.BlockSpec((batch, C, RW_W), lambda c: (0, c, 0)),
        out_shape=jax.ShapeDtypeStruct((batch, seq, RW_W), BF16),
        scratch_shapes=[pltpu.VMEM((batch * RW_HEADS, RW_HEAD_DIM, RW_HEAD_DIM), F32)],
        compiler_params=_params("arbitrary"),
        name="rwkv7",
    )(rw3, rw3, *consts)
    return y.reshape(batch * seq, RW_W)


@functools.lru_cache(maxsize=None)
def _rope_tables(seq):
    half = ROPE_DIM // 2
    inv_freq = ROPE_THETA ** (-np.arange(half, dtype=np.float64) * 2.0 / ROPE_DIM)
    ang = np.arange(seq, dtype=np.float64)[:, None] * inv_freq[None, :]
    cos, sin = np.cos(ang), np.sin(ang)
    ones = np.ones((seq, ATT_HEAD_DIM - ROPE_DIM))
    zeros = np.zeros((seq, ATT_HEAD_DIM - ROPE_DIM))
    zh = np.zeros((seq, half))
    reps = LANES // ATT_HEAD_DIM
    cos_t = np.concatenate([cos, cos, ones] * reps, axis=1).astype(np.float32)
    sin_lo = np.concatenate([-sin, zh, zeros] * reps, axis=1).astype(np.float32)
    sin_hi = np.concatenate([zh, sin, zeros] * reps, axis=1).astype(np.float32)
    return cos_t, sin_lo, sin_hi


def _row(v):
    return v.reshape(1, -1).astype(F32)


def _col(v):
    return v.reshape(-1, 1).astype(F32)


def kernel(x, mix_norm, mlp_norm, w_up, w_down, e_w_in, e_w_out, att_sinks, ml_conv_w, ml_conv_b, ml_i_bias,
           ml_f_bias, ml_norm, o_w_in, o_w_out, ssm_conv_w, ssm_conv_b, ssm_dt_bias, ssm_a_log, ssm_d, ssm_norm,
           rw_mu, rw_w0, rw_w2, rw_a0, rw_a2, rw_g2, rw_k_k, rw_k_a, rw_r_k, rw_ln_w, rw_ln_b, final_norm):
    batch, seq, _ = x.shape
    h = x.reshape(batch * seq, D_MODEL)
    fg = _row(final_norm)

    w = e_w_in[0].astype(BF16)
    o = 0
    parts = []
    for width in (ATT_Q_W, ATT_KV_W, ATT_KV_W, 2 * ML_QK_W, ML_V_W, ML_V_W):
        parts.append(w[:, o:o + width])
        o += width
    parts.append(w[:, o:o + 2 * ML_HEADS].T)
    qa, ka, va, qkm, vm, om, gt = _inproj_even(h, _row(mix_norm[0]), _rope_tables(seq), parts, seq)
    gate_bias = _col(jnp.concatenate([ml_i_bias[0], ml_f_bias[0]]))
    h = _layer0(qa, ka, va, _row(att_sinks[0]), qkm, vm, om, gt, ml_conv_w[0].astype(F32), _row(ml_conv_b[0]),
                gate_bias, _row(ml_norm[0]), h, e_w_out[0].astype(BF16), _row(mlp_norm[0]), w_up[0].astype(BF16),
                w_down[0].astype(BF16), fg, batch, seq)

    w = o_w_in[0].astype(BF16)
    o_x = SSM_INNER
    o_dt = o_x + SSM_CONV_W
    o_rw = o_dt + SSM_HEADS
    parts = [w[:, :o_x], w[:, o_x:o_dt], w[:, o_dt:o_rw].T, w[:, o_rw:]]
    z, xbc, dt_t, rw = _inproj_odd(h, _row(mix_norm[1]), parts, seq)
    rw_consts = (_row(rw_mu[0]), _row(rw_w0[0]), rw_w2[0].astype(BF16), _row(rw_a0[0]), rw_a2[0].astype(BF16),
                 rw_g2[0].astype(BF16), _row(rw_k_k[0]), _row(rw_k_a[0]), _row(rw_r_k[0]), _row(rw_ln_w[0]),
                 _row(rw_ln_b[0]))
    y_rw = _rwkv(rw, rw_consts, batch, seq)
    wo = o_w_out[0].astype(BF16)
    h = _layer1(xbc, z, dt_t, ssm_conv_w[0].astype(F32), _row(ssm_conv_b[0]), _col(ssm_dt_bias[0]),
                _col(ssm_a_log[0]), _row(jnp.repeat(ssm_d[0], SSM_HEAD_DIM)), _row(ssm_norm[0]), y_rw, h,
                wo[:SSM_INNER], wo[SSM_INNER:], _row(mlp_norm[1]), w_up[1].astype(BF16), w_down[1].astype(BF16), fg,
                batch, seq)
    return h.reshape(batch, seq, D_MODEL)
```

```python
import functools

import numpy as np

import jax
import jax.numpy as jnp
from jax import lax
from jax.experimental import pallas as pl
from jax.experimental.pallas import tpu as pltpu

F32 = jnp.float32
BF16 = jnp.bfloat16
HI = lax.Precision.HIGHEST

D_MODEL = 1024
D_FF = 4 * D_MODEL
NORM_EPS = 1e-5
CONV_TAPS = 4

ATT_HEAD_DIM = 64
ATT_Q_HEADS = 8
ATT_KV_HEADS = 2
ATT_GROUP = ATT_Q_HEADS // ATT_KV_HEADS
ATT_WINDOW = 128
ROPE_THETA = 500000.0
ROPE_DIM = 16
ATT_Q_W = ATT_Q_HEADS * ATT_HEAD_DIM
ATT_KV_W = ATT_KV_HEADS * ATT_HEAD_DIM

ML_HEADS = 4
ML_V_DIM = 128
ML_QK_DIM = 64
ML_CHUNK = 128
ML_NORM_EPS = 1e-6
ML_QK_W = ML_HEADS * ML_QK_DIM
ML_V_W = ML_HEADS * ML_V_DIM

SSM_HEAD_DIM = 64
SSM_HEADS = 8
SSM_GROUPS = 2
SSM_HEADS_PER_GROUP = SSM_HEADS // SSM_GROUPS
SSM_STATE = 128
SSM_CHUNK = 128
SSM_INNER = SSM_HEADS * SSM_HEAD_DIM
SSM_BC = SSM_GROUPS * SSM_STATE
SSM_CONV_W = SSM_INNER + 2 * SSM_BC

RW_HEAD_DIM = 64
RW_HEADS = 8
RW_W = RW_HEADS * RW_HEAD_DIM
RW_DECAY_LORA = 64
RW_AAA_LORA = 64
RW_GATE_LORA = 128
RW_IN = 3 * RW_W + RW_DECAY_LORA + RW_AAA_LORA + RW_GATE_LORA
RW_LN_EPS = 64e-5
RW_CHUNK = 64
RW_EXP_MINUS_HALF = float(np.exp(-0.5))

V7X_VMEM_BYTES = 64 * 1024 * 1024
VMEM_LIMIT_BYTES = V7X_VMEM_BYTES - 6 * 1024 * 1024
SUBLANES = 8
LANES = 128

PROJ_ROWS = 512
FF_CHUNK = 512


def _params(*sem):
    return pltpu.CompilerParams(dimension_semantics=sem, vmem_limit_bytes=VMEM_LIMIT_BYTES)


def _const_spec(shape):
    nd = len(shape)
    return pl.BlockSpec(shape, lambda *_: (0,) * nd, pipeline_mode=pl.Buffered(1))


def _iota(shape, dim):
    return lax.broadcasted_iota(jnp.int32, shape, dim)


def _tri_lower(n):
    return (_iota((n, n), 1) <= _iota((n, n), 0)).astype(F32)


def _tri_upper(n):
    return (_iota((n, n), 0) <= _iota((n, n), 1)).astype(F32)


def _eye(n):
    return (_iota((n, n), 0) == _iota((n, n), 1)).astype(F32)


def _dot(a, b, precision=None):
    return jnp.dot(a, b, preferred_element_type=F32, precision=precision)


def _dot_nt(a, b, precision=None):
    return lax.dot_general(a, b, (((1,), (1,)), ((), ())), preferred_element_type=F32, precision=precision)


def _sigmoid(x):
    return 1.0 / (1.0 + jnp.exp(-x))


def _softplus(x):
    return jnp.maximum(x, 0.0) + jnp.log1p(jnp.exp(-jnp.abs(x)))


def _silu(x):
    return x * _sigmoid(x)


def _rmsnorm(x, g):
    y = x * lax.rsqrt(jnp.mean(x * x, axis=-1, keepdims=True) + NORM_EPS)
    return y * g


def _shifted_rows(x, tail, k):
    rolled = pltpu.roll(x, k, axis=0)
    tail_rolled = pltpu.roll(tail, k, axis=0)
    first = jnp.where(_iota((SUBLANES, 1), 0) < k, tail_rolled, rolled[:SUBLANES])
    return jnp.concatenate([first, rolled[SUBLANES:]], axis=0)


def _causal_conv_silu(x, tail, w, b):
    out = b + x * w[CONV_TAPS - 1:CONV_TAPS, :]
    for k in range(1, CONV_TAPS):
        out = out + _shifted_rows(x, tail, k) * w[CONV_TAPS - 1 - k:CONV_TAPS - k, :]
    return _silu(out)


def _rope(x, cos, sin_lo, sin_hi, reps):
    w = x.shape[1]
    c = jnp.concatenate([cos] * reps, axis=1)
    s_lo = jnp.concatenate([sin_lo] * reps, axis=1)
    s_hi = jnp.concatenate([sin_hi] * reps, axis=1)
    half = ROPE_DIM // 2
    nxt = pltpu.roll(x, w - half, axis=1)
    prv = pltpu.roll(x, half, axis=1)
    return x * c + nxt * s_lo + prv * s_hi


def _inproj_even_kernel(x_ref, g_ref, cos_ref, slo_ref, shi_ref, wq_ref, wk_ref, wv_ref, wqk_ref, wvm_ref,
                        wom_ref, wgt_ref, qa_ref, ka_ref, va_ref, qkm_ref, vm_ref, om_ref, gt_ref):
    hb = _rmsnorm(x_ref[...], g_ref[...]).astype(BF16)
    cos, s_lo, s_hi = cos_ref[...], slo_ref[...], shi_ref[...]
    q = _rope(_dot(hb, wq_ref[...]), cos, s_lo, s_hi, ATT_Q_W // LANES)
    qa_ref[...] = (q * (ATT_HEAD_DIM ** -0.5)).astype(BF16)
    ka_ref[...] = _rope(_dot(hb, wk_ref[...]), cos, s_lo, s_hi, ATT_KV_W // LANES).astype(BF16)
    va_ref[...] = _dot(hb, wv_ref[...]).astype(BF16)
    qkm_ref[...] = _dot(hb, wqk_ref[...])
    vm_ref[...] = _dot(hb, wvm_ref[...]).astype(BF16)
    om_ref[...] = _dot(hb, wom_ref[...])
    gt_ref[...] = _dot_nt(wgt_ref[...], hb)


def _inproj_even(x2, g, tables, w, seq):
    n = x2.shape[0]
    tm = PROJ_ROWS
    per_seq = seq // tm
    row = lambda width: pl.BlockSpec((tm, width), lambda i: (i, 0))
    tab = pl.BlockSpec((tm, LANES), lambda i: (i % per_seq, 0))
    wq, wk, wv, wqk, wvm, wom, wgt = w
    out_shapes = (
        jax.ShapeDtypeStruct((n, ATT_Q_W), BF16), jax.ShapeDtypeStruct((n, ATT_KV_W), BF16),
        jax.ShapeDtypeStruct((n, ATT_KV_W), BF16), jax.ShapeDtypeStruct((n, 2 * ML_QK_W), F32),
        jax.ShapeDtypeStruct((n, ML_V_W), BF16), jax.ShapeDtypeStruct((n, ML_V_W), F32),
        jax.ShapeDtypeStruct((n // seq, 2 * ML_HEADS, seq), F32))
    return pl.pallas_call(
        _inproj_even_kernel,
        grid=(n // tm,),
        in_specs=[row(D_MODEL), _const_spec((1, D_MODEL)), tab, tab, tab] + [_const_spec(a.shape) for a in w],
        out_specs=(row(ATT_Q_W), row(ATT_KV_W), row(ATT_KV_W), row(2 * ML_QK_W), row(ML_V_W), row(ML_V_W),
                   pl.BlockSpec((None, 2 * ML_HEADS, tm), lambda i: (i // per_seq, 0, i % per_seq))),
        out_shape=out_shapes,
        compiler_params=_params("arbitrary"),
        name="inproj_even",
    )(x2, g, *tables, wq, wk, wv, wqk, wvm, wom, wgt)


def _attention_stages(q_ref, kc_ref, kp_ref, vc_ref, vp_ref, sink_ref, blk, nb, emit):
    w = ATT_WINDOW
    dh = ATT_HEAD_DIM
    G = ATT_GROUP
    i = _iota((G * w, 2 * w), 0) % w
    j = _iota((G * w, 2 * w), 1)
    mask = (j > i) & (j <= i + w) & ((j >= w) | (blk > 0))
    sinks = sink_ref[...]
    problems = [(b, g) for b in range(nb) for g in range(ATT_KV_HEADS)]
    scores, values, sink_cols = [], [], []
    for b, g in problems:
        q = q_ref[b]
        qs = jnp.concatenate([q[:, (g * G + a) * dh:(g * G + a + 1) * dh] for a in range(G)], axis=0)
        ks = jnp.concatenate([kp_ref[b][:, g * dh:(g + 1) * dh], kc_ref[b][:, g * dh:(g + 1) * dh]], axis=0)
        values.append(jnp.concatenate([vp_ref[b][:, g * dh:(g + 1) * dh], vc_ref[b][:, g * dh:(g + 1) * dh]], axis=0))
        scores.append(_dot_nt(qs, ks))
        sink_cols.append(jnp.concatenate(
            [jnp.broadcast_to(sinks[:, g * G + a:g * G + a + 1], (w, 1)) for a in range(G)], axis=0))
    yield
    probs, dens = [], []
    for s, sink in zip(scores, sink_cols):
        s = jnp.where(mask, s, -jnp.inf)
        m = jnp.maximum(jnp.max(s, axis=-1, keepdims=True), sink)
        p = jnp.exp(s - m)
        dens.append(jnp.sum(p, axis=-1, keepdims=True) + jnp.exp(sink - m))
        probs.append(p.astype(BF16))
    pv = [_dot(p, v) for p, v in zip(probs, values)]
    yield
    outs = [o / d for o, d in zip(pv, dens)]
    for b in range(nb):
        heads = [outs[b * ATT_KV_HEADS + g][a * w:(a + 1) * w] for g in range(ATT_KV_HEADS) for a in range(G)]
        emit(b, jnp.concatenate(heads, axis=1).astype(BF16))


def _mlstm_stages(qk_ref, tail_ref, v_ref, o_ref, gt_ref, cw_ref, cb_ref, gb_ref, nw_ref, c_s, m_s, chunk, nb, emit):
    L = ML_CHUNK
    H = ML_HEADS
    DV = ML_V_DIM
    NG = 2 * H

    cw, cb = cw_ref[...], cb_ref[...]
    qk = [_causal_conv_silu(qk_ref[b], jnp.where(chunk == 0, 0.0, tail_ref[b]), cw, cb) for b in range(nb)]

    g = jnp.concatenate([gt_ref[b] + gb_ref[...] for b in range(nb)], axis=0)
    lg = jnp.where(_iota(g.shape, 0) % NG < H, g, -_softplus(-g))
    tri_l, tri_u, eye = _tri_lower(L), _tri_upper(L), _eye(L)
    cum_row = _dot(lg, tri_u, HI)
    lg_col = _dot_nt(eye, lg, HI)
    cum_col = _dot_nt(tri_l, lg, HI)
    causal = _iota((L, L), 1) <= _iota((L, L), 0)

    chains = [(b, h) for b in range(nb) for h in range(H)]
    idx = lambda b, h: b * H + h
    q_c = [qk[b][:, h * ML_QK_DIM:(h + 1) * ML_QK_DIM] for b, h in chains]
    k_c = [qk[b][:, ML_QK_W + h * ML_QK_DIM:ML_QK_W + (h + 1) * ML_QK_DIM] * (ML_QK_DIM ** -0.5) for b, h in chains]
    ones = jnp.ones((L, DV), BF16)
    v_c = [jnp.concatenate([v_ref[b][:, h * DV:(h + 1) * DV], ones], axis=1) for b, h in chains]
    qb_c = [q.astype(BF16) for q in q_c]
    c_prev = [c_s[idx(b, h)] for b, h in chains]
    m_prev = [m_s[idx(b, h):idx(b, h) + 1, 0:1] for b, h in chains]

    s_qk = [_dot_nt(qb, k.astype(BF16)) for qb, k in zip(qb_c, k_c)]
    inter = [_dot(qb, c.astype(BF16)) for qb, c in zip(qb_c, c_prev)]

    yield
    gi_c = [b * NG + h for b, h in chains]
    b_col = [cum_col[:, gi + H:gi + H + 1] for gi in gi_c]
    b_last = [bc[L - 1:L, :] for bc in b_col]
    dlog = [jnp.where(causal, bc - cum_row[gi + H:gi + H + 1, :] + lg[gi:gi + 1, :], -jnp.inf)
            for gi, bc in zip(gi_c, b_col)]
    row_max = [jnp.max(d, axis=1, keepdims=True) for d in dlog]
    a_loc = [bl - bc + lg_col[:, gi:gi + 1] for gi, bc, bl in zip(gi_c, b_col, b_last)]
    m_loc = [jnp.max(a, axis=0, keepdims=True) for a in a_loc]
    inter_log = [bc + mp for bc, mp in zip(b_col, m_prev)]
    m_t = [jnp.maximum(il, rm) for il, rm in zip(inter_log, row_max)]
    qk_w = [s * jnp.exp(d - mt) for s, d, mt in zip(s_qk, dlog, m_t)]
    w_inter = [jnp.exp(il - mt) for il, mt in zip(inter_log, m_t)]
    kw = [k * jnp.exp(a - ml) for k, a, ml in zip(k_c, a_loc, m_loc)]

    intra = [_dot(w.astype(BF16), v) for w, v in zip(qk_w, v_c)]
    c_loc = [lax.dot_general(w.astype(BF16), v, (((0,), (0,)), ((), ())), preferred_element_type=F32)
             for w, v in zip(kw, v_c)]

    yield
    both = [a + w * e for a, w, e in zip(intra, w_inter, inter)]
    xs = [nd[:, :DV] / jnp.maximum(jnp.abs(nd[:, DV:DV + 1]), jnp.exp(-mt)) for nd, mt in zip(both, m_t)]
    mus = [jnp.mean(x, axis=-1, keepdims=True) for x in xs]
    cen = [x - mu for x, mu in zip(xs, mus)]
    var = [jnp.mean(jnp.square(c), axis=-1, keepdims=True) for c in cen]
    outs = [c * lax.rsqrt(vr + ML_NORM_EPS) for c, vr in zip(cen, var)]
    for i, (b, h) in enumerate(chains):
        m_new = jnp.maximum(b_last[i] + m_prev[i], m_loc[i])
        fa = jnp.exp(b_last[i] + m_prev[i] - m_new)
        fb = jnp.exp(m_loc[i] - m_new)
        c_s[idx(b, h)] = fa * c_prev[i] + fb * c_loc[i]
        m_s[idx(b, h):idx(b, h) + 1, 0:1] = m_new

    nw = nw_ref[...]
    for b in range(nb):
        hm = jnp.concatenate(outs[b * H:(b + 1) * H], axis=1) * nw
        emit(b, (hm * _sigmoid(o_ref[b])).astype(BF16))


def _outproj_mlp_stages(y_parts, h, g_ref, wu_ref, wd_ref, fg_ref, final_norm, emit):
    h1 = h
    for y, w_ref in y_parts:
        h1 = h1 + _dot(y, w_ref[...])
    hb = _rmsnorm(h1, g_ref[...]).astype(BF16)
    n_chunks = D_FF // FF_CHUNK
    mlp = None
    for c in range(n_chunks):
        u = _dot(hb, wu_ref[:, c * FF_CHUNK:(c + 1) * FF_CHUNK])
        a = jnp.square(jnp.maximum(u, 0.0)).astype(BF16)
        d = _dot(a, wd_ref[c * FF_CHUNK:(c + 1) * FF_CHUNK, :])
        mlp = d if mlp is None else mlp + d
        if c + 1 in MLP_SPLIT:
            yield
    h2 = h1 + mlp
    if final_norm:
        h2 = _rmsnorm(h2, fg_ref[...])
    emit(h2)


MLP_SPLIT = (5,)


def _drive(mixers, mlp):
    for m in mixers:
        next(m)
    next(mlp)
    for m in mixers:
        next(m)
    for m in mixers:
        for _ in m:
            pass
    for _ in mlp:
        pass


def _layer0_kernel(q_ref, kc_ref, kp_ref, vc_ref, vp_ref, sink_ref,
                   qk_ref, tail_ref, vm_ref, om_ref, gt_ref, cw_ref, cb_ref, gb_ref, nw_ref,
                   h_ref, wo_ref, g_ref, wu_ref, wd_ref, fg_ref,
                   o_ref, c_s, m_s, y_s, *, nb, nc):
    step = pl.program_id(0)
    chunk = jnp.minimum(step, nc - 1)
    L = ML_CHUNK

    @pl.when(step == 0)
    def _():
        c_s[...] = jnp.zeros_like(c_s)
        m_s[...] = jnp.zeros_like(m_s)
        y_s[...] = jnp.zeros_like(y_s)

    wr = step % 2
    rd = 1 - wr

    def emit_att(b, block):
        y_s[wr, b * L:(b + 1) * L, 0:ATT_Q_W] = block

    def emit_ml(b, block):
        y_s[wr, b * L:(b + 1) * L, ATT_Q_W:ATT_Q_W + ML_V_W] = block

    def emit_out(block):
        o_ref[...] = block.reshape(nb, L, D_MODEL)

    mixers = [
        _attention_stages(q_ref, kc_ref, kp_ref, vc_ref, vp_ref, sink_ref, chunk, nb, emit_att),
        _mlstm_stages(qk_ref, tail_ref, vm_ref, om_ref, gt_ref, cw_ref, cb_ref, gb_ref, nw_ref, c_s, m_s, chunk, nb,
                      emit_ml),
    ]
    mlp = _outproj_mlp_stages([(y_s[rd], wo_ref)], h_ref[...].reshape(nb * L, D_MODEL), g_ref, wu_ref, wd_ref,
                              fg_ref, False, emit_out)
    _drive(mixers, mlp)


def _layer0(qa, ka, va, sinks, qkm, vm, om, gt, conv_w, conv_b, gate_bias, norm_w, h, wo, g, wu, wd, fg, batch, seq):
    L = ML_CHUNK
    nc = seq // L
    per8 = L // SUBLANES
    three = lambda t: t.reshape(batch, seq, t.shape[-1])
    mix = lambda c: jnp.minimum(c, nc - 1)
    cur = lambda width: pl.BlockSpec((batch, L, width), lambda c: (0, mix(c), 0))
    prev = lambda width: pl.BlockSpec((batch, L, width), lambda c: (0, jnp.maximum(mix(c) - 1, 0), 0))
    tail = pl.BlockSpec((batch, SUBLANES, 2 * ML_QK_W), lambda c: (0, jnp.maximum(mix(c) * per8 - 1, 0), 0))
    late = pl.BlockSpec((batch, L, D_MODEL), lambda c: (0, jnp.maximum(c - 1, 0), 0))
    q3, k3, v3, qk3 = three(qa), three(ka), three(va), three(qkm)
    consts = (conv_w, conv_b, gate_bias, norm_w)
    weights = (wo, g, wu, wd, fg)
    out = pl.pallas_call(
        functools.partial(_layer0_kernel, nb=batch, nc=nc),
        grid=(nc + 1,),
        in_specs=[cur(ATT_Q_W), cur(ATT_KV_W), prev(ATT_KV_W), cur(ATT_KV_W), prev(ATT_KV_W),
                  _const_spec((1, ATT_Q_HEADS)),
                  cur(2 * ML_QK_W), tail, cur(ML_V_W), cur(ML_V_W),
                  pl.BlockSpec((batch, 2 * ML_HEADS, L), lambda c: (0, 0, mix(c)))]
                 + [_const_spec(a.shape) for a in consts] + [late] + [_const_spec(a.shape) for a in weights],
        out_specs=late,
        out_shape=jax.ShapeDtypeStruct((batch, seq, D_MODEL), F32),
        scratch_shapes=[pltpu.VMEM((batch * ML_HEADS, ML_QK_DIM, 2 * ML_V_DIM), F32),
                        pltpu.VMEM((batch * ML_HEADS, LANES), F32),
                        pltpu.VMEM((2, batch * L, ATT_Q_W + ML_V_W), BF16)],
        compiler_params=_params("arbitrary"),
        name="layer0_mix_mlp",
    )(q3, k3, k3, v3, v3, sinks, qk3, qk3, three(vm), three(om), gt, *consts, three(h), *weights)
    return out.reshape(batch * seq, D_MODEL)


def _inproj_odd_kernel(x_ref, g_ref, wz_ref, wx_ref, wdt_ref, wrw_ref, z_ref, xbc_ref, dt_ref, rw_ref):
    hb = _rmsnorm(x_ref[...], g_ref[...]).astype(BF16)
    z_ref[...] = _dot(hb, wz_ref[...])
    xbc_ref[...] = _dot(hb, wx_ref[...])
    dt_ref[...] = _dot_nt(wdt_ref[...], hb)
    rw_ref[...] = _dot(hb, wrw_ref[...])


def _inproj_odd(x2, g, w, seq):
    n = x2.shape[0]
    tm = PROJ_ROWS
    per_seq = seq // tm
    row = lambda width: pl.BlockSpec((tm, width), lambda i: (i, 0))
    out_shapes = (jax.ShapeDtypeStruct((n, SSM_INNER), F32), jax.ShapeDtypeStruct((n, SSM_CONV_W), F32),
                  jax.ShapeDtypeStruct((n // seq, SSM_HEADS, seq), F32), jax.ShapeDtypeStruct((n, RW_IN), F32))
    return pl.pallas_call(
        _inproj_odd_kernel,
        grid=(n // tm,),
        in_specs=[row(D_MODEL), _const_spec((1, D_MODEL))] + [_const_spec(a.shape) for a in w],
        out_specs=(row(SSM_INNER), row(SSM_CONV_W),
                   pl.BlockSpec((None, SSM_HEADS, tm), lambda i: (i // per_seq, 0, i % per_seq)), row(RW_IN)),
        out_shape=out_shapes,
        compiler_params=_params("arbitrary"),
        name="inproj_odd",
    )(x2, g, *w)


def _ssd_stages(xbc_ref, tail_ref, z_ref, dt_ref, cw_ref, cb_ref, dtb_ref, alog_ref, d_ref, nw_ref, s_s, chunk, nb,
                emit):
    L = SSM_CHUNK
    P = SSM_HEAD_DIM
    HG = SSM_HEADS_PER_GROUP
    NH = SSM_HEADS

    cw, cb_ = cw_ref[...], cb_ref[...]
    xbc = [_causal_conv_silu(xbc_ref[b], jnp.where(chunk == 0, 0.0, tail_ref[b]), cw, cb_) for b in range(nb)]

    dt_row = _softplus(jnp.concatenate([dt_ref[b] + dtb_ref[...] for b in range(nb)], axis=0))
    neg_a = -jnp.exp(jnp.concatenate([alog_ref[...]] * nb, axis=0))
    a_row = dt_row * neg_a
    tri_l, tri_u, eye = _tri_lower(L), _tri_upper(L), _eye(L)
    acum_row = _dot(a_row, tri_u, HI)
    acum_col = _dot_nt(tri_l, a_row, HI)
    dt_col = _dot_nt(eye, dt_row, HI)
    causal = _iota((L, L), 1) <= _iota((L, L), 0)
    eye_b = eye.astype(BF16)

    groups = [(b, g) for b in range(nb) for g in range(SSM_GROUPS)]
    gidx = lambda b, g: b * SSM_GROUPS + g
    bg = [xbc[b][:, SSM_INNER + g * SSM_STATE:SSM_INNER + (g + 1) * SSM_STATE].astype(BF16) for b, g in groups]
    cg = [xbc[b][:, SSM_INNER + SSM_BC + g * SSM_STATE:SSM_INNER + SSM_BC + (g + 1) * SSM_STATE].astype(BF16)
          for b, g in groups]
    s_prev = [s_s[gidx(b, g)] for b, g in groups]

    cb = [_dot_nt(c, bq) for c, bq in zip(cg, bg)]
    bg_t = [_dot_nt(eye_b, bq).astype(BF16) for bq in bg]
    y_off = [_dot(c, sp.astype(BF16)) for c, sp in zip(cg, s_prev)]

    yield
    heads = [(b, g, j) for b, g in groups for j in range(HG)]
    scores, xcs, e_col, xws, cds = [], [], [], [], []
    for b, g, j in heads:
        h = g * HG + j
        col = b * NH + h
        a_col = acum_col[:, col:col + 1]
        a_last = a_col[L - 1:L, :]
        decay = jnp.exp(jnp.where(causal, a_col - acum_row[col:col + 1, :], -jnp.inf))
        xc = xbc[b][:, h * P:(h + 1) * P] * dt_col[:, col:col + 1]
        scores.append((cb[gidx(b, g)] * decay).astype(BF16))
        xcs.append(xc.astype(BF16))
        e_col.append(jnp.exp(a_col))
        xws.append(xc * jnp.exp(a_last - a_col))
        cds.append(jnp.broadcast_to(jnp.exp(a_last), (1, P)))

    y_diag = [_dot(sc, xc) for sc, xc in zip(scores, xcs)]
    states = [_dot(bg_t[i], jnp.concatenate(xws[i * HG:(i + 1) * HG], axis=1).astype(BF16))
              for i in range(len(groups))]

    yield
    for i, (b, g) in enumerate(groups):
        s_s[gidx(b, g)] = s_prev[i] * jnp.concatenate(cds[i * HG:(i + 1) * HG], axis=1) + states[i]

    gw = SSM_INNER // SSM_GROUPS
    for b in range(nb):
        ys = []
        for g in range(SSM_GROUPS):
            for j in range(HG):
                i = (b * SSM_GROUPS + g) * HG + j
                ys.append(y_diag[i] + y_off[gidx(b, g)][:, j * P:(j + 1) * P] * e_col[i])
        xs = xbc[b][:, :SSM_INNER]
        y = jnp.concatenate(ys, axis=1) + d_ref[...] * xs
        y = y * _silu(z_ref[b])
        normed = []
        for g in range(SSM_GROUPS):
            yg = y[:, g * gw:(g + 1) * gw]
            normed.append(yg * lax.rsqrt(jnp.mean(yg * yg, axis=-1, keepdims=True) + NORM_EPS))
        emit(b, (jnp.concatenate(normed, axis=1) * nw_ref[...]).astype(BF16))


def _layer1_kernel(xbc_ref, tail_ref, z_ref, dt_ref, cw_ref, cb_ref, dtb_ref, alog_ref, d_ref, nw_ref,
                   yrw_ref, h_ref, wa_ref, wb_ref, g_ref, wu_ref, wd_ref, fg_ref,
                   o_ref, s_s, y_s, *, nb, nc):
    step = pl.program_id(0)
    chunk = jnp.minimum(step, nc - 1)
    L = SSM_CHUNK

    @pl.when(step == 0)
    def _():
        s_s[...] = jnp.zeros_like(s_s)
        y_s[...] = jnp.zeros_like(y_s)

    wr = step % 2
    rd = 1 - wr

    def emit_ssm(b, block):
        y_s[wr, b * L:(b + 1) * L, :] = block

    def emit_out(block):
        o_ref[...] = block.reshape(nb, L, D_MODEL)

    mixers = [_ssd_stages(xbc_ref, tail_ref, z_ref, dt_ref, cw_ref, cb_ref, dtb_ref, alog_ref, d_ref, nw_ref, s_s,
                          chunk, nb, emit_ssm)]
    mlp = _outproj_mlp_stages([(y_s[rd], wa_ref), (yrw_ref[...].reshape(nb * L, RW_W), wb_ref)],
                              h_ref[...].reshape(nb * L, D_MODEL), g_ref, wu_ref, wd_ref, fg_ref, True, emit_out)
    _drive(mixers, mlp)


def _layer1(xbc, z, dt_t, conv_w, conv_b, dt_bias, a_log, d_row, norm_w, y_rw, h, wa, wb, g, wu, wd, fg, batch, seq):
    L = SSM_CHUNK
    nc = seq // L
    per8 = L // SUBLANES
    three = lambda t: t.reshape(batch, seq, t.shape[-1])
    mix = lambda c: jnp.minimum(c, nc - 1)
    cur = lambda width: pl.BlockSpec((batch, L, width), lambda c: (0, mix(c), 0))
    tail = pl.BlockSpec((batch, SUBLANES, SSM_CONV_W), lambda c: (0, jnp.maximum(mix(c) * per8 - 1, 0), 0))
    late = lambda width: pl.BlockSpec((batch, L, width), lambda c: (0, jnp.maximum(c - 1, 0), 0))
    consts = (conv_w, conv_b, dt_bias, a_log, d_row, norm_w)
    weights = (wa, wb, g, wu, wd, fg)
    x3 = three(xbc)
    out = pl.pallas_call(
        functools.partial(_layer1_kernel, nb=batch, nc=nc),
        grid=(nc + 1,),
        in_specs=[cur(SSM_CONV_W), tail, cur(SSM_INNER),
                  pl.BlockSpec((batch, SSM_HEADS, L), lambda c: (0, 0, mix(c)))]
                 + [_const_spec(a.shape) for a in consts] + [late(RW_W), late(D_MODEL)]
                 + [_const_spec(a.shape) for a in weights],
        out_specs=late(D_MODEL),
        out_shape=jax.ShapeDtypeStruct((batch, seq, D_MODEL), F32),
        scratch_shapes=[pltpu.VMEM((batch * SSM_GROUPS, SSM_STATE, SSM_HEADS_PER_GROUP * SSM_HEAD_DIM), F32),
                        pltpu.VMEM((2, batch * L, SSM_INNER), BF16)],
        compiler_params=_params("arbitrary"),
        name="layer1_mix_mlp",
    )(x3, x3, three(z), dt_t, *consts, three(y_rw), three(h), *weights)
    return out.reshape(batch * seq, D_MODEL)


def _split_bf16(x):
    hi = x.astype(BF16)
    lo = (x - hi.astype(F32)).astype(BF16)
    return hi, lo


def _dot_ones_exact(t, ones_matrix):
    hi, lo = _split_bf16(t)
    return _dot(hi, ones_matrix) + _dot(lo, ones_matrix)


def _unit_lower_inverses(mats, eye, n):
    xbs = [(eye - a).astype(BF16) for a in mats]
    mats = [a.astype(BF16) for a in mats]
    steps = 0
    terms = 2
    while terms < n:
        terms *= 2
        steps += 1
    for _ in range(steps):
        xfs = [xb.astype(F32) for xb in xbs]
        resids = [(eye - xf - _dot(a, xb)).astype(BF16) for a, xb, xf in zip(mats, xbs, xfs)]
        xbs = [(xf + _dot(xb, e)).astype(BF16) for xb, xf, e in zip(xbs, xfs, resids)]
    return xbs


def _rwkv_kernel(rw_ref, tail_ref, mu_ref, w0_ref, w2_ref, a0_ref, a2_ref, g2_ref, kk_ref, ka_ref, rk_ref,
                 lnw_ref, lnb_ref, y_ref, st_s, *, nb):
    chunk = pl.program_id(0)
    C = RW_CHUNK
    D = RW_HEAD_DIM
    R = nb * C

    @pl.when(chunk == 0)
    def _():
        st_s[...] = jnp.zeros_like(st_s)

    x = rw_ref[...].reshape(R, RW_IN)
    rolled = pltpu.roll(x, 1, axis=0)
    first_row = _iota((C, 1), 0) == 0
    prev = []
    for b in range(nb):
        before = jnp.where(chunk == 0, 0.0, tail_ref[b, SUBLANES - 1:SUBLANES, :])
        prev.append(jnp.where(first_row, before, rolled[b * C:(b + 1) * C]))
    prev = jnp.concatenate(prev, axis=0)
    xm = x + (prev - x) * mu_ref[...]
    r = xm[:, 0:RW_W]
    k = xm[:, RW_W:2 * RW_W]
    v = xm[:, 2 * RW_W:3 * RW_W]
    o = 3 * RW_W
    wd = xm[:, o:o + RW_DECAY_LORA]
    ad = xm[:, o + RW_DECAY_LORA:o + RW_DECAY_LORA + RW_AAA_LORA]
    gd = xm[:, o + RW_DECAY_LORA + RW_AAA_LORA:]

    z = w0_ref[...] + _dot(jnp.tanh(wd).astype(BF16), w2_ref[...])
    ld = -RW_EXP_MINUS_HALF * _sigmoid(z)
    a = _sigmoid(a0_ref[...] + _dot(ad.astype(BF16), a2_ref[...]))
    gate = _dot(_sigmoid(gd).astype(BF16), g2_ref[...])

    half = RW_W // 2
    same_head = (_iota((half, half), 0) // D == _iota((half, half), 1) // D).astype(BF16)
    head_sum = lambda t: jnp.concatenate(
        [_dot_ones_exact(t[:, :half], same_head), _dot_ones_exact(t[:, half:], same_head)], axis=1)

    kk = k * kk_ref[...]
    kk = kk * lax.rsqrt(jnp.maximum(head_sum(kk * kk), 1e-24))
    k2 = k * (1.0 + (a - 1.0) * ka_ref[...])
    bvec = kk * a

    tri = ((_iota((R, R), 1) <= _iota((R, R), 0)) & (_iota((R, R), 1) // C == _iota((R, R), 0) // C)).astype(BF16)
    ld_hi, ld_rest = _split_bf16(ld)
    ld_mid = ld - ld_hi.astype(F32)
    ld_lo = (ld_mid - ld_rest.astype(F32)).astype(BF16)
    cl = _dot(tri, ld_hi) + (_dot(tri, ld_rest) + _dot(tri, ld_lo))
    c_last = jnp.concatenate(
        [jnp.broadcast_to(cl[(b + 1) * C - 1:(b + 1) * C, :], (C, RW_W)) for b in range(nb)], axis=0)
    e_neg = jnp.exp(-cl)
    e_end = jnp.exp(c_last - cl)
    w_end = [jnp.exp(cl[(b + 1) * C - 1:(b + 1) * C, :]) for b in range(nb)]
    rw_ = r * jnp.exp(cl)
    kkw = kk * jnp.exp(cl - ld)
    bd = bvec * e_neg
    kd = k2 * e_neg
    bd_end = bvec * e_end
    kd_end = k2 * e_end

    g_row = _iota((2 * C, 2 * C), 0)
    g_col = _iota((2 * C, 2 * C), 1) % C
    keep = g_col < jnp.where(g_row < C, g_row, g_row - C + 1)
    eye = _eye(C)
    chains = [(b, h) for b in range(nb) for h in range(RW_HEADS)]
    blk = lambda t, b, h: t[b * C:(b + 1) * C, h * D:(h + 1) * D]
    kkw_c = [blk(kkw, b, h).astype(BF16) for b, h in chains]
    rw_c = [blk(rw_, b, h).astype(BF16) for b, h in chains]
    vb_c = [blk(v, b, h).astype(BF16) for b, h in chains]
    gms = [jnp.where(keep, _dot_nt(jnp.concatenate([kq, rq], axis=0),
                                   jnp.concatenate([blk(bd, b, h), blk(kd, b, h)], axis=0).astype(BF16)), 0.0)
           for (b, h), kq, rq in zip(chains, kkw_c, rw_c)]
    a_v = [_dot(g[:, C:].astype(BF16), vb) for g, vb in zip(gms, vb_c)]
    a_rb = [g[C:, :C].astype(BF16) for g in gms]
    invs = _unit_lower_inverses([g[:C, :C] for g in gms], eye, C)
    tws = [_dot(inv, jnp.concatenate([av[:C].astype(BF16), kq], axis=1))
           for inv, av, kq in zip(invs, a_v, kkw_c)]
    sts = [st_s[b * RW_HEADS + h] for b, h in chains]
    xss = [_dot_nt(jnp.concatenate([tw[:, D:].astype(BF16), rq], axis=0), st.astype(BF16))
           for tw, rq, st in zip(tws, rw_c, sts)]
    ubs = [(-tw[:, :D] - xs[:C]).astype(BF16) for tw, xs in zip(tws, xss)]
    ys = [xs[C:] + _dot(m, ub) + av[C:] for xs, m, ub, av in zip(xss, a_rb, ubs, a_v)]
    for (b, h), st, ub, vb in zip(chains, sts, ubs, vb_c):
        ends = jnp.concatenate([blk(bd_end, b, h), blk(kd_end, b, h)], axis=0).astype(BF16)
        upd = lax.dot_general(jnp.concatenate([ub, vb], axis=0), ends, (((0,), (0,)), ((), ())),
                              preferred_element_type=F32)
        st_s[b * RW_HEADS + h] = st * w_end[b][:, h * D:(h + 1) * D] + upd

    y = jnp.concatenate([jnp.concatenate(ys[b * RW_HEADS:(b + 1) * RW_HEADS], axis=1) for b in range(nb)], axis=0)
    mu = head_sum(y) * (1.0 / D)
    var = head_sum(jnp.square(y - mu)) * (1.0 / D)
    yn = (y - mu) * lax.rsqrt(var + RW_LN_EPS) * lnw_ref[...] + lnb_ref[...]
    bonus = head_sum(r * k2 * rk_ref[...]) * v
    y_ref[...] = ((yn + bonus) * gate).astype(BF16).reshape(nb, C, RW_W)


def _rwkv(rw, consts, batch, seq):
    C = RW_CHUNK
    nc = seq // C
    per8 = C // SUBLANES
    rw3 = rw.reshape(batch, seq, RW_IN)
    y = pl.pallas_call(
        functools.partial(_rwkv_kernel, nb=batch),
        grid=(nc,),
        in_specs=[pl.BlockSpec((batch, C, RW_IN), lambda c: (0, c, 0)),
                  pl.BlockSpec((batch, SUBLANES, RW_IN), lambda c: (0, jnp.maximum(c * per8 - 1, 0), 0))]
                 + [_const_spec(a.shape) for a in consts],
        out_specs=pl.BlockSpec((batch, C, RW_W), lambda c: (0, c, 0)),
        out_shape=jax.ShapeDtypeStruct((batch, seq, RW_W), BF16),
        scratch_shapes=[pltpu.VMEM((batch * RW_HEADS, RW_HEAD_DIM, RW_HEAD_DIM), F32)],
        compiler_params=_params("arbitrary"),
        name="rwkv7",
    )(rw3, rw3, *consts)
    return y.reshape(batch * seq, RW_W)


@functools.lru_cache(maxsize=None)
def _rope_tables(seq):
    half = ROPE_DIM // 2
    inv_freq = ROPE_THETA ** (-np.arange(half, dtype=np.float64) * 2.0 / ROPE_DIM)
    ang = np.arange(seq, dtype=np.float64)[:, None] * inv_freq[None, :]
    cos, sin = np.cos(ang), np.sin(ang)
    ones = np.ones((seq, ATT_HEAD_DIM - ROPE_DIM))
    zeros = np.zeros((seq, ATT_HEAD_DIM - ROPE_DIM))
    zh = np.zeros((seq, half))
    reps = LANES // ATT_HEAD_DIM
    cos_t = np.concatenate([cos, cos, ones] * reps, axis=1).astype(np.float32)
    sin_lo = np.concatenate([-sin, zh, zeros] * reps, axis=1).astype(np.float32)
    sin_hi = np.concatenate([zh, sin, zeros] * reps, axis=1).astype(np.float32)
    return cos_t, sin_lo, sin_hi


def _row(v):
    return v.reshape(1, -1).astype(F32)


def _col(v):
    return v.reshape(-1, 1).astype(F32)


def kernel(x, mix_norm, mlp_norm, w_up, w_down, e_w_in, e_w_out, att_sinks, ml_conv_w, ml_conv_b, ml_i_bias,
           ml_f_bias, ml_norm, o_w_in, o_w_out, ssm_conv_w, ssm_conv_b, ssm_dt_bias, ssm_a_log, ssm_d, ssm_norm,
           rw_mu, rw_w0, rw_w2, rw_a0, rw_a2, rw_g2, rw_k_k, rw_k_a, rw_r_k, rw_ln_w, rw_ln_b, final_norm):
    batch, seq, _ = x.shape
    h = x.reshape(batch * seq, D_MODEL)
    fg = _row(final_norm)

    w = e_w_in[0].astype(BF16)
    o = 0
    parts = []
    for width in (ATT_Q_W, ATT_KV_W, ATT_KV_W, 2 * ML_QK_W, ML_V_W, ML_V_W):
        parts.append(w[:, o:o + width])
        o += width
    parts.append(w[:, o:o + 2 * ML_HEADS].T)
    qa, ka, va, qkm, vm, om, gt = _inproj_even(h, _row(mix_norm[0]), _rope_tables(seq), parts, seq)
    gate_bias = _col(jnp.concatenate([ml_i_bias[0], ml_f_bias[0]]))
    h = _layer0(qa, ka, va, _row(att_sinks[0]), qkm, vm, om, gt, ml_conv_w[0].astype(F32), _row(ml_conv_b[0]),
                gate_bias, _row(ml_norm[0]), h, e_w_out[0].astype(BF16), _row(mlp_norm[0]), w_up[0].astype(BF16),
                w_down[0].astype(BF16), fg, batch, seq)

    w = o_w_in[0].astype(BF16)
    o_x = SSM_INNER
    o_dt = o_x + SSM_CONV_W
    o_rw = o_dt + SSM_HEADS
    parts = [w[:, :o_x], w[:, o_x:o_dt], w[:, o_dt:o_rw].T, w[:, o_rw:]]
    z, xbc, dt_t, rw = _inproj_odd(h, _row(mix_norm[1]), parts, seq)
    rw_consts = (_row(rw_mu[0]), _row(rw_w0[0]), rw_w2[0].astype(BF16), _row(rw_a0[0]), rw_a2[0].astype(BF16),
                 rw_g2[0].astype(BF16), _row(rw_k_k[0]), _row(rw_k_a[0]), _row(rw_r_k[0]), _row(rw_ln_w[0]),
                 _row(rw_ln_b[0]))
    y_rw = _rwkv(rw, rw_consts, batch, seq)
    wo = o_w_out[0].astype(BF16)
    h = _layer1(xbc, z, dt_t, ssm_conv_w[0].astype(F32), _row(ssm_conv_b[0]), _col(ssm_dt_bias[0]),
                _col(ssm_a_log[0]), _row(jnp.repeat(ssm_d[0], SSM_HEAD_DIM)), _row(ssm_norm[0]), y_rw, h,
                wo[:SSM_INNER], wo[SSM_INNER:], _row(mlp_norm[1]), w_up[1].astype(BF16), w_down[1].astype(BF16), fg,
                batch, seq)
    return h.reshape(batch, seq, D_MODEL)
```

```python
import functools

import numpy as np

import jax
import jax.numpy as jnp
from jax import lax
from jax.experimental import pallas as pl
from jax.experimental.pallas import tpu as pltpu

F32 = jnp.float32
BF16 = jnp.bfloat16

D_MODEL = 1024
D_FF = 4 * D_MODEL
NORM_EPS = 1e-5
CONV_TAPS = 4

ATT_HEAD_DIM = 64
ATT_Q_HEADS = 8
ATT_KV_HEADS = 2
ATT_GROUP = ATT_Q_HEADS // ATT_KV_HEADS
ATT_WINDOW = 128
ROPE_THETA = 500000.0
ROPE_DIM = 16
ATT_Q_W = ATT_Q_HEADS * ATT_HEAD_DIM
ATT_KV_W = ATT_KV_HEADS * ATT_HEAD_DIM

ML_HEADS = 4
ML_V_DIM = 128
ML_QK_DIM = 64
ML_CHUNK = 128
ML_NORM_EPS = 1e-6
ML_QK_W = ML_HEADS * ML_QK_DIM
ML_V_W = ML_HEADS * ML_V_DIM

SSM_HEAD_DIM = 64
SSM_HEADS = 8
SSM_GROUPS = 2
SSM_HEADS_PER_GROUP = SSM_HEADS // SSM_GROUPS
SSM_STATE = 128
SSM_CHUNK = 128
SSM_INNER = SSM_HEADS * SSM_HEAD_DIM
SSM_BC = SSM_GROUPS * SSM_STATE
SSM_CONV_W = SSM_INNER + 2 * SSM_BC

RW_HEAD_DIM = 64
RW_HEADS = 8
RW_W = RW_HEADS * RW_HEAD_DIM
RW_DECAY_LORA = 64
RW_AAA_LORA = 64
RW_GATE_LORA = 128
RW_IN = 3 * RW_W + RW_DECAY_LORA + RW_AAA_LORA + RW_GATE_LORA
RW_LN_EPS = 64e-5
RW_CHUNK = 64
RW_EXP_MINUS_HALF = float(np.exp(-0.5))

V7X_VMEM_BYTES = 64 * 1024 * 1024
VMEM_LIMIT_BYTES = V7X_VMEM_BYTES - 6 * 1024 * 1024
SUBLANES = 8
LANES = 128

PROJ_ROWS = 1024
FF_CHUNK = 512


def _params(*sem):
    return pltpu.CompilerParams(dimension_semantics=sem, vmem_limit_bytes=VMEM_LIMIT_BYTES)


def _const_spec(shape):
    nd = len(shape)
    return pl.BlockSpec(shape, lambda *_: (0,) * nd, pipeline_mode=pl.Buffered(1))


def _iota(shape, dim):
    return lax.broadcasted_iota(jnp.int32, shape, dim)


def _tri_lower(n):
    return (_iota((n, n), 1) <= _iota((n, n), 0)).astype(F32)


def _tri_upper(n):
    return (_iota((n, n), 0) <= _iota((n, n), 1)).astype(F32)


def _eye(n):
    return (_iota((n, n), 0) == _iota((n, n), 1)).astype(F32)


def _dot(a, b, precision=None):
    return jnp.dot(a, b, preferred_element_type=F32, precision=precision)


def _dot_nt(a, b, precision=None):
    return lax.dot_general(a, b, (((1,), (1,)), ((), ())), preferred_element_type=F32, precision=precision)


def _split3_bf16(x):
    hi = x.astype(BF16)
    r1 = x - hi.astype(F32)
    mid = r1.astype(BF16)
    lo = (r1 - mid.astype(F32)).astype(BF16)
    return hi, mid, lo


def _sum3(terms):
    hi, mid, lo = terms
    return hi + (mid + lo)


def _sigmoid(x):
    return 1.0 / (1.0 + jnp.exp(-x))


def _softplus(x):
    return jnp.maximum(x, 0.0) + jnp.log1p(jnp.exp(-jnp.abs(x)))


def _silu(x):
    return x * _sigmoid(x)


def _rmsnorm(x, g):
    y = x * lax.rsqrt(jnp.mean(x * x, axis=-1, keepdims=True) + NORM_EPS)
    return y * g


def _shifted_rows(x, tail, k):
    rolled = pltpu.roll(x, k, axis=0)
    tail_rolled = pltpu.roll(tail, k, axis=0)
    first = jnp.where(_iota((SUBLANES, 1), 0) < k, tail_rolled, rolled[:SUBLANES])
    return jnp.concatenate([first, rolled[SUBLANES:]], axis=0)


def _causal_conv_silu(x, tail, w, b):
    out = b + x * w[CONV_TAPS - 1:CONV_TAPS, :]
    for k in range(1, CONV_TAPS):
        out = out + _shifted_rows(x, tail, k) * w[CONV_TAPS - 1 - k:CONV_TAPS - k, :]
    return _silu(out)


def _rope(x, cos, sin_lo, sin_hi, reps):
    w = x.shape[1]
    c = jnp.concatenate([cos] * reps, axis=1)
    s_lo = jnp.concatenate([sin_lo] * reps, axis=1)
    s_hi = jnp.concatenate([sin_hi] * reps, axis=1)
    half = ROPE_DIM // 2
    nxt = pltpu.roll(x, w - half, axis=1)
    prv = pltpu.roll(x, half, axis=1)
    return x * c + nxt * s_lo + prv * s_hi


def _inproj_even_kernel(x_ref, g_ref, cos_ref, slo_ref, shi_ref, wq_ref, wk_ref, wv_ref, wqk_ref, wvm_ref,
                        wom_ref, wgt_ref, qa_ref, ka_ref, va_ref, qkm_ref, vm_ref, om_ref, gt_ref):
    hb = _rmsnorm(x_ref[...], g_ref[...]).astype(BF16)
    cos, s_lo, s_hi = cos_ref[...], slo_ref[...], shi_ref[...]
    q = _rope(_dot(hb, wq_ref[...]), cos, s_lo, s_hi, ATT_Q_W // LANES)
    qa_ref[...] = (q * (ATT_HEAD_DIM ** -0.5)).astype(BF16)
    ka_ref[...] = _rope(_dot(hb, wk_ref[...]), cos, s_lo, s_hi, ATT_KV_W // LANES).astype(BF16)
    va_ref[...] = _dot(hb, wv_ref[...]).astype(BF16)
    qkm_ref[...] = _dot(hb, wqk_ref[...])
    vm_ref[...] = _dot(hb, wvm_ref[...]).astype(BF16)
    om_ref[...] = _dot(hb, wom_ref[...])
    gt_ref[...] = _dot_nt(wgt_ref[...], hb)


def _inproj_even(x2, g, tables, w, seq):
    n = x2.shape[0]
    tm = min(PROJ_ROWS, seq)
    per_seq = seq // tm
    row = lambda width: pl.BlockSpec((tm, width), lambda i: (i, 0))
    tab = pl.BlockSpec((tm, LANES), lambda i: (i % per_seq, 0))
    wq, wk, wv, wqk, wvm, wom, wgt = w
    out_shapes = (
        jax.ShapeDtypeStruct((n, ATT_Q_W), BF16), jax.ShapeDtypeStruct((n, ATT_KV_W), BF16),
        jax.ShapeDtypeStruct((n, ATT_KV_W), BF16), jax.ShapeDtypeStruct((n, 2 * ML_QK_W), F32),
        jax.ShapeDtypeStruct((n, ML_V_W), BF16), jax.ShapeDtypeStruct((n, ML_V_W), F32),
        jax.ShapeDtypeStruct((n // seq, 2 * ML_HEADS, seq), F32))
    return pl.pallas_call(
        _inproj_even_kernel,
        grid=(n // tm,),
        in_specs=[row(D_MODEL), _const_spec((1, D_MODEL)), tab, tab, tab] + [_const_spec(a.shape) for a in w],
        out_specs=(row(ATT_Q_W), row(ATT_KV_W), row(ATT_KV_W), row(2 * ML_QK_W), row(ML_V_W), row(ML_V_W),
                   pl.BlockSpec((None, 2 * ML_HEADS, tm), lambda i: (i // per_seq, 0, i % per_seq))),
        out_shape=out_shapes,
        compiler_params=_params("arbitrary"),
        name="inproj_even",
    )(x2, g, *tables, wq, wk, wv, wqk, wvm, wom, wgt)


def _attention_stages(q_ref, kc_ref, kp_ref, vc_ref, vp_ref, sink_ref, blk, nb, emit):
    w = ATT_WINDOW
    dh = ATT_HEAD_DIM
    G = ATT_GROUP
    i = _iota((G * w, 2 * w), 0) % w
    j = _iota((G * w, 2 * w), 1)
    mask = (j > i) & (j <= i + w) & ((j >= w) | (blk > 0))
    sinks = sink_ref[...]
    problems = [(b, g) for b in range(nb) for g in range(ATT_KV_HEADS)]
    scores, values, sink_cols = [], [], []
    for b, g in problems:
        q = q_ref[b]
        qs = jnp.concatenate([q[:, (g * G + a) * dh:(g * G + a + 1) * dh] for a in range(G)], axis=0)
        ks = jnp.concatenate([kp_ref[b][:, g * dh:(g + 1) * dh], kc_ref[b][:, g * dh:(g + 1) * dh]], axis=0)
        values.append(jnp.concatenate([vp_ref[b][:, g * dh:(g + 1) * dh], vc_ref[b][:, g * dh:(g + 1) * dh]], axis=0))
        scores.append(_dot_nt(qs, ks))
        sink_cols.append(jnp.concatenate(
            [jnp.broadcast_to(sinks[:, g * G + a:g * G + a + 1], (w, 1)) for a in range(G)], axis=0))
    yield
    probs, dens = [], []
    for s, sink in zip(scores, sink_cols):
        s = jnp.where(mask, s, -jnp.inf)
        m = jnp.maximum(jnp.max(s, axis=-1, keepdims=True), sink)
        p = jnp.exp(s - m)
        dens.append(jnp.sum(p, axis=-1, keepdims=True) + jnp.exp(sink - m))
        probs.append(p.astype(BF16))
    pv = [_dot(p, v) for p, v in zip(probs, values)]
    yield
    outs = [o / d for o, d in zip(pv, dens)]
    for b in range(nb):
        heads = [outs[b * ATT_KV_HEADS + g][a * w:(a + 1) * w] for g in range(ATT_KV_HEADS) for a in range(G)]
        emit(b, jnp.concatenate(heads, axis=1).astype(BF16))


def _mlstm_stages(qk_ref, tail_ref, v_ref, o_ref, gt_ref, cw_ref, cb_ref, gb_ref, nw_ref, c_s, m_s, chunk, nb, emit):
    L = ML_CHUNK
    H = ML_HEADS
    DV = ML_V_DIM
    NG = 2 * H

    cw, cb = cw_ref[...], cb_ref[...]
    qk = [_causal_conv_silu(qk_ref[b], jnp.where(chunk == 0, 0.0, tail_ref[b]), cw, cb) for b in range(nb)]

    g = jnp.concatenate([gt_ref[b] + gb_ref[...] for b in range(nb)], axis=0)
    lg = jnp.where(_iota(g.shape, 0) % NG < H, g, -_softplus(-g))
    tri_l, tri_u, eye = _tri_lower(L).astype(BF16), _tri_upper(L).astype(BF16), _eye(L).astype(BF16)
    lg3 = _split3_bf16(lg)
    cum_row = _sum3([_dot(p, tri_u) for p in lg3])
    lg_col = _sum3([_dot_nt(eye, p) for p in lg3])
    cum_col = _sum3([_dot_nt(tri_l, p) for p in lg3])
    causal = _iota((L, L), 1) <= _iota((L, L), 0)

    chains = [(b, h) for b in range(nb) for h in range(H)]
    idx = lambda b, h: b * H + h
    q_c = [qk[b][:, h * ML_QK_DIM:(h + 1) * ML_QK_DIM] for b, h in chains]
    k_c = [qk[b][:, ML_QK_W + h * ML_QK_DIM:ML_QK_W + (h + 1) * ML_QK_DIM] * (ML_QK_DIM ** -0.5) for b, h in chains]
    ones = jnp.ones((L, DV), BF16)
    v_c = [jnp.concatenate([v_ref[b][:, h * DV:(h + 1) * DV], ones], axis=1) for b, h in chains]
    qb_c = [q.astype(BF16) for q in q_c]
    c_prev = [c_s[idx(b, h)] for b, h in chains]
    m_prev = [m_s[idx(b, h):idx(b, h) + 1, 0:1] for b, h in chains]

    s_qk = [_dot_nt(qb, k.astype(BF16)) for qb, k in zip(qb_c, k_c)]
    inter = [_dot(qb, c.astype(BF16)) for qb, c in zip(qb_c, c_prev)]

    yield
    gi_c = [b * NG + h for b, h in chains]
    b_col = [cum_col[:, gi + H:gi + H + 1] for gi in gi_c]
    b_last = [bc[L - 1:L, :] for bc in b_col]
    dlog = [jnp.where(causal, bc - cum_row[gi + H:gi + H + 1, :] + lg[gi:gi + 1, :], -jnp.inf)
            for gi, bc in zip(gi_c, b_col)]
    row_max = [jnp.max(d, axis=1, keepdims=True) for d in dlog]
    a_loc = [bl - bc + lg_col[:, gi:gi + 1] for gi, bc, bl in zip(gi_c, b_col, b_last)]
    m_loc = [jnp.max(a, axis=0, keepdims=True) for a in a_loc]
    inter_log = [bc + mp for bc, mp in zip(b_col, m_prev)]
    m_t = [jnp.maximum(il, rm) for il, rm in zip(inter_log, row_max)]
    qk_w = [s * jnp.exp(d - mt) for s, d, mt in zip(s_qk, dlog, m_t)]
    w_inter = [jnp.exp(il - mt) for il, mt in zip(inter_log, m_t)]
    kw = [k * jnp.exp(a - ml) for k, a, ml in zip(k_c, a_loc, m_loc)]

    intra = [_dot(w.astype(BF16), v) for w, v in zip(qk_w, v_c)]
    c_loc = [lax.dot_general(w.astype(BF16), v, (((0,), (0,)), ((), ())), preferred_element_type=F32)
             for w, v in zip(kw, v_c)]

    yield
    both = [a + w * e for a, w, e in zip(intra, w_inter, inter)]
    xs = [nd[:, :DV] / jnp.maximum(jnp.abs(nd[:, DV:DV + 1]), jnp.exp(-mt)) for nd, mt in zip(both, m_t)]
    mus = [jnp.mean(x, axis=-1, keepdims=True) for x in xs]
    cen = [x - mu for x, mu in zip(xs, mus)]
    var = [jnp.mean(jnp.square(c), axis=-1, keepdims=True) for c in cen]
    outs = [c * lax.rsqrt(vr + ML_NORM_EPS) for c, vr in zip(cen, var)]
    for i, (b, h) in enumerate(chains):
        m_new = jnp.maximum(b_last[i] + m_prev[i], m_loc[i])
        fa = jnp.exp(b_last[i] + m_prev[i] - m_new)
        fb = jnp.exp(m_loc[i] - m_new)
        c_s[idx(b, h)] = fa * c_prev[i] + fb * c_loc[i]
        m_s[idx(b, h):idx(b, h) + 1, 0:1] = m_new

    nw = nw_ref[...]
    for b in range(nb):
        hm = jnp.concatenate(outs[b * H:(b + 1) * H], axis=1) * nw
        emit(b, (hm * _sigmoid(o_ref[b])).astype(BF16))


def _outproj_mlp_stages(y_parts, h, g_ref, wu_ref, wd_ref, fg_ref, final_norm, emit):
    h1 = h
    for y, w_ref in y_parts:
        h1 = h1 + _dot(y, w_ref[...])
    hb = _rmsnorm(h1, g_ref[...]).astype(BF16)
    n_chunks = D_FF // FF_CHUNK
    mlp = None
    for c in range(n_chunks):
        u = _dot(hb, wu_ref[:, c * FF_CHUNK:(c + 1) * FF_CHUNK])
        a = jnp.square(jnp.maximum(u, 0.0)).astype(BF16)
        d = _dot(a, wd_ref[c * FF_CHUNK:(c + 1) * FF_CHUNK, :])
        mlp = d if mlp is None else mlp + d
        if c + 1 in MLP_SPLIT:
            yield
    h2 = h1 + mlp
    if final_norm:
        h2 = _rmsnorm(h2, fg_ref[...])
    emit(h2)


MLP_SPLIT = (5,)


def _drive(mixers, mlp):
    for m in mixers:
        next(m)
    next(mlp)
    for m in mixers:
        next(m)
    for m in mixers:
        for _ in m:
            pass
    for _ in mlp:
        pass


def _layer0_kernel(q_ref, kc_ref, kp_ref, vc_ref, vp_ref, sink_ref,
                   qk_ref, tail_ref, vm_ref, om_ref, gt_ref, cw_ref, cb_ref, gb_ref, nw_ref,
                   h_ref, wo_ref, g_ref, wu_ref, wd_ref, fg_ref,
                   o_ref, c_s, m_s, y_s, *, nb, nc):
    step = pl.program_id(0)
    chunk = jnp.minimum(step, nc - 1)
    L = ML_CHUNK

    @pl.when(step == 0)
    def _():
        c_s[...] = jnp.zeros_like(c_s)
        m_s[...] = jnp.zeros_like(m_s)
        y_s[...] = jnp.zeros_like(y_s)

    wr = step % 2
    rd = 1 - wr

    def emit_att(b, block):
        y_s[wr, b * L:(b + 1) * L, 0:ATT_Q_W] = block

    def emit_ml(b, block):
        y_s[wr, b * L:(b + 1) * L, ATT_Q_W:ATT_Q_W + ML_V_W] = block

    def emit_out(block):
        o_ref[...] = block.reshape(nb, L, D_MODEL)

    mixers = [
        _attention_stages(q_ref, kc_ref, kp_ref, vc_ref, vp_ref, sink_ref, chunk, nb, emit_att),
        _mlstm_stages(qk_ref, tail_ref, vm_ref, om_ref, gt_ref, cw_ref, cb_ref, gb_ref, nw_ref, c_s, m_s, chunk, nb,
                      emit_ml),
    ]
    mlp = _outproj_mlp_stages([(y_s[rd], wo_ref)], h_ref[...].reshape(nb * L, D_MODEL), g_ref, wu_ref, wd_ref,
                              fg_ref, False, emit_out)
    _drive(mixers, mlp)


def _layer0(qa, ka, va, sinks, qkm, vm, om, gt, conv_w, conv_b, gate_bias, norm_w, h, wo, g, wu, wd, fg, batch, seq):
    L = ML_CHUNK
    nc = seq // L
    per8 = L // SUBLANES
    three = lambda t: t.reshape(batch, seq, t.shape[-1])
    mix = lambda c: jnp.minimum(c, nc - 1)
    cur = lambda width: pl.BlockSpec((batch, L, width), lambda c: (0, mix(c), 0))
    prev = lambda width: pl.BlockSpec((batch, L, width), lambda c: (0, jnp.maximum(mix(c) - 1, 0), 0))
    tail = pl.BlockSpec((batch, SUBLANES, 2 * ML_QK_W), lambda c: (0, jnp.maximum(mix(c) * per8 - 1, 0), 0))
    late = pl.BlockSpec((batch, L, D_MODEL), lambda c: (0, jnp.maximum(c - 1, 0), 0))
    q3, k3, v3, qk3 = three(qa), three(ka), three(va), three(qkm)
    consts = (conv_w, conv_b, gate_bias, norm_w)
    weights = (wo, g, wu, wd, fg)
    out = pl.pallas_call(
        functools.partial(_layer0_kernel, nb=batch, nc=nc),
        grid=(nc + 1,),
        in_specs=[cur(ATT_Q_W), cur(ATT_KV_W), prev(ATT_KV_W), cur(ATT_KV_W), prev(ATT_KV_W),
                  _const_spec((1, ATT_Q_HEADS)),
                  cur(2 * ML_QK_W), tail, cur(ML_V_W), cur(ML_V_W),
                  pl.BlockSpec((batch, 2 * ML_HEADS, L), lambda c: (0, 0, mix(c)))]
                 + [_const_spec(a.shape) for a in consts] + [late] + [_const_spec(a.shape) for a in weights],
        out_specs=late,
        out_shape=jax.ShapeDtypeStruct((batch, seq, D_MODEL), F32),
        scratch_shapes=[pltpu.VMEM((batch * ML_HEADS, ML_QK_DIM, 2 * ML_V_DIM), F32),
                        pltpu.VMEM((batch * ML_HEADS, LANES), F32),
                        pltpu.VMEM((2, batch * L, ATT_Q_W + ML_V_W), BF16)],
        compiler_params=_params("arbitrary"),
        name="layer0_mix_mlp",
    )(q3, k3, k3, v3, v3, sinks, qk3, qk3, three(vm), three(om), gt, *consts, three(h), *weights)
    return out.reshape(batch * seq, D_MODEL)


def _inproj_odd_kernel(x_ref, g_ref, wz_ref, wx_ref, wdt_ref, wrw_ref, z_ref, xbc_ref, dt_ref, rw_ref):
    hb = _rmsnorm(x_ref[...], g_ref[...]).astype(BF16)
    z_ref[...] = _dot(hb, wz_ref[...])
    xbc_ref[...] = _dot(hb, wx_ref[...])
    dt_ref[...] = _dot_nt(wdt_ref[...], hb)
    rw_ref[...] = _dot(hb, wrw_ref[...])


def _inproj_odd(x2, g, w, seq):
    n = x2.shape[0]
    tm = min(PROJ_ROWS, seq)
    per_seq = seq // tm
    row = lambda width: pl.BlockSpec((tm, width), lambda i: (i, 0))
    out_shapes = (jax.ShapeDtypeStruct((n, SSM_INNER), F32), jax.ShapeDtypeStruct((n, SSM_CONV_W), F32),
                  jax.ShapeDtypeStruct((n // seq, SSM_HEADS, seq), F32), jax.ShapeDtypeStruct((n, RW_IN), F32))
    return pl.pallas_call(
        _inproj_odd_kernel,
        grid=(n // tm,),
        in_specs=[row(D_MODEL), _const_spec((1, D_MODEL))] + [_const_spec(a.shape) for a in w],
        out_specs=(row(SSM_INNER), row(SSM_CONV_W),
                   pl.BlockSpec((None, SSM_HEADS, tm), lambda i: (i // per_seq, 0, i % per_seq)), row(RW_IN)),
        out_shape=out_shapes,
        compiler_params=_params("arbitrary"),
        name="inproj_odd",
    )(x2, g, *w)


def _ssd_stages(xbc_ref, tail_ref, z_ref, dt_ref, cw_ref, cb_ref, dtb_ref, alog_ref, d_ref, nw_ref, s_s, chunk, nb,
                emit):
    L = SSM_CHUNK
    P = SSM_HEAD_DIM
    HG = SSM_HEADS_PER_GROUP
    NH = SSM_HEADS

    cw, cb_ = cw_ref[...], cb_ref[...]
    xbc = [_causal_conv_silu(xbc_ref[b], jnp.where(chunk == 0, 0.0, tail_ref[b]), cw, cb_) for b in range(nb)]

    dt_row = _softplus(jnp.concatenate([dt_ref[b] + dtb_ref[...] for b in range(nb)], axis=0))
    neg_a = -jnp.exp(jnp.concatenate([alog_ref[...]] * nb, axis=0))
    a_row = dt_row * neg_a
    tri_l, tri_u, eye_b = _tri_lower(L).astype(BF16), _tri_upper(L).astype(BF16), _eye(L).astype(BF16)
    a3 = _split3_bf16(a_row)
    acum_row = _sum3([_dot(p, tri_u) for p in a3])
    acum_col = _sum3([_dot_nt(tri_l, p) for p in a3])
    dt_col = _sum3([_dot_nt(eye_b, p) for p in _split3_bf16(dt_row)])
    causal = _iota((L, L), 1) <= _iota((L, L), 0)

    groups = [(b, g) for b in range(nb) for g in range(SSM_GROUPS)]
    gidx = lambda b, g: b * SSM_GROUPS + g
    bg = [xbc[b][:, SSM_INNER + g * SSM_STATE:SSM_INNER + (g + 1) * SSM_STATE].astype(BF16) for b, g in groups]
    cg = [xbc[b][:, SSM_INNER + SSM_BC + g * SSM_STATE:SSM_INNER + SSM_BC + (g + 1) * SSM_STATE].astype(BF16)
          for b, g in groups]
    s_prev = [s_s[gidx(b, g)] for b, g in groups]

    cb = [_dot_nt(c, bq) for c, bq in zip(cg, bg)]
    bg_t = [_dot_nt(eye_b, bq).astype(BF16) for bq in bg]
    y_off = [_dot(c, sp.astype(BF16)) for c, sp in zip(cg, s_prev)]

    yield
    heads = [(b, g, j) for b, g in groups for j in range(HG)]
    scores, xcs, e_col, xws, cds = [], [], [], [], []
    for b, g, j in heads:
        h = g * HG + j
        col = b * NH + h
        a_col = acum_col[:, col:col + 1]
        a_last = a_col[L - 1:L, :]
        decay = jnp.exp(jnp.where(causal, a_col - acum_row[col:col + 1, :], -jnp.inf))
        xc = xbc[b][:, h * P:(h + 1) * P] * dt_col[:, col:col + 1]
        scores.append((cb[gidx(b, g)] * decay).astype(BF16))
        xcs.append(xc.astype(BF16))
        e_col.append(jnp.exp(a_col))
        xws.append(xc * jnp.exp(a_last - a_col))
        cds.append(jnp.broadcast_to(jnp.exp(a_last), (1, P)))

    y_diag = [_dot(sc, xc) for sc, xc in zip(scores, xcs)]
    states = [_dot(bg_t[i], jnp.concatenate(xws[i * HG:(i + 1) * HG], axis=1).astype(BF16))
              for i in range(len(groups))]

    yield
    for i, (b, g) in enumerate(groups):
        s_s[gidx(b, g)] = s_prev[i] * jnp.concatenate(cds[i * HG:(i + 1) * HG], axis=1) + states[i]

    gw = SSM_INNER // SSM_GROUPS
    for b in range(nb):
        ys = []
        for g in range(SSM_GROUPS):
            for j in range(HG):
                i = (b * SSM_GROUPS + g) * HG + j
                ys.append(y_diag[i] + y_off[gidx(b, g)][:, j * P:(j + 1) * P] * e_col[i])
        xs = xbc[b][:, :SSM_INNER]
        y = jnp.concatenate(ys, axis=1) + d_ref[...] * xs
        y = y * _silu(z_ref[b])
        normed = []
        for g in range(SSM_GROUPS):
            yg = y[:, g * gw:(g + 1) * gw]
            normed.append(yg * lax.rsqrt(jnp.mean(yg * yg, axis=-1, keepdims=True) + NORM_EPS))
        emit(b, (jnp.concatenate(normed, axis=1) * nw_ref[...]).astype(BF16))


def _layer1_kernel(xbc_ref, tail_ref, z_ref, dt_ref, cw_ref, cb_ref, dtb_ref, alog_ref, d_ref, nw_ref,
                   yrw_ref, h_ref, wa_ref, wb_ref, g_ref, wu_ref, wd_ref, fg_ref,
                   o_ref, s_s, y_s, *, nb, nc):
    step = pl.program_id(0)
    chunk = jnp.minimum(step, nc - 1)
    L = SSM_CHUNK

    @pl.when(step == 0)
    def _():
        s_s[...] = jnp.zeros_like(s_s)
        y_s[...] = jnp.zeros_like(y_s)

    wr = step % 2
    rd = 1 - wr

    def emit_ssm(b, block):
        y_s[wr, b * L:(b + 1) * L, :] = block

    def emit_out(block):
        o_ref[...] = block.reshape(nb, L, D_MODEL)

    mixers = [_ssd_stages(xbc_ref, tail_ref, z_ref, dt_ref, cw_ref, cb_ref, dtb_ref, alog_ref, d_ref, nw_ref, s_s,
                          chunk, nb, emit_ssm)]
    mlp = _outproj_mlp_stages([(y_s[rd], wa_ref), (yrw_ref[...].reshape(nb * L, RW_W), wb_ref)],
                              h_ref[...].reshape(nb * L, D_MODEL), g_ref, wu_ref, wd_ref, fg_ref, True, emit_out)
    _drive(mixers, mlp)


def _layer1(xbc, z, dt_t, conv_w, conv_b, dt_bias, a_log, d_row, norm_w, y_rw, h, wa, wb, g, wu, wd, fg, batch, seq):
    L = SSM_CHUNK
    nc = seq // L
    per8 = L // SUBLANES
    three = lambda t: t.reshape(batch, seq, t.shape[-1])
    mix = lambda c: jnp.minimum(c, nc - 1)
    cur = lambda width: pl.BlockSpec((batch, L, width), lambda c: (0, mix(c), 0))
    tail = pl.BlockSpec((batch, SUBLANES, SSM_CONV_W), lambda c: (0, jnp.maximum(mix(c) * per8 - 1, 0), 0))
    late = lambda width: pl.BlockSpec((batch, L, width), lambda c: (0, jnp.maximum(c - 1, 0), 0))
    consts = (conv_w, conv_b, dt_bias, a_log, d_row, norm_w)
    weights = (wa, wb, g, wu, wd, fg)
    x3 = three(xbc)
    out = pl.pallas_call(
        functools.partial(_layer1_kernel, nb=batch, nc=nc),
        grid=(nc + 1,),
        in_specs=[cur(SSM_CONV_W), tail, cur(SSM_INNER),
                  pl.BlockSpec((batch, SSM_HEADS, L), lambda c: (0, 0, mix(c)))]
                 + [_const_spec(a.shape) for a in consts] + [late(RW_W), late(D_MODEL)]
                 + [_const_spec(a.shape) for a in weights],
        out_specs=late(D_MODEL),
        out_shape=jax.ShapeDtypeStruct((batch, seq, D_MODEL), F32),
        scratch_shapes=[pltpu.VMEM((batch * SSM_GROUPS, SSM_STATE, SSM_HEADS_PER_GROUP * SSM_HEAD_DIM), F32),
                        pltpu.VMEM((2, batch * L, SSM_INNER), BF16)],
        compiler_params=_params("arbitrary"),
        name="layer1_mix_mlp",
    )(x3, x3, three(z), dt_t, *consts, three(y_rw), three(h), *weights)
    return out.reshape(batch * seq, D_MODEL)


def _split_bf16(x):
    hi = x.astype(BF16)
    lo = (x - hi.astype(F32)).astype(BF16)
    return hi, lo


def _dot_ones_exact(t, ones_matrix):
    hi, lo = _split_bf16(t)
    return _dot(hi, ones_matrix) + _dot(lo, ones_matrix)


def _unit_lower_inverses(mats, eye, n):
    xbs = [(eye - a).astype(BF16) for a in mats]
    mats = [a.astype(BF16) for a in mats]
    steps = 0
    terms = 2
    while terms < n:
        terms *= 2
        steps += 1
    for _ in range(steps):
        xfs = [xb.astype(F32) for xb in xbs]
        resids = [(eye - xf - _dot(a, xb)).astype(BF16) for a, xb, xf in zip(mats, xbs, xfs)]
        xbs = [(xf + _dot(xb, e)).astype(BF16) for xb, xf, e in zip(xbs, xfs, resids)]
    return xbs


def _rwkv_kernel(rw_ref, tail_ref, mu_ref, w0_ref, w2_ref, a0_ref, a2_ref, g2_ref, kk_ref, ka_ref, rk_ref,
                 lnw_ref, lnb_ref, y_ref, st_s, *, nb):
    chunk = pl.program_id(0)
    C = RW_CHUNK
    D = RW_HEAD_DIM
    R = nb * C

    @pl.when(chunk == 0)
    def _():
        st_s[...] = jnp.zeros_like(st_s)

    x = rw_ref[...].reshape(R, RW_IN)
    rolled = pltpu.roll(x, 1, axis=0)
    first_row = _iota((C, 1), 0) == 0
    prev = []
    for b in range(nb):
        before = jnp.where(chunk == 0, 0.0, tail_ref[b, SUBLANES - 1:SUBLANES, :])
        prev.append(jnp.where(first_row, before, rolled[b * C:(b + 1) * C]))
    prev = jnp.concatenate(prev, axis=0)
    xm = x + (prev - x) * mu_ref[...]
    r = xm[:, 0:RW_W]
    k = xm[:, RW_W:2 * RW_W]
    v = xm[:, 2 * RW_W:3 * RW_W]
    o = 3 * RW_W
    wd = xm[:, o:o + RW_DECAY_LORA]
    ad = xm[:, o + RW_DECAY_LORA:o + RW_DECAY_LORA + RW_AAA_LORA]
    gd = xm[:, o + RW_DECAY_LORA + RW_AAA_LORA:]

    z = w0_ref[...] + _dot(jnp.tanh(wd).astype(BF16), w2_ref[...])
    ld = -RW_EXP_MINUS_HALF * _sigmoid(z)
    a = _sigmoid(a0_ref[...] + _dot(ad.astype(BF16), a2_ref[...]))
    gate = _dot(_sigmoid(gd).astype(BF16), g2_ref[...])

    half = RW_W // 2
    same_head = (_iota((half, half), 0) // D == _iota((half, half), 1) // D).astype(BF16)
    head_sum = lambda t: jnp.concatenate(
        [_dot_ones_exact(t[:, :half], same_head), _dot_ones_exact(t[:, half:], same_head)], axis=1)

    kk = k * kk_ref[...]
    kk = kk * lax.rsqrt(jnp.maximum(head_sum(kk * kk), 1e-24))
    k2 = k * (1.0 + (a - 1.0) * ka_ref[...])
    bvec = kk * a

    tri = ((_iota((R, R), 1) <= _iota((R, R), 0)) & (_iota((R, R), 1) // C == _iota((R, R), 0) // C)).astype(BF16)
    cl = _sum3([_dot(tri, p) for p in _split3_bf16(ld)])
    c_last = jnp.concatenate(
        [jnp.broadcast_to(cl[(b + 1) * C - 1:(b + 1) * C, :], (C, RW_W)) for b in range(nb)], axis=0)
    e_neg = jnp.exp(-cl)
    e_end = jnp.exp(c_last - cl)
    w_end = [jnp.exp(cl[(b + 1) * C - 1:(b + 1) * C, :]) for b in range(nb)]
    rw_ = r * jnp.exp(cl)
    kkw = kk * jnp.exp(cl - ld)
    bd = bvec * e_neg
    kd = k2 * e_neg
    bd_end = bvec * e_end
    kd_end = k2 * e_end

    g_row = _iota((2 * C, 2 * C), 0)
    g_col = _iota((2 * C, 2 * C), 1) % C
    keep = g_col < jnp.where(g_row < C, g_row, g_row - C + 1)
    eye = _eye(C)
    chains = [(b, h) for b in range(nb) for h in range(RW_HEADS)]
    blk = lambda t, b, h: t[b * C:(b + 1) * C, h * D:(h + 1) * D]
    kkw_c = [blk(kkw, b, h).astype(BF16) for b, h in chains]
    rw_c = [blk(rw_, b, h).astype(BF16) for b, h in chains]
    vb_c = [blk(v, b, h).astype(BF16) for b, h in chains]
    gms = [jnp.where(keep, _dot_nt(jnp.concatenate([kq, rq], axis=0),
                                   jnp.concatenate([blk(bd, b, h), blk(kd, b, h)], axis=0).astype(BF16)), 0.0)
           for (b, h), kq, rq in zip(chains, kkw_c, rw_c)]
    a_v = [_dot(g[:, C:].astype(BF16), vb) for g, vb in zip(gms, vb_c)]
    a_rb = [g[C:, :C].astype(BF16) for g in gms]
    invs = _unit_lower_inverses([g[:C, :C] for g in gms], eye, C)
    tws = [_dot(inv, jnp.concatenate([av[:C].astype(BF16), kq], axis=1))
           for inv, av, kq in zip(invs, a_v, kkw_c)]
    sts = [st_s[b * RW_HEADS + h] for b, h in chains]
    xss = [_dot_nt(jnp.concatenate([tw[:, D:].astype(BF16), rq], axis=0), st.astype(BF16))
           for tw, rq, st in zip(tws, rw_c, sts)]
    ubs = [(-tw[:, :D] - xs[:C]).astype(BF16) for tw, xs in zip(tws, xss)]
    ys = [xs[C:] + _dot(m, ub) + av[C:] for xs, m, ub, av in zip(xss, a_rb, ubs, a_v)]
    for (b, h), st, ub, vb in zip(chains, sts, ubs, vb_c):
        ends = jnp.concatenate([blk(bd_end, b, h), blk(kd_end, b, h)], axis=0).astype(BF16)
        upd = lax.dot_general(jnp.concatenate([ub, vb], axis=0), ends, (((0,), (0,)), ((), ())),
                              preferred_element_type=F32)
        st_s[b * RW_HEADS + h] = st * w_end[b][:, h * D:(h + 1) * D] + upd

    y = jnp.concatenate([jnp.concatenate(ys[b * RW_HEADS:(b + 1) * RW_HEADS], axis=1) for b in range(nb)], axis=0)
    mu = head_sum(y) * (1.0 / D)
    var = head_sum(jnp.square(y - mu)) * (1.0 / D)
    yn = (y - mu) * lax.rsqrt(var + RW_LN_EPS) * lnw_ref[...] + lnb_ref[...]
    bonus = head_sum(r * k2 * rk_ref[...]) * v
    y_ref[...] = ((yn + bonus) * gate).astype(BF16).reshape(nb, C, RW_W)


def _rwkv(rw, consts, batch, seq):
    C = RW_CHUNK
    nc = seq // C
    per8 = C // SUBLANES
    rw3 = rw.reshape(batch, seq, RW_IN)
    y = pl.pallas_call(
        functools.partial(_rwkv_kernel, nb=batch),
        grid=(nc,),
        in_specs=[pl.BlockSpec((batch, C, RW_IN), lambda c: (0, c, 0)),
                  pl.BlockSpec((batch, SUBLANES, RW_IN), lambda c: (0, jnp.maximum(c * per8 - 1, 0), 0))]
                 + [_const_spec(a.shape) for a in consts],
        out_specs=pl.BlockSpec((batch, C, RW_W), lambda c: (0, c, 0)),
        out_shape=jax.ShapeDtypeStruct((batch, seq, RW_W), BF16),
        scratch_shapes=[pltpu.VMEM((batch * RW_HEADS, RW_HEAD_DIM, RW_HEAD_DIM), F32)],
        compiler_params=_params("arbitrary"),
        name="rwkv7",
    )(rw3, rw3, *consts)
    return y.reshape(batch * seq, RW_W)


@functools.lru_cache(maxsize=None)
def _rope_tables(seq):
    half = ROPE_DIM // 2
    inv_freq = ROPE_THETA ** (-np.arange(half, dtype=np.float64) * 2.0 / ROPE_DIM)
    ang = np.arange(seq, dtype=np.float64)[:, None] * inv_freq[None, :]
    cos, sin = np.cos(ang), np.sin(ang)
    ones = np.ones((seq, ATT_HEAD_DIM - ROPE_DIM))
    zeros = np.zeros((seq, ATT_HEAD_DIM - ROPE_DIM))
    zh = np.zeros((seq, half))
    reps = LANES // ATT_HEAD_DIM
    cos_t = np.concatenate([cos, cos, ones] * reps, axis=1).astype(np.float32)
    sin_lo = np.concatenate([-sin, zh, zeros] * reps, axis=1).astype(np.float32)
    sin_hi = np.concatenate([zh, sin, zeros] * reps, axis=1).astype(np.float32)
    return cos_t, sin_lo, sin_hi


def _row(v):
    return v.reshape(1, -1).astype(F32)


def _col(v):
    return v.reshape(-1, 1).astype(F32)


def kernel(x, mix_norm, mlp_norm, w_up, w_down, e_w_in, e_w_out, att_sinks, ml_conv_w, ml_conv_b, ml_i_bias,
           ml_f_bias, ml_norm, o_w_in, o_w_out, ssm_conv_w, ssm_conv_b, ssm_dt_bias, ssm_a_log, ssm_d, ssm_norm,
           rw_mu, rw_w0, rw_w2, rw_a0, rw_a2, rw_g2, rw_k_k, rw_k_a, rw_r_k, rw_ln_w, rw_ln_b, final_norm):
    batch, seq, _ = x.shape
    h = x.reshape(batch * seq, D_MODEL)
    fg = _row(final_norm)

    w = e_w_in[0].astype(BF16)
    o = 0
    parts = []
    for width in (ATT_Q_W, ATT_KV_W, ATT_KV_W, 2 * ML_QK_W, ML_V_W, ML_V_W):
        parts.append(w[:, o:o + width])
        o += width
    parts.append(w[:, o:o + 2 * ML_HEADS].T)
    qa, ka, va, qkm, vm, om, gt = _inproj_even(h, _row(mix_norm[0]), _rope_tables(seq), parts, seq)
    gate_bias = _col(jnp.concatenate([ml_i_bias[0], ml_f_bias[0]]))
    h = _layer0(qa, ka, va, _row(att_sinks[0]), qkm, vm, om, gt, ml_conv_w[0].astype(F32), _row(ml_conv_b[0]),
                gate_bias, _row(ml_norm[0]), h, e_w_out[0].astype(BF16), _row(mlp_norm[0]), w_up[0].astype(BF16),
                w_down[0].astype(BF16), fg, batch, seq)

    w = o_w_in[0].astype(BF16)
    o_x = SSM_INNER
    o_dt = o_x + SSM_CONV_W
    o_rw = o_dt + SSM_HEADS
    parts = [w[:, :o_x], w[:, o_x:o_dt], w[:, o_dt:o_rw].T, w[:, o_rw:]]
    z, xbc, dt_t, rw = _inproj_odd(h, _row(mix_norm[1]), parts, seq)
    rw_consts = (_row(rw_mu[0]), _row(rw_w0[0]), rw_w2[0].astype(BF16), _row(rw_a0[0]), rw_a2[0].astype(BF16),
                 rw_g2[0].astype(BF16), _row(rw_k_k[0]), _row(rw_k_a[0]), _row(rw_r_k[0]), _row(rw_ln_w[0]),
                 _row(rw_ln_b[0]))
    y_rw = _rwkv(rw, rw_consts, batch, seq)
    wo = o_w_out[0].astype(BF16)
    h = _layer1(xbc, z, dt_t, ssm_conv_w[0].astype(F32), _row(ssm_conv_b[0]), _col(ssm_dt_bias[0]),
                _col(ssm_a_log[0]), _row(jnp.repeat(ssm_d[0], SSM_HEAD_DIM)), _row(ssm_norm[0]), y_rw, h,
                wo[:SSM_INNER], wo[SSM_INNER:], _row(mlp_norm[1]), w_up[1].astype(BF16), w_down[1].astype(BF16), fg,
                batch, seq)
    return h.reshape(batch, seq, D_MODEL)
```

```python
import functools

import numpy as np

import jax
import jax.numpy as jnp
from jax import lax
from jax.experimental import pallas as pl
from jax.experimental.pallas import tpu as pltpu

F32 = jnp.float32
BF16 = jnp.bfloat16

D_MODEL = 1024
D_FF = 4 * D_MODEL
NORM_EPS = 1e-5
CONV_TAPS = 4

ATT_HEAD_DIM = 64
ATT_Q_HEADS = 8
ATT_KV_HEADS = 2
ATT_GROUP = ATT_Q_HEADS // ATT_KV_HEADS
ATT_WINDOW = 128
ROPE_THETA = 500000.0
ROPE_DIM = 16
ATT_Q_W = ATT_Q_HEADS * ATT_HEAD_DIM
ATT_KV_W = ATT_KV_HEADS * ATT_HEAD_DIM

ML_HEADS = 4
ML_V_DIM = 128
ML_QK_DIM = 64
ML_CHUNK = 128
ML_NORM_EPS = 1e-6
ML_QK_W = ML_HEADS * ML_QK_DIM
ML_V_W = ML_HEADS * ML_V_DIM

SSM_HEAD_DIM = 64
SSM_HEADS = 8
SSM_GROUPS = 2
SSM_HEADS_PER_GROUP = SSM_HEADS // SSM_GROUPS
SSM_STATE = 128
SSM_CHUNK = 128
SSM_INNER = SSM_HEADS * SSM_HEAD_DIM
SSM_BC = SSM_GROUPS * SSM_STATE
SSM_CONV_W = SSM_INNER + 2 * SSM_BC

RW_HEAD_DIM = 64
RW_HEADS = 8
RW_W = RW_HEADS * RW_HEAD_DIM
RW_DECAY_LORA = 64
RW_AAA_LORA = 64
RW_GATE_LORA = 128
RW_IN = 3 * RW_W + RW_DECAY_LORA + RW_AAA_LORA + RW_GATE_LORA
RW_LN_EPS = 64e-5
RW_CHUNK = 64
RW_EXP_MINUS_HALF = float(np.exp(-0.5))

V7X_VMEM_BYTES = 64 * 1024 * 1024
VMEM_LIMIT_BYTES = V7X_VMEM_BYTES - 6 * 1024 * 1024
SUBLANES = 8
LANES = 128

PROJ_ROWS = 1024
FF_CHUNK = 512


def _params(*sem):
    return pltpu.CompilerParams(dimension_semantics=sem, vmem_limit_bytes=VMEM_LIMIT_BYTES)


def _const_spec(shape):
    nd = len(shape)
    return pl.BlockSpec(shape, lambda *_: (0,) * nd, pipeline_mode=pl.Buffered(1))


def _iota(shape, dim):
    return lax.broadcasted_iota(jnp.int32, shape, dim)


def _tri_lower(n):
    return (_iota((n, n), 1) <= _iota((n, n), 0)).astype(F32)


def _tri_upper(n):
    return (_iota((n, n), 0) <= _iota((n, n), 1)).astype(F32)


def _eye(n):
    return (_iota((n, n), 0) == _iota((n, n), 1)).astype(F32)


def _dot(a, b, precision=None):
    return jnp.dot(a, b, preferred_element_type=F32, precision=precision)


def _dot_nt(a, b, precision=None):
    return lax.dot_general(a, b, (((1,), (1,)), ((), ())), preferred_element_type=F32, precision=precision)


def _split3_bf16(x):
    hi = x.astype(BF16)
    r1 = x - hi.astype(F32)
    mid = r1.astype(BF16)
    lo = (r1 - mid.astype(F32)).astype(BF16)
    return hi, mid, lo


def _sum3(terms):
    hi, mid, lo = terms
    return hi + (mid + lo)


def _sigmoid(x):
    return 1.0 / (1.0 + jnp.exp(-x))


def _softplus(x):
    return jnp.maximum(x, 0.0) + jnp.log1p(jnp.exp(-jnp.abs(x)))


def _silu(x):
    return x * _sigmoid(x)


def _rmsnorm(x, g):
    y = x * lax.rsqrt(jnp.mean(x * x, axis=-1, keepdims=True) + NORM_EPS)
    return y * g


def _shifted_rows(x, tail, k):
    rolled = pltpu.roll(x, k, axis=0)
    tail_rolled = pltpu.roll(tail, k, axis=0)
    first = jnp.where(_iota((SUBLANES, 1), 0) < k, tail_rolled, rolled[:SUBLANES])
    return jnp.concatenate([first, rolled[SUBLANES:]], axis=0)


def _causal_conv_silu(x, tail, w, b):
    out = b + x * w[CONV_TAPS - 1:CONV_TAPS, :]
    for k in range(1, CONV_TAPS):
        out = out + _shifted_rows(x, tail, k) * w[CONV_TAPS - 1 - k:CONV_TAPS - k, :]
    return _silu(out)


def _rope(x, cos, sin_lo, sin_hi, reps):
    w = x.shape[1]
    c = jnp.concatenate([cos] * reps, axis=1)
    s_lo = jnp.concatenate([sin_lo] * reps, axis=1)
    s_hi = jnp.concatenate([sin_hi] * reps, axis=1)
    half = ROPE_DIM // 2
    nxt = pltpu.roll(x, w - half, axis=1)
    prv = pltpu.roll(x, half, axis=1)
    return x * c + nxt * s_lo + prv * s_hi


def _inproj_even_kernel(x_ref, g_ref, cos_ref, slo_ref, shi_ref, wq_ref, wk_ref, wv_ref, wqk_ref, wvm_ref,
                        wom_ref, wgt_ref, qa_ref, ka_ref, va_ref, qkm_ref, vm_ref, om_ref, gt_ref):
    hb = _rmsnorm(x_ref[...], g_ref[...]).astype(BF16)
    cos, s_lo, s_hi = cos_ref[...], slo_ref[...], shi_ref[...]
    q = _rope(_dot(hb, wq_ref[...]), cos, s_lo, s_hi, ATT_Q_W // LANES)
    qa_ref[...] = (q * (ATT_HEAD_DIM ** -0.5)).astype(BF16)
    ka_ref[...] = _rope(_dot(hb, wk_ref[...]), cos, s_lo, s_hi, ATT_KV_W // LANES).astype(BF16)
    va_ref[...] = _dot(hb, wv_ref[...]).astype(BF16)
    qkm_ref[...] = _dot(hb, wqk_ref[...])
    vm_ref[...] = _dot(hb, wvm_ref[...]).astype(BF16)
    om_ref[...] = _dot(hb, wom_ref[...])
    gt_ref[...] = _dot_nt(wgt_ref[...], hb)


def _inproj_even(x2, g, tables, w, seq):
    n = x2.shape[0]
    tm = min(PROJ_ROWS, seq)
    per_seq = seq // tm
    row = lambda width: pl.BlockSpec((tm, width), lambda i: (i, 0))
    tab = pl.BlockSpec((tm, LANES), lambda i: (i % per_seq, 0))
    wq, wk, wv, wqk, wvm, wom, wgt = w
    out_shapes = (
        jax.ShapeDtypeStruct((n, ATT_Q_W), BF16), jax.ShapeDtypeStruct((n, ATT_KV_W), BF16),
        jax.ShapeDtypeStruct((n, ATT_KV_W), BF16), jax.ShapeDtypeStruct((n, 2 * ML_QK_W), F32),
        jax.ShapeDtypeStruct((n, ML_V_W), BF16), jax.ShapeDtypeStruct((n, ML_V_W), F32),
        jax.ShapeDtypeStruct((n // seq, 2 * ML_HEADS, seq), F32))
    return pl.pallas_call(
        _inproj_even_kernel,
        grid=(n // tm,),
        in_specs=[row(D_MODEL), _const_spec((1, D_MODEL)), tab, tab, tab] + [_const_spec(a.shape) for a in w],
        out_specs=(row(ATT_Q_W), row(ATT_KV_W), row(ATT_KV_W), row(2 * ML_QK_W), row(ML_V_W), row(ML_V_W),
                   pl.BlockSpec((None, 2 * ML_HEADS, tm), lambda i: (i // per_seq, 0, i % per_seq))),
        out_shape=out_shapes,
        compiler_params=_params("arbitrary"),
        name="inproj_even",
    )(x2, g, *tables, wq, wk, wv, wqk, wvm, wom, wgt)


def _attention_stages(q_ref, kc_ref, kp_ref, vc_ref, vp_ref, sink_ref, blk, nb, emit):
    w = ATT_WINDOW
    dh = ATT_HEAD_DIM
    G = ATT_GROUP
    i = _iota((G * w, 2 * w), 0) % w
    j = _iota((G * w, 2 * w), 1)
    mask = (j > i) & (j <= i + w) & ((j >= w) | (blk > 0))
    sinks = sink_ref[...]
    problems = [(b, g) for b in range(nb) for g in range(ATT_KV_HEADS)]
    scores, values, sink_cols = [], [], []
    for b, g in problems:
        q = q_ref[b]
        qs = jnp.concatenate([q[:, (g * G + a) * dh:(g * G + a + 1) * dh] for a in range(G)], axis=0)
        ks = jnp.concatenate([kp_ref[b][:, g * dh:(g + 1) * dh], kc_ref[b][:, g * dh:(g + 1) * dh]], axis=0)
        values.append(jnp.concatenate([vp_ref[b][:, g * dh:(g + 1) * dh], vc_ref[b][:, g * dh:(g + 1) * dh]], axis=0))
        scores.append(_dot_nt(qs, ks))
        sink_cols.append(jnp.concatenate(
            [jnp.broadcast_to(sinks[:, g * G + a:g * G + a + 1], (w, 1)) for a in range(G)], axis=0))
    yield
    probs, dens = [], []
    for s, sink in zip(scores, sink_cols):
        s = jnp.where(mask, s, -jnp.inf)
        m = jnp.maximum(jnp.max(s, axis=-1, keepdims=True), sink)
        p = jnp.exp(s - m)
        dens.append(jnp.sum(p, axis=-1, keepdims=True) + jnp.exp(sink - m))
        probs.append(p.astype(BF16))
    pv = [_dot(p, v) for p, v in zip(probs, values)]
    yield
    outs = [o / d for o, d in zip(pv, dens)]
    for b in range(nb):
        heads = [outs[b * ATT_KV_HEADS + g][a * w:(a + 1) * w] for g in range(ATT_KV_HEADS) for a in range(G)]
        emit(b, jnp.concatenate(heads, axis=1).astype(BF16))


def _mlstm_stages(qk_ref, tail_ref, v_ref, o_ref, gt_ref, cw_ref, cb_ref, gb_ref, nw_ref, c_s, m_s, chunk, nb, emit):
    L = ML_CHUNK
    H = ML_HEADS
    DV = ML_V_DIM
    NG = 2 * H

    cw, cb = cw_ref[...], cb_ref[...]
    qk = [_causal_conv_silu(qk_ref[b], jnp.where(chunk == 0, 0.0, tail_ref[b]), cw, cb) for b in range(nb)]

    g = jnp.concatenate([gt_ref[b] + gb_ref[...] for b in range(nb)], axis=0)
    lg = jnp.where(_iota(g.shape, 0) % NG < H, g, -_softplus(-g))
    tri_l, tri_u, eye = _tri_lower(L).astype(BF16), _tri_upper(L).astype(BF16), _eye(L).astype(BF16)
    lg3 = _split3_bf16(lg)
    cum_row = _sum3([_dot(p, tri_u) for p in lg3])
    lg_col = _sum3([_dot_nt(eye, p) for p in lg3])
    cum_col = _sum3([_dot_nt(tri_l, p) for p in lg3])
    causal = _iota((L, L), 1) <= _iota((L, L), 0)

    chains = [(b, h) for b in range(nb) for h in range(H)]
    idx = lambda b, h: b * H + h
    q_c = [qk[b][:, h * ML_QK_DIM:(h + 1) * ML_QK_DIM] for b, h in chains]
    k_c = [qk[b][:, ML_QK_W + h * ML_QK_DIM:ML_QK_W + (h + 1) * ML_QK_DIM] * (ML_QK_DIM ** -0.5) for b, h in chains]
    ones = jnp.ones((L, DV), BF16)
    v_c = [jnp.concatenate([v_ref[b][:, h * DV:(h + 1) * DV], ones], axis=1) for b, h in chains]
    qb_c = [q.astype(BF16) for q in q_c]
    c_prev = [c_s[idx(b, h)] for b, h in chains]
    m_prev = [m_s[idx(b, h):idx(b, h) + 1, 0:1] for b, h in chains]

    s_qk = [_dot_nt(qb, k.astype(BF16)) for qb, k in zip(qb_c, k_c)]
    inter = [_dot(qb, c.astype(BF16)) for qb, c in zip(qb_c, c_prev)]

    yield
    gi_c = [b * NG + h for b, h in chains]
    b_col = [cum_col[:, gi + H:gi + H + 1] for gi in gi_c]
    b_last = [bc[L - 1:L, :] for bc in b_col]
    dlog = [jnp.where(causal, bc - cum_row[gi + H:gi + H + 1, :] + lg[gi:gi + 1, :], -jnp.inf)
            for gi, bc in zip(gi_c, b_col)]
    row_max = [jnp.max(d, axis=1, keepdims=True) for d in dlog]
    a_loc = [bl - bc + lg_col[:, gi:gi + 1] for gi, bc, bl in zip(gi_c, b_col, b_last)]
    m_loc = [jnp.max(a, axis=0, keepdims=True) for a in a_loc]
    inter_log = [bc + mp for bc, mp in zip(b_col, m_prev)]
    m_t = [jnp.maximum(il, rm) for il, rm in zip(inter_log, row_max)]
    qk_w = [s * jnp.exp(d - mt) for s, d, mt in zip(s_qk, dlog, m_t)]
    w_inter = [jnp.exp(il - mt) for il, mt in zip(inter_log, m_t)]
    kw = [k * jnp.exp(a - ml) for k, a, ml in zip(k_c, a_loc, m_loc)]

    intra = [_dot(w.astype(BF16), v) for w, v in zip(qk_w, v_c)]
    c_loc = [lax.dot_general(w.astype(BF16), v, (((0,), (0,)), ((), ())), preferred_element_type=F32)
             for w, v in zip(kw, v_c)]

    yield
    both = [a + w * e for a, w, e in zip(intra, w_inter, inter)]
    xs = [nd[:, :DV] / jnp.maximum(jnp.abs(nd[:, DV:DV + 1]), jnp.exp(-mt)) for nd, mt in zip(both, m_t)]
    mus = [jnp.mean(x, axis=-1, keepdims=True) for x in xs]
    cen = [x - mu for x, mu in zip(xs, mus)]
    var = [jnp.mean(jnp.square(c), axis=-1, keepdims=True) for c in cen]
    outs = [c * lax.rsqrt(vr + ML_NORM_EPS) for c, vr in zip(cen, var)]
    for i, (b, h) in enumerate(chains):
        m_new = jnp.maximum(b_last[i] + m_prev[i], m_loc[i])
        fa = jnp.exp(b_last[i] + m_prev[i] - m_new)
        fb = jnp.exp(m_loc[i] - m_new)
        c_s[idx(b, h)] = fa * c_prev[i] + fb * c_loc[i]
        m_s[idx(b, h):idx(b, h) + 1, 0:1] = m_new

    nw = nw_ref[...]
    for b in range(nb):
        hm = jnp.concatenate(outs[b * H:(b + 1) * H], axis=1) * nw
        emit(b, (hm * _sigmoid(o_ref[b])).astype(BF16))


def _outproj_mlp_stages(y_parts, h, g_ref, wu_ref, wd_ref, fg_ref, final_norm, emit):
    h1 = h
    for y, w_ref in y_parts:
        h1 = h1 + _dot(y, w_ref[...])
    hb = _rmsnorm(h1, g_ref[...]).astype(BF16)
    n_chunks = D_FF // FF_CHUNK
    mlp = None
    for c in range(n_chunks):
        u = _dot(hb, wu_ref[:, c * FF_CHUNK:(c + 1) * FF_CHUNK])
        a = jnp.square(jnp.maximum(u, 0.0)).astype(BF16)
        d = _dot(a, wd_ref[c * FF_CHUNK:(c + 1) * FF_CHUNK, :])
        mlp = d if mlp is None else mlp + d
        if c + 1 in MLP_SPLIT:
            yield
    h2 = h1 + mlp
    if final_norm:
        h2 = _rmsnorm(h2, fg_ref[...])
    emit(h2)


MLP_SPLIT = (5,)


def _drive(mixers, mlp):
    for m in mixers:
        next(m)
    next(mlp)
    for m in mixers:
        next(m)
    for m in mixers:
        for _ in m:
            pass
    for _ in mlp:
        pass


def _layer0_kernel(q_ref, kc_ref, kp_ref, vc_ref, vp_ref, sink_ref,
                   qk_ref, tail_ref, vm_ref, om_ref, gt_ref, cw_ref, cb_ref, gb_ref, nw_ref,
                   h_ref, wo_ref, g_ref, wu_ref, wd_ref, fg_ref,
                   o_ref, c_s, m_s, y_s, *, nb, nc):
    step = pl.program_id(0)
    chunk = jnp.minimum(step, nc - 1)
    L = ML_CHUNK

    @pl.when(step == 0)
    def _():
        c_s[...] = jnp.zeros_like(c_s)
        m_s[...] = jnp.zeros_like(m_s)
        y_s[...] = jnp.zeros_like(y_s)

    wr = step % 2
    rd = 1 - wr

    def emit_att(b, block):
        y_s[wr, b * L:(b + 1) * L, 0:ATT_Q_W] = block

    def emit_ml(b, block):
        y_s[wr, b * L:(b + 1) * L, ATT_Q_W:ATT_Q_W + ML_V_W] = block

    def emit_out(block):
        o_ref[...] = block.reshape(nb, L, D_MODEL)

    mixers = [
        _attention_stages(q_ref, kc_ref, kp_ref, vc_ref, vp_ref, sink_ref, chunk, nb, emit_att),
        _mlstm_stages(qk_ref, tail_ref, vm_ref, om_ref, gt_ref, cw_ref, cb_ref, gb_ref, nw_ref, c_s, m_s, chunk, nb,
                      emit_ml),
    ]
    mlp = _outproj_mlp_stages([(y_s[rd], wo_ref)], h_ref[...].reshape(nb * L, D_MODEL), g_ref, wu_ref, wd_ref,
                              fg_ref, False, emit_out)
    _drive(mixers, mlp)


def _layer0(qa, ka, va, sinks, qkm, vm, om, gt, conv_w, conv_b, gate_bias, norm_w, h, wo, g, wu, wd, fg, batch, seq):
    L = ML_CHUNK
    nc = seq // L
    per8 = L // SUBLANES
    three = lambda t: t.reshape(batch, seq, t.shape[-1])
    mix = lambda c: jnp.minimum(c, nc - 1)
    cur = lambda width: pl.BlockSpec((batch, L, width), lambda c: (0, mix(c), 0))
    prev = lambda width: pl.BlockSpec((batch, L, width), lambda c: (0, jnp.maximum(mix(c) - 1, 0), 0))
    tail = pl.BlockSpec((batch, SUBLANES, 2 * ML_QK_W), lambda c: (0, jnp.maximum(mix(c) * per8 - 1, 0), 0))
    late = pl.BlockSpec((batch, L, D_MODEL), lambda c: (0, jnp.maximum(c - 1, 0), 0))
    q3, k3, v3, qk3 = three(qa), three(ka), three(va), three(qkm)
    consts = (conv_w, conv_b, gate_bias, norm_w)
    weights = (wo, g, wu, wd, fg)
    out = pl.pallas_call(
        functools.partial(_layer0_kernel, nb=batch, nc=nc),
        grid=(nc + 1,),
        in_specs=[cur(ATT_Q_W), cur(ATT_KV_W), prev(ATT_KV_W), cur(ATT_KV_W), prev(ATT_KV_W),
                  _const_spec((1, ATT_Q_HEADS)),
                  cur(2 * ML_QK_W), tail, cur(ML_V_W), cur(ML_V_W),
                  pl.BlockSpec((batch, 2 * ML_HEADS, L), lambda c: (0, 0, mix(c)))]
                 + [_const_spec(a.shape) for a in consts] + [late] + [_const_spec(a.shape) for a in weights],
        out_specs=late,
        out_shape=jax.ShapeDtypeStruct((batch, seq, D_MODEL), F32),
        scratch_shapes=[pltpu.VMEM((batch * ML_HEADS, ML_QK_DIM, 2 * ML_V_DIM), F32),
                        pltpu.VMEM((batch * ML_HEADS, LANES), F32),
                        pltpu.VMEM((2, batch * L, ATT_Q_W + ML_V_W), BF16)],
        compiler_params=_params("arbitrary"),
        name="layer0_mix_mlp",
    )(q3, k3, k3, v3, v3, sinks, qk3, qk3, three(vm), three(om), gt, *consts, three(h), *weights)
    return out.reshape(batch * seq, D_MODEL)


def _inproj_odd_kernel(x_ref, g_ref, wz_ref, wx_ref, wdt_ref, wrw_ref, z_ref, xbc_ref, dt_ref, rw_ref):
    hb = _rmsnorm(x_ref[...], g_ref[...]).astype(BF16)
    z_ref[...] = _dot(hb, wz_ref[...])
    xbc_ref[...] = _dot(hb, wx_ref[...])
    dt_ref[...] = _dot_nt(wdt_ref[...], hb)
    rw_ref[...] = _dot(hb, wrw_ref[...])


def _inproj_odd(x2, g, w, seq):
    n = x2.shape[0]
    tm = min(PROJ_ROWS, seq)
    per_seq = seq // tm
    row = lambda width: pl.BlockSpec((tm, width), lambda i: (i, 0))
    out_shapes = (jax.ShapeDtypeStruct((n, SSM_INNER), F32), jax.ShapeDtypeStruct((n, SSM_CONV_W), F32),
                  jax.ShapeDtypeStruct((n // seq, SSM_HEADS, seq), F32), jax.ShapeDtypeStruct((n, RW_IN), F32))
    return pl.pallas_call(
        _inproj_odd_kernel,
        grid=(n // tm,),
        in_specs=[row(D_MODEL), _const_spec((1, D_MODEL))] + [_const_spec(a.shape) for a in w],
        out_specs=(row(SSM_INNER), row(SSM_CONV_W),
                   pl.BlockSpec((None, SSM_HEADS, tm), lambda i: (i // per_seq, 0, i % per_seq)), row(RW_IN)),
        out_shape=out_shapes,
        compiler_params=_params("arbitrary"),
        name="inproj_odd",
    )(x2, g, *w)


def _ssd_stages(xbc_ref, tail_ref, z_ref, dt_ref, cw_ref, cb_ref, dtb_ref, alog_ref, d_ref, nw_ref, s_s, chunk, nb,
                emit):
    L = SSM_CHUNK
    P = SSM_HEAD_DIM
    HG = SSM_HEADS_PER_GROUP
    NH = SSM_HEADS

    cw, cb_ = cw_ref[...], cb_ref[...]
    xbc = [_causal_conv_silu(xbc_ref[b], jnp.where(chunk == 0, 0.0, tail_ref[b]), cw, cb_) for b in range(nb)]

    dt_row = _softplus(jnp.concatenate([dt_ref[b] + dtb_ref[...] for b in range(nb)], axis=0))
    neg_a = -jnp.exp(jnp.concatenate([alog_ref[...]] * nb, axis=0))
    a_row = dt_row * neg_a
    tri_l, tri_u, eye_b = _tri_lower(L).astype(BF16), _tri_upper(L).astype(BF16), _eye(L).astype(BF16)
    a3 = _split3_bf16(a_row)
    acum_row = _sum3([_dot(p, tri_u) for p in a3])
    acum_col = _sum3([_dot_nt(tri_l, p) for p in a3])
    dt_col = _sum3([_dot_nt(eye_b, p) for p in _split3_bf16(dt_row)])
    causal = _iota((L, L), 1) <= _iota((L, L), 0)

    groups = [(b, g) for b in range(nb) for g in range(SSM_GROUPS)]
    gidx = lambda b, g: b * SSM_GROUPS + g
    bg = [xbc[b][:, SSM_INNER + g * SSM_STATE:SSM_INNER + (g + 1) * SSM_STATE].astype(BF16) for b, g in groups]
    cg = [xbc[b][:, SSM_INNER + SSM_BC + g * SSM_STATE:SSM_INNER + SSM_BC + (g + 1) * SSM_STATE].astype(BF16)
          for b, g in groups]
    s_prev = [s_s[gidx(b, g)] for b, g in groups]

    cb = [_dot_nt(c, bq) for c, bq in zip(cg, bg)]
    bg_t = [_dot_nt(eye_b, bq).astype(BF16) for bq in bg]
    y_off = [_dot(c, sp.astype(BF16)) for c, sp in zip(cg, s_prev)]

    yield
    heads = [(b, g, j) for b, g in groups for j in range(HG)]
    scores, xcs, e_col, xws, cds = [], [], [], [], []
    for b, g, j in heads:
        h = g * HG + j
        col = b * NH + h
        a_col = acum_col[:, col:col + 1]
        a_last = a_col[L - 1:L, :]
        decay = jnp.exp(jnp.where(causal, a_col - acum_row[col:col + 1, :], -jnp.inf))
        xc = xbc[b][:, h * P:(h + 1) * P] * dt_col[:, col:col + 1]
        scores.append((cb[gidx(b, g)] * decay).astype(BF16))
        xcs.append(xc.astype(BF16))
        e_col.append(jnp.exp(a_col))
        xws.append(xc * jnp.exp(a_last - a_col))
        cds.append(jnp.broadcast_to(jnp.exp(a_last), (1, P)))

    y_diag = [_dot(sc, xc) for sc, xc in zip(scores, xcs)]
    states = [_dot(bg_t[i], jnp.concatenate(xws[i * HG:(i + 1) * HG], axis=1).astype(BF16))
              for i in range(len(groups))]

    yield
    for i, (b, g) in enumerate(groups):
        s_s[gidx(b, g)] = s_prev[i] * jnp.concatenate(cds[i * HG:(i + 1) * HG], axis=1) + states[i]

    gw = SSM_INNER // SSM_GROUPS
    for b in range(nb):
        ys = []
        for g in range(SSM_GROUPS):
            for j in range(HG):
                i = (b * SSM_GROUPS + g) * HG + j
                ys.append(y_diag[i] + y_off[gidx(b, g)][:, j * P:(j + 1) * P] * e_col[i])
        xs = xbc[b][:, :SSM_INNER]
        y = jnp.concatenate(ys, axis=1) + d_ref[...] * xs
        y = y * _silu(z_ref[b])
        normed = []
        for g in range(SSM_GROUPS):
            yg = y[:, g * gw:(g + 1) * gw]
            normed.append(yg * lax.rsqrt(jnp.mean(yg * yg, axis=-1, keepdims=True) + NORM_EPS))
        emit(b, (jnp.concatenate(normed, axis=1) * nw_ref[...]).astype(BF16))


def _layer1_kernel(xbc_ref, tail_ref, z_ref, dt_ref, cw_ref, cb_ref, dtb_ref, alog_ref, d_ref, nw_ref,
                   yrw_ref, h_ref, wa_ref, wb_ref, g_ref, wu_ref, wd_ref, fg_ref,
                   o_ref, s_s, y_s, *, nb, nc):
    step = pl.program_id(0)
    chunk = jnp.minimum(step, nc - 1)
    L = SSM_CHUNK

    @pl.when(step == 0)
    def _():
        s_s[...] = jnp.zeros_like(s_s)
        y_s[...] = jnp.zeros_like(y_s)

    wr = step % 2
    rd = 1 - wr

    def emit_ssm(b, block):
        y_s[wr, b * L:(b + 1) * L, :] = block

    def emit_out(block):
        o_ref[...] = block.reshape(nb, L, D_MODEL)

    mixers = [_ssd_stages(xbc_ref, tail_ref, z_ref, dt_ref, cw_ref, cb_ref, dtb_ref, alog_ref, d_ref, nw_ref, s_s,
                          chunk, nb, emit_ssm)]
    mlp = _outproj_mlp_stages([(y_s[rd], wa_ref), (yrw_ref[...].reshape(nb * L, RW_W), wb_ref)],
                              h_ref[...].reshape(nb * L, D_MODEL), g_ref, wu_ref, wd_ref, fg_ref, True, emit_out)
    _drive(mixers, mlp)


def _layer1(xbc, z, dt_t, conv_w, conv_b, dt_bias, a_log, d_row, norm_w, y_rw, h, wa, wb, g, wu, wd, fg, batch, seq):
    L = SSM_CHUNK
    nc = seq // L
    per8 = L // SUBLANES
    three = lambda t: t.reshape(batch, seq, t.shape[-1])
    mix = lambda c: jnp.minimum(c, nc - 1)
    cur = lambda width: pl.BlockSpec((batch, L, width), lambda c: (0, mix(c), 0))
    tail = pl.BlockSpec((batch, SUBLANES, SSM_CONV_W), lambda c: (0, jnp.maximum(mix(c) * per8 - 1, 0), 0))
    late = lambda width: pl.BlockSpec((batch, L, width), lambda c: (0, jnp.maximum(c - 1, 0), 0))
    consts = (conv_w, conv_b, dt_bias, a_log, d_row, norm_w)
    weights = (wa, wb, g, wu, wd, fg)
    x3 = three(xbc)
    out = pl.pallas_call(
        functools.partial(_layer1_kernel, nb=batch, nc=nc),
        grid=(nc + 1,),
        in_specs=[cur(SSM_CONV_W), tail, cur(SSM_INNER),
                  pl.BlockSpec((batch, SSM_HEADS, L), lambda c: (0, 0, mix(c)))]
                 + [_const_spec(a.shape) for a in consts] + [late(RW_W), late(D_MODEL)]
                 + [_const_spec(a.shape) for a in weights],
        out_specs=late(D_MODEL),
        out_shape=jax.ShapeDtypeStruct((batch, seq, D_MODEL), F32),
        scratch_shapes=[pltpu.VMEM((batch * SSM_GROUPS, SSM_STATE, SSM_HEADS_PER_GROUP * SSM_HEAD_DIM), F32),
                        pltpu.VMEM((2, batch * L, SSM_INNER), BF16)],
        compiler_params=_params("arbitrary"),
        name="layer1_mix_mlp",
    )(x3, x3, three(z), dt_t, *consts, three(y_rw), three(h), *weights)
    return out.reshape(batch * seq, D_MODEL)


def _split_bf16(x):
    hi = x.astype(BF16)
    lo = (x - hi.astype(F32)).astype(BF16)
    return hi, lo


def _dot_ones_exact(t, ones_matrix):
    hi, lo = _split_bf16(t)
    return _dot(hi, ones_matrix) + _dot(lo, ones_matrix)


def _unit_lower_inverse_steps(mats, eye, n):
    xbs = [(eye - a).astype(BF16) for a in mats]
    mats = [a.astype(BF16) for a in mats]
    terms = 2
    while terms < n:
        terms *= 2
        xfs = [xb.astype(F32) for xb in xbs]
        resids = [(eye - xf - _dot(a, xb)).astype(BF16) for a, xb, xf in zip(mats, xbs, xfs)]
        xbs = [(xf + _dot(xb, e)).astype(BF16) for xb, xf, e in zip(xbs, xfs, resids)]
        yield xbs


RW_FRONT_STAGES = 4
RW_CHAIN_STAGES = 10


def _rwkv_stages(x, before, mu_ref, w0_ref, w2_ref, a0_ref, a2_ref, g2_ref, kk_ref, ka_ref, rk_ref, lnw_ref, lnb_ref,
                 st_s, nb, emit):
    C = RW_CHUNK
    D = RW_HEAD_DIM
    R = nb * C
    rolled = pltpu.roll(x, 1, axis=0)
    first_row = _iota((C, 1), 0) == 0
    prev = jnp.concatenate([jnp.where(first_row, before[b], rolled[b * C:(b + 1) * C]) for b in range(nb)], axis=0)
    xm = x + (prev - x) * mu_ref[...]
    r = xm[:, 0:RW_W]
    k = xm[:, RW_W:2 * RW_W]
    v = xm[:, 2 * RW_W:3 * RW_W]
    o = 3 * RW_W
    wd = xm[:, o:o + RW_DECAY_LORA]
    ad = xm[:, o + RW_DECAY_LORA:o + RW_DECAY_LORA + RW_AAA_LORA]
    gd = xm[:, o + RW_DECAY_LORA + RW_AAA_LORA:]
    yield

    z = w0_ref[...] + _dot(jnp.tanh(wd).astype(BF16), w2_ref[...])
    ld = -RW_EXP_MINUS_HALF * _sigmoid(z)
    a = _sigmoid(a0_ref[...] + _dot(ad.astype(BF16), a2_ref[...]))
    gate = _dot(_sigmoid(gd).astype(BF16), g2_ref[...])
    yield

    half = RW_W // 2
    same_head = (_iota((half, half), 0) // D == _iota((half, half), 1) // D).astype(BF16)
    head_sum = lambda t: jnp.concatenate(
        [_dot_ones_exact(t[:, :half], same_head), _dot_ones_exact(t[:, half:], same_head)], axis=1)

    kk = k * kk_ref[...]
    kk = kk * lax.rsqrt(jnp.maximum(head_sum(kk * kk), 1e-24))
    k2 = k * (1.0 + (a - 1.0) * ka_ref[...])
    bvec = kk * a
    yield

    tri = ((_iota((R, R), 1) <= _iota((R, R), 0)) & (_iota((R, R), 1) // C == _iota((R, R), 0) // C)).astype(BF16)
    cl = _sum3([_dot(tri, p) for p in _split3_bf16(ld)])
    c_last = jnp.concatenate(
        [jnp.broadcast_to(cl[(b + 1) * C - 1:(b + 1) * C, :], (C, RW_W)) for b in range(nb)], axis=0)
    e_neg = jnp.exp(-cl)
    e_end = jnp.exp(c_last - cl)
    w_end = [jnp.exp(cl[(b + 1) * C - 1:(b + 1) * C, :]) for b in range(nb)]
    rw_ = r * jnp.exp(cl)
    kkw = kk * jnp.exp(cl - ld)
    bd = bvec * e_neg
    kd = k2 * e_neg
    bd_end = bvec * e_end
    kd_end = k2 * e_end
    yield

    g_row = _iota((2 * C, 2 * C), 0)
    g_col = _iota((2 * C, 2 * C), 1) % C
    keep = g_col < jnp.where(g_row < C, g_row, g_row - C + 1)
    eye = _eye(C)
    chains = [(b, h) for b in range(nb) for h in range(RW_HEADS)]
    blk = lambda t, b, h: t[b * C:(b + 1) * C, h * D:(h + 1) * D]
    kkw_c = [blk(kkw, b, h).astype(BF16) for b, h in chains]
    rw_c = [blk(rw_, b, h).astype(BF16) for b, h in chains]
    vb_c = [blk(v, b, h).astype(BF16) for b, h in chains]
    gms = [jnp.where(keep, _dot_nt(jnp.concatenate([kq, rq], axis=0),
                                   jnp.concatenate([blk(bd, b, h), blk(kd, b, h)], axis=0).astype(BF16)), 0.0)
           for (b, h), kq, rq in zip(chains, kkw_c, rw_c)]
    yield
    a_v = [_dot(g[:, C:].astype(BF16), vb) for g, vb in zip(gms, vb_c)]
    a_rb = [g[C:, :C].astype(BF16) for g in gms]
    yield
    invs = None
    for invs in _unit_lower_inverse_steps([g[:C, :C] for g in gms], eye, C):
        yield
    tws = [_dot(inv, jnp.concatenate([av[:C].astype(BF16), kq], axis=1))
           for inv, av, kq in zip(invs, a_v, kkw_c)]
    yield
    sts = [st_s[b * RW_HEADS + h] for b, h in chains]
    xss = [_dot_nt(jnp.concatenate([tw[:, D:].astype(BF16), rq], axis=0), st.astype(BF16))
           for tw, rq, st in zip(tws, rw_c, sts)]
    yield
    ubs =[(-tw[:, :D] - xs[:C]).astype(BF16) for tw, xs in zip(tws, xss)]
    ys = [xs[C:] + _dot(m, ub) + av[C:] for xs, m, ub, av in zip(xss, a_rb, ubs, a_v)]
    for (b, h), st, ub, vb in zip(chains, sts, ubs, vb_c):
        ends = jnp.concatenate([blk(bd_end, b, h), blk(kd_end, b, h)], axis=0).astype(BF16)
        upd = lax.dot_general(jnp.concatenate([ub, vb], axis=0), ends, (((0,), (0,)), ((), ())),
                              preferred_element_type=F32)
        st_s[b * RW_HEADS + h] = st * w_end[b][:, h * D:(h + 1) * D] + upd
    yield

    y = jnp.concatenate([jnp.concatenate(ys[b * RW_HEADS:(b + 1) * RW_HEADS], axis=1) for b in range(nb)], axis=0)
    mu = head_sum(y) * (1.0 / D)
    var = head_sum(jnp.square(y - mu)) * (1.0 / D)
    yn = (y - mu) * lax.rsqrt(var + RW_LN_EPS) * lnw_ref[...] + lnb_ref[...]
    bonus = head_sum(r * k2 * rk_ref[...]) * v
    emit(((yn + bonus) * gate).astype(BF16))


RW_STEP_CHUNKS = 2


def _rwkv_kernel(rw_ref, tail_ref, mu_ref, w0_ref, w2_ref, a0_ref, a2_ref, g2_ref, kk_ref, ka_ref, rk_ref,
                 lnw_ref, lnb_ref, y_ref, st_s, *, nb):
    step = pl.program_id(0)
    C = RW_CHUNK

    @pl.when(step == 0)
    def _():
        st_s[...] = jnp.zeros_like(st_s)

    xs = [rw_ref[:, j * C:(j + 1) * C, :].reshape(nb * C, RW_IN) for j in range(RW_STEP_CHUNKS)]
    first = [jnp.where(step == 0, 0.0, tail_ref[b, SUBLANES - 1:SUBLANES, :]) for b in range(nb)]
    second = [xs[0][(b + 1) * C - 1:(b + 1) * C, :] for b in range(nb)]

    def emit(j):
        def write(block):
            y_ref[:, j * C:(j + 1) * C, :] = block.reshape(nb, C, RW_W)
        return write

    args = (mu_ref, w0_ref, w2_ref, a0_ref, a2_ref, g2_ref, kk_ref, ka_ref, rk_ref, lnw_ref, lnb_ref, st_s, nb)
    g0 = _rwkv_stages(xs[0], first, *args, emit(0))
    g1 = _rwkv_stages(xs[1], second, *args, emit(1))
    for _ in range(RW_FRONT_STAGES):
        next(g0)
    for _ in range(RW_FRONT_STAGES):
        next(g1)
    for _ in range(RW_CHAIN_STAGES):
        next(g0)
    next(g1)
    for _ in g0:
        pass
    for _ in g1:
        pass


def _rwkv(rw, consts, batch, seq):
    C = RW_CHUNK * RW_STEP_CHUNKS
    nc = seq // C
    per8 = C // SUBLANES
    rw3 = rw.reshape(batch, seq, RW_IN)
    y = pl.pallas_call(
        functools.partial(_rwkv_kernel, nb=batch),
        grid=(nc,),
        in_specs=[pl.BlockSpec((batch, C, RW_IN), lambda c: (0, c, 0)),
                  pl.BlockSpec((batch, SUBLANES, RW_IN), lambda c: (0, jnp.maximum(c * per8 - 1, 0), 0))]
                 + [_const_spec(a.shape) for a in consts],
        out_specs=pl.BlockSpec((batch, C, RW_W), lambda c: (0, c, 0)),
        out_shape=jax.ShapeDtypeStruct((batch, seq, RW_W), BF16),
        scratch_shapes=[pltpu.VMEM((batch * RW_HEADS, RW_HEAD_DIM, RW_HEAD_DIM), F32)],
        compiler_params=_params("arbitrary"),
        name="rwkv7",
    )(rw3, rw3, *consts)
    return y.reshape(batch * seq, RW_W)


@functools.lru_cache(maxsize=None)
def _rope_tables(seq):
    half = ROPE_DIM // 2
    inv_freq = ROPE_THETA ** (-np.arange(half, dtype=np.float64) * 2.0 / ROPE_DIM)
    ang = np.arange(seq, dtype=np.float64)[:, None] * inv_freq[None, :]
    cos, sin = np.cos(ang), np.sin(ang)
    ones = np.ones((seq, ATT_HEAD_DIM - ROPE_DIM))
    zeros = np.zeros((seq, ATT_HEAD_DIM - ROPE_DIM))
    zh = np.zeros((seq, half))
    reps = LANES // ATT_HEAD_DIM
    cos_t = np.concatenate([cos, cos, ones] * reps, axis=1).astype(np.float32)
    sin_lo = np.concatenate([-sin, zh, zeros] * reps, axis=1).astype(np.float32)
    sin_hi = np.concatenate([zh, sin, zeros] * reps, axis=1).astype(np.float32)
    return cos_t, sin_lo, sin_hi


def _row(v):
    return v.reshape(1, -1).astype(F32)


def _col(v):
    return v.reshape(-1, 1).astype(F32)


def kernel(x, mix_norm, mlp_norm, w_up, w_down, e_w_in, e_w_out, att_sinks, ml_conv_w, ml_conv_b, ml_i_bias,
           ml_f_bias, ml_norm, o_w_in, o_w_out, ssm_conv_w, ssm_conv_b, ssm_dt_bias, ssm_a_log, ssm_d, ssm_norm,
           rw_mu, rw_w0, rw_w2, rw_a0, rw_a2, rw_g2, rw_k_k, rw_k_a, rw_r_k, rw_ln_w, rw_ln_b, final_norm):
    batch, seq, _ = x.shape
    h = x.reshape(batch * seq, D_MODEL)
    fg = _row(final_norm)

    w = e_w_in[0].astype(BF16)
    o = 0
    parts = []
    for width in (ATT_Q_W, ATT_KV_W, ATT_KV_W, 2 * ML_QK_W, ML_V_W, ML_V_W):
        parts.append(w[:, o:o + width])
        o += width
    parts.append(w[:, o:o + 2 * ML_HEADS].T)
    qa, ka, va, qkm, vm, om, gt = _inproj_even(h, _row(mix_norm[0]), _rope_tables(seq), parts, seq)
    gate_bias = _col(jnp.concatenate([ml_i_bias[0], ml_f_bias[0]]))
    h = _layer0(qa, ka, va, _row(att_sinks[0]), qkm, vm, om, gt, ml_conv_w[0].astype(F32), _row(ml_conv_b[0]),
                gate_bias, _row(ml_norm[0]), h, e_w_out[0].astype(BF16), _row(mlp_norm[0]), w_up[0].astype(BF16),
                w_down[0].astype(BF16), fg, batch, seq)

    w = o_w_in[0].astype(BF16)
    o_x = SSM_INNER
    o_dt = o_x + SSM_CONV_W
    o_rw = o_dt + SSM_HEADS
    parts = [w[:, :o_x], w[:, o_x:o_dt], w[:, o_dt:o_rw].T, w[:, o_rw:]]
    z, xbc, dt_t, rw = _inproj_odd(h, _row(mix_norm[1]), parts, seq)
    rw_consts = (_row(rw_mu[0]), _row(rw_w0[0]), rw_w2[0].astype(BF16), _row(rw_a0[0]), rw_a2[0].astype(BF16),
                 rw_g2[0].astype(BF16), _row(rw_k_k[0]), _row(rw_k_a[0]), _row(rw_r_k[0]), _row(rw_ln_w[0]),
                 _row(rw_ln_b[0]))
    y_rw = _rwkv(rw, rw_consts, batch, seq)
    wo = o_w_out[0].astype(BF16)
    h = _layer1(xbc, z, dt_t, ssm_conv_w[0].astype(F32), _row(ssm_conv_b[0]), _col(ssm_dt_bias[0]),
                _col(ssm_a_log[0]), _row(jnp.repeat(ssm_d[0], SSM_HEAD_DIM)), _row(ssm_norm[0]), y_rw, h,
                wo[:SSM_INNER], wo[SSM_INNER:], _row(mlp_norm[1]), w_up[1].astype(BF16), w_down[1].astype(BF16), fg,
                batch, seq)
    return h.reshape(batch, seq, D_MODEL)
```

```python
import functools

import numpy as np

import jax
import jax.numpy as jnp
from jax import lax
from jax.experimental import pallas as pl
from jax.experimental.pallas import tpu as pltpu

F32 = jnp.float32
BF16 = jnp.bfloat16

D_MODEL = 1024
D_FF = 4 * D_MODEL
NORM_EPS = 1e-5
CONV_TAPS = 4

ATT_HEAD_DIM = 64
ATT_Q_HEADS = 8
ATT_KV_HEADS = 2
ATT_GROUP = ATT_Q_HEADS // ATT_KV_HEADS
ATT_WINDOW = 128
ROPE_THETA = 500000.0
ROPE_DIM = 16
ATT_Q_W = ATT_Q_HEADS * ATT_HEAD_DIM
ATT_KV_W = ATT_KV_HEADS * ATT_HEAD_DIM

ML_HEADS = 4
ML_V_DIM = 128
ML_QK_DIM = 64
ML_CHUNK = 128
ML_NORM_EPS = 1e-6
ML_QK_W = ML_HEADS * ML_QK_DIM
ML_V_W = ML_HEADS * ML_V_DIM

SSM_HEAD_DIM = 64
SSM_HEADS = 8
SSM_GROUPS = 2
SSM_HEADS_PER_GROUP = SSM_HEADS // SSM_GROUPS
SSM_STATE = 128
SSM_CHUNK = 128
SSM_INNER = SSM_HEADS * SSM_HEAD_DIM
SSM_BC = SSM_GROUPS * SSM_STATE
SSM_CONV_W = SSM_INNER + 2 * SSM_BC

RW_HEAD_DIM = 64
RW_HEADS = 8
RW_W = RW_HEADS * RW_HEAD_DIM
RW_DECAY_LORA = 64
RW_AAA_LORA = 64
RW_GATE_LORA = 128
RW_IN = 3 * RW_W + RW_DECAY_LORA + RW_AAA_LORA + RW_GATE_LORA
RW_LN_EPS = 64e-5
RW_CHUNK = 64
RW_EXP_MINUS_HALF = float(np.exp(-0.5))

V7X_VMEM_BYTES = 64 * 1024 * 1024
VMEM_LIMIT_BYTES = V7X_VMEM_BYTES - 6 * 1024 * 1024
SUBLANES = 8
LANES = 128

PROJ_ROWS = 1024
FF_CHUNK = 512


def _params(*sem):
    return pltpu.CompilerParams(dimension_semantics=sem, vmem_limit_bytes=VMEM_LIMIT_BYTES)


def _const_spec(shape):
    nd = len(shape)
    return pl.BlockSpec(shape, lambda *_: (0,) * nd, pipeline_mode=pl.Buffered(1))


def _weight_spec(a, layer):
    if a.ndim == 2:
        return _const_spec(a.shape)
    return pl.BlockSpec((None,) + tuple(a.shape[1:]), lambda *_: (layer, 0, 0), pipeline_mode=pl.Buffered(1))


def _iota(shape, dim):
    return lax.broadcasted_iota(jnp.int32, shape, dim)


def _tri_lower(n):
    return (_iota((n, n), 1) <= _iota((n, n), 0)).astype(F32)


def _tri_upper(n):
    return (_iota((n, n), 0) <= _iota((n, n), 1)).astype(F32)


def _eye(n):
    return (_iota((n, n), 0) == _iota((n, n), 1)).astype(F32)


def _dot(a, b, precision=None):
    return jnp.dot(a, b, preferred_element_type=F32, precision=precision)


def _dot_nt(a, b, precision=None):
    return lax.dot_general(a, b, (((1,), (1,)), ((), ())), preferred_element_type=F32, precision=precision)


def _dot_tn(a, b):
    return lax.dot_general(a, b, (((0,), (1,)), ((), ())), preferred_element_type=F32)


def _split3_bf16(x):
    hi = x.astype(BF16)
    r1 = x - hi.astype(F32)
    mid = r1.astype(BF16)
    lo = (r1 - mid.astype(F32)).astype(BF16)
    return hi, mid, lo


def _sum3(terms):
    hi, mid, lo = terms
    return hi + (mid + lo)


def _sigmoid(x):
    return 1.0 / (1.0 + jnp.exp(-x))


def _softplus(x):
    return jnp.maximum(x, 0.0) + jnp.log1p(jnp.exp(-jnp.abs(x)))


def _silu(x):
    return x * _sigmoid(x)


def _rmsnorm(x, g):
    y = x * lax.rsqrt(jnp.mean(x * x, axis=-1, keepdims=True) + NORM_EPS)
    return y * g


def _shifted_rows(x, tail, k):
    rolled = pltpu.roll(x, k, axis=0)
    tail_rolled = pltpu.roll(tail, k, axis=0)
    first = jnp.where(_iota((SUBLANES, 1), 0) < k, tail_rolled, rolled[:SUBLANES])
    return jnp.concatenate([first, rolled[SUBLANES:]], axis=0)


def _causal_conv_silu(x, tail, w, b):
    out = b + x * w[CONV_TAPS - 1:CONV_TAPS, :]
    for k in range(1, CONV_TAPS):
        out = out + _shifted_rows(x, tail, k) * w[CONV_TAPS - 1 - k:CONV_TAPS - k, :]
    return _silu(out)


def _rope(x, cos, sin_lo, sin_hi, reps):
    w = x.shape[1]
    c = jnp.concatenate([cos] * reps, axis=1)
    s_lo = jnp.concatenate([sin_lo] * reps, axis=1)
    s_hi = jnp.concatenate([sin_hi] * reps, axis=1)
    half = ROPE_DIM // 2
    nxt = pltpu.roll(x, w - half, axis=1)
    prv = pltpu.roll(x, half, axis=1)
    return x * c + nxt * s_lo + prv * s_hi


EVEN_SEGMENTS = (ATT_Q_W, ATT_KV_W, ATT_KV_W, 2 * ML_QK_W, ML_V_W, ML_V_W, 2 * ML_HEADS)
EVEN_OFFSETS = tuple(sum(EVEN_SEGMENTS[:i]) for i in range(len(EVEN_SEGMENTS)))


def _inproj_even_kernel(x_ref, g_ref, cos_ref, slo_ref, shi_ref, w_ref,
                        qa_ref, ka_ref, va_ref, qkm_ref, vm_ref, om_ref, gt_ref):
    hb = _rmsnorm(x_ref[...], g_ref[...]).astype(BF16)
    cos, s_lo, s_hi = cos_ref[...], slo_ref[...], shi_ref[...]
    seg = lambda i: w_ref[:, EVEN_OFFSETS[i]:EVEN_OFFSETS[i] + EVEN_SEGMENTS[i]]
    q = _rope(_dot(hb, seg(0)), cos, s_lo, s_hi, ATT_Q_W // LANES)
    qa_ref[...] = (q * (ATT_HEAD_DIM ** -0.5)).astype(BF16)
    ka_ref[...] = _rope(_dot(hb, seg(1)), cos, s_lo, s_hi, ATT_KV_W // LANES).astype(BF16)
    va_ref[...] = _dot(hb, seg(2)).astype(BF16)
    qkm_ref[...] = _dot(hb, seg(3))
    vm_ref[...] = _dot(hb, seg(4)).astype(BF16)
    om_ref[...] = _dot(hb, seg(5))
    gt_ref[...] = _dot_tn(seg(6), hb)


def _inproj_even(x2, g, tables, w, seq):
    n = x2.shape[0]
    tm = min(PROJ_ROWS, seq)
    per_seq = seq // tm
    row = lambda width: pl.BlockSpec((tm, width), lambda i: (i, 0))
    tab = pl.BlockSpec((tm, LANES), lambda i: (i % per_seq, 0))
    out_shapes = (
        jax.ShapeDtypeStruct((n, ATT_Q_W), BF16), jax.ShapeDtypeStruct((n, ATT_KV_W), BF16),
        jax.ShapeDtypeStruct((n, ATT_KV_W), BF16), jax.ShapeDtypeStruct((n, 2 * ML_QK_W), F32),
        jax.ShapeDtypeStruct((n, ML_V_W), BF16), jax.ShapeDtypeStruct((n, ML_V_W), F32),
        jax.ShapeDtypeStruct((n // seq, 2 * ML_HEADS, seq), F32))
    return pl.pallas_call(
        _inproj_even_kernel,
        grid=(n // tm,),
        in_specs=[row(D_MODEL), _const_spec((1, D_MODEL)), tab, tab, tab, _const_spec(w.shape)],
        out_specs=(row(ATT_Q_W), row(ATT_KV_W), row(ATT_KV_W), row(2 * ML_QK_W), row(ML_V_W), row(ML_V_W),
                   pl.BlockSpec((None, 2 * ML_HEADS, tm), lambda i: (i // per_seq, 0, i % per_seq))),
        out_shape=out_shapes,
        compiler_params=_params("arbitrary"),
        name="inproj_even",
    )(x2, g, *tables, w)


def _attention_stages(q_ref, kc_ref, kp_ref, vc_ref, vp_ref, sink_ref, blk, nb, emit):
    w = ATT_WINDOW
    dh = ATT_HEAD_DIM
    G = ATT_GROUP
    i = _iota((G * w, 2 * w), 0) % w
    j = _iota((G * w, 2 * w), 1)
    mask = (j > i) & (j <= i + w) & ((j >= w) | (blk > 0))
    sinks = sink_ref[...]
    problems = [(b, g) for b in range(nb) for g in range(ATT_KV_HEADS)]
    scores, values, sink_cols = [], [], []
    for b, g in problems:
        q = q_ref[b]
        qs = jnp.concatenate([q[:, (g * G + a) * dh:(g * G + a + 1) * dh] for a in range(G)], axis=0)
        ks = jnp.concatenate([kp_ref[b][:, g * dh:(g + 1) * dh], kc_ref[b][:, g * dh:(g + 1) * dh]], axis=0)
        values.append(jnp.concatenate([vp_ref[b][:, g * dh:(g + 1) * dh], vc_ref[b][:, g * dh:(g + 1) * dh]], axis=0))
        scores.append(_dot_nt(qs, ks))
        sink_cols.append(jnp.concatenate(
            [jnp.broadcast_to(sinks[:, g * G + a:g * G + a + 1], (w, 1)) for a in range(G)], axis=0))
    yield
    probs, dens = [], []
    for s, sink in zip(scores, sink_cols):
        s = jnp.where(mask, s, -jnp.inf)
        m = jnp.maximum(jnp.max(s, axis=-1, keepdims=True), sink)
        p = jnp.exp(s - m)
        dens.append(jnp.sum(p, axis=-1, keepdims=True) + jnp.exp(sink - m))
        probs.append(p.astype(BF16))
    pv = [_dot(p, v) for p, v in zip(probs, values)]
    yield
    outs = [o / d for o, d in zip(pv, dens)]
    for b in range(nb):
        heads = [outs[b * ATT_KV_HEADS + g][a * w:(a + 1) * w] for g in range(ATT_KV_HEADS) for a in range(G)]
        emit(b, jnp.concatenate(heads, axis=1).astype(BF16))


def _mlstm_stages(qk_ref, tail_ref, v_ref, o_ref, gt_ref, cw_ref, cb_ref, gb_ref, nw_ref, c_s, m_s, chunk, nb, emit):
    L = ML_CHUNK
    H = ML_HEADS
    DV = ML_V_DIM
    NG = 2 * H

    cw, cb = cw_ref[...], cb_ref[...]
    qk = [_causal_conv_silu(qk_ref[b], jnp.where(chunk == 0, 0.0, tail_ref[b]), cw, cb) for b in range(nb)]

    g = jnp.concatenate([gt_ref[b] + gb_ref[...] for b in range(nb)], axis=0)
    lg = jnp.where(_iota(g.shape, 0) % NG < H, g, -_softplus(-g))
    tri_l, tri_u, eye = _tri_lower(L).astype(BF16), _tri_upper(L).astype(BF16), _eye(L).astype(BF16)
    lg3 = _split3_bf16(lg)
    cum_row = _sum3([_dot(p, tri_u) for p in lg3])
    lg_col = _sum3([_dot_nt(eye, p) for p in lg3])
    cum_col = _sum3([_dot_nt(tri_l, p) for p in lg3])
    causal = _iota((L, L), 1) <= _iota((L, L), 0)

    chains = [(b, h) for b in range(nb) for h in range(H)]
    idx = lambda b, h: b * H + h
    q_c = [qk[b][:, h * ML_QK_DIM:(h + 1) * ML_QK_DIM] for b, h in chains]
    k_c = [qk[b][:, ML_QK_W + h * ML_QK_DIM:ML_QK_W + (h + 1) * ML_QK_DIM] * (ML_QK_DIM ** -0.5) for b, h in chains]
    ones = jnp.ones((L, DV), BF16)
    v_c = [jnp.concatenate([v_ref[b][:, h * DV:(h + 1) * DV], ones], axis=1) for b, h in chains]
    qb_c = [q.astype(BF16) for q in q_c]
    c_prev = [c_s[idx(b, h)] for b, h in chains]
    m_prev = [m_s[idx(b, h):idx(b, h) + 1, 0:1] for b, h in chains]

    s_qk = [_dot_nt(qb, k.astype(BF16)) for qb, k in zip(qb_c, k_c)]
    inter = [_dot(qb, c.astype(BF16)) for qb, c in zip(qb_c, c_prev)]

    yield
    gi_c = [b * NG + h for b, h in chains]
    b_col = [cum_col[:, gi + H:gi + H + 1] for gi in gi_c]
    b_last = [bc[L - 1:L, :] for bc in b_col]
    dlog = [jnp.where(causal, bc - cum_row[gi + H:gi + H + 1, :] + lg[gi:gi + 1, :], -jnp.inf)
            for gi, bc in zip(gi_c, b_col)]
    row_max = [jnp.max(d, axis=1, keepdims=True) for d in dlog]
    a_loc = [bl - bc + lg_col[:, gi:gi + 1] for gi, bc, bl in zip(gi_c, b_col, b_last)]
    m_loc = [jnp.max(a, axis=0, keepdims=True) for a in a_loc]
    inter_log = [bc + mp for bc, mp in zip(b_col, m_prev)]
    m_t = [jnp.maximum(il, rm) for il, rm in zip(inter_log, row_max)]
    qk_w = [s * jnp.exp(d - mt) for s, d, mt in zip(s_qk, dlog, m_t)]
    w_inter = [jnp.exp(il - mt) for il, mt in zip(inter_log, m_t)]
    kw = [k * jnp.exp(a - ml) for k, a, ml in zip(k_c, a_loc, m_loc)]

    intra = [_dot(w.astype(BF16), v) for w, v in zip(qk_w, v_c)]
    c_loc = [lax.dot_general(w.astype(BF16), v, (((0,), (0,)), ((), ())), preferred_element_type=F32)
             for w, v in zip(kw, v_c)]

    yield
    both = [a + w * e for a, w, e in zip(intra, w_inter, inter)]
    xs = [nd[:, :DV] / jnp.maximum(jnp.abs(nd[:, DV:DV + 1]), jnp.exp(-mt)) for nd, mt in zip(both, m_t)]
    mus = [jnp.mean(x, axis=-1, keepdims=True) for x in xs]
    cen = [x - mu for x, mu in zip(xs, mus)]
    var = [jnp.mean(jnp.square(c), axis=-1, keepdims=True) for c in cen]
    outs = [c * lax.rsqrt(vr + ML_NORM_EPS) for c, vr in zip(cen, var)]
    for i, (b, h) in enumerate(chains):
        m_new = jnp.maximum(b_last[i] + m_prev[i], m_loc[i])
        fa = jnp.exp(b_last[i] + m_prev[i] - m_new)
        fb = jnp.exp(m_loc[i] - m_new)
        c_s[idx(b, h)] = fa * c_prev[i] + fb * c_loc[i]
        m_s[idx(b, h):idx(b, h) + 1, 0:1] = m_new

    nw = nw_ref[...]
    for b in range(nb):
        hm = jnp.concatenate(outs[b * H:(b + 1) * H], axis=1) * nw
        emit(b, (hm * _sigmoid(o_ref[b])).astype(BF16))


def _outproj_mlp_stages(y_parts, h, g_ref, wu_ref, wd_ref, fg_ref, final_norm, emit):
    h1 = h
    for y, w_ref in y_parts:
        h1 = h1 + _dot(y, w_ref[...])
    hb = _rmsnorm(h1, g_ref[...]).astype(BF16)
    n_chunks = D_FF // FF_CHUNK
    mlp = None
    for c in range(n_chunks):
        u = _dot(hb, wu_ref[:, c * FF_CHUNK:(c + 1) * FF_CHUNK])
        a = jnp.square(jnp.maximum(u, 0.0)).astype(BF16)
        d = _dot(a, wd_ref[c * FF_CHUNK:(c + 1) * FF_CHUNK, :])
        mlp = d if mlp is None else mlp + d
        if c + 1 in MLP_SPLIT:
            yield
    h2 = h1 + mlp
    if final_norm:
        h2 = _rmsnorm(h2, fg_ref[...])
    emit(h2)


MLP_SPLIT = (5,)


def _drive(mixers, mlp):
    for m in mixers:
        next(m)
    next(mlp)
    for m in mixers:
        next(m)
    for m in mixers:
        for _ in m:
            pass
    for _ in mlp:
        pass


def _layer0_kernel(q_ref, kc_ref, kp_ref, vc_ref, vp_ref, sink_ref,
                   qk_ref, tail_ref, vm_ref, om_ref, gt_ref, cw_ref, cb_ref, gb_ref, nw_ref,
                   h_ref, wo_ref, g_ref, wu_ref, wd_ref, fg_ref,
                   o_ref, c_s, m_s, y_s, *, nb, nc):
    step = pl.program_id(0)
    chunk = jnp.minimum(step, nc - 1)
    L = ML_CHUNK

    @pl.when(step == 0)
    def _():
        c_s[...] = jnp.zeros_like(c_s)
        m_s[...] = jnp.zeros_like(m_s)
        y_s[...] = jnp.zeros_like(y_s)

    wr = step % 2
    rd = 1 - wr

    def emit_att(b, block):
        y_s[wr, b * L:(b + 1) * L, 0:ATT_Q_W] = block

    def emit_ml(b, block):
        y_s[wr, b * L:(b + 1) * L, ATT_Q_W:ATT_Q_W + ML_V_W] = block

    def emit_out(block):
        o_ref[...] = block.reshape(nb, L, D_MODEL)

    mixers = [
        _attention_stages(q_ref, kc_ref, kp_ref, vc_ref, vp_ref, sink_ref, chunk, nb, emit_att),
        _mlstm_stages(qk_ref, tail_ref, vm_ref, om_ref, gt_ref, cw_ref, cb_ref, gb_ref, nw_ref, c_s, m_s, chunk, nb,
                      emit_ml),
    ]
    mlp = _outproj_mlp_stages([(y_s[rd], wo_ref)], h_ref[...].reshape(nb * L, D_MODEL), g_ref, wu_ref, wd_ref,
                              fg_ref, False, emit_out)
    _drive(mixers, mlp)


def _layer0(qa, ka, va, sinks, qkm, vm, om, gt, conv_w, conv_b, gate_bias, norm_w, h, wo, g, wu, wd, fg, batch, seq):
    L = ML_CHUNK
    nc = seq // L
    per8 = L // SUBLANES
    three = lambda t: t.reshape(batch, seq, t.shape[-1])
    mix = lambda c: jnp.minimum(c, nc - 1)
    cur = lambda width: pl.BlockSpec((batch, L, width), lambda c: (0, mix(c), 0))
    prev = lambda width: pl.BlockSpec((batch, L, width), lambda c: (0, jnp.maximum(mix(c) - 1, 0), 0))
    tail = pl.BlockSpec((batch, SUBLANES, 2 * ML_QK_W), lambda c: (0, jnp.maximum(mix(c) * per8 - 1, 0), 0))
    late = pl.BlockSpec((batch, L, D_MODEL), lambda c: (0, jnp.maximum(c - 1, 0), 0))
    q3, k3, v3, qk3 = three(qa), three(ka), three(va), three(qkm)
    consts = (conv_w, conv_b, gate_bias, norm_w)
    weights = (wo, g, wu, wd, fg)
    out = pl.pallas_call(
        functools.partial(_layer0_kernel, nb=batch, nc=nc),
        grid=(nc + 1,),
        in_specs=[cur(ATT_Q_W), cur(ATT_KV_W), prev(ATT_KV_W), cur(ATT_KV_W), prev(ATT_KV_W),
                  _const_spec((1, ATT_Q_HEADS)),
                  cur(2 * ML_QK_W), tail, cur(ML_V_W), cur(ML_V_W),
                  pl.BlockSpec((batch, 2 * ML_HEADS, L), lambda c: (0, 0, mix(c)))]
                 + [_const_spec(a.shape) for a in consts] + [late] + [_weight_spec(a, 0) for a in weights],
        out_specs=late,
        out_shape=jax.ShapeDtypeStruct((batch, seq, D_MODEL), F32),
        scratch_shapes=[pltpu.VMEM((batch * ML_HEADS, ML_QK_DIM, 2 * ML_V_DIM), F32),
                        pltpu.VMEM((batch * ML_HEADS, LANES), F32),
                        pltpu.VMEM((2, batch * L, ATT_Q_W + ML_V_W), BF16)],
        compiler_params=_params("arbitrary"),
        name="layer0_mix_mlp",
    )(q3, k3, k3, v3, v3, sinks, qk3, qk3, three(vm), three(om), gt, *consts, three(h), *weights)
    return out.reshape(batch * seq, D_MODEL)


def _inproj_odd_kernel(x_ref, g_ref, wz_ref, wx_ref, wdt_ref, wrw_ref, z_ref, xbc_ref, dt_ref, rw_ref):
    hb = _rmsnorm(x_ref[...], g_ref[...]).astype(BF16)
    z_ref[...] = _dot(hb, wz_ref[...])
    xbc_ref[...] = _dot(hb, wx_ref[...])
    dt_ref[...] = _dot_tn(wdt_ref[...], hb)
    rw_ref[...] = _dot(hb, wrw_ref[...])


def _inproj_odd(x2, g, w, seq):
    n = x2.shape[0]
    tm = min(PROJ_ROWS, seq)
    per_seq = seq // tm
    row = lambda width: pl.BlockSpec((tm, width), lambda i: (i, 0))
    out_shapes = (jax.ShapeDtypeStruct((n, SSM_INNER), F32), jax.ShapeDtypeStruct((n, SSM_CONV_W), F32),
                  jax.ShapeDtypeStruct((n // seq, SSM_HEADS, seq), F32), jax.ShapeDtypeStruct((n, RW_IN), F32))
    return pl.pallas_call(
        _inproj_odd_kernel,
        grid=(n // tm,),
        in_specs=[row(D_MODEL), _const_spec((1, D_MODEL))] + [_const_spec(a.shape) for a in w],
        out_specs=(row(SSM_INNER), row(SSM_CONV_W),
                   pl.BlockSpec((None, SSM_HEADS, tm), lambda i: (i // per_seq, 0, i % per_seq)), row(RW_IN)),
        out_shape=out_shapes,
        compiler_params=_params("arbitrary"),
        name="inproj_odd",
    )(x2, g, *w)


def _ssd_stages(xbc_ref, tail_ref, z_ref, dt_ref, cw_ref, cb_ref, dtb_ref, alog_ref, d_ref, nw_ref, s_s, chunk, nb,
                emit):
    L = SSM_CHUNK
    P = SSM_HEAD_DIM
    HG = SSM_HEADS_PER_GROUP
    NH = SSM_HEADS

    cw, cb_ = cw_ref[...], cb_ref[...]
    xbc = [_causal_conv_silu(xbc_ref[b], jnp.where(chunk == 0, 0.0, tail_ref[b]), cw, cb_) for b in range(nb)]

    dt_row = _softplus(jnp.concatenate([dt_ref[b] + dtb_ref[...] for b in range(nb)], axis=0))
    neg_a = -jnp.exp(jnp.concatenate([alog_ref[...]] * nb, axis=0))
    a_row = dt_row * neg_a
    tri_l, tri_u, eye_b = _tri_lower(L).astype(BF16), _tri_upper(L).astype(BF16), _eye(L).astype(BF16)
    a3 = _split3_bf16(a_row)
    acum_row = _sum3([_dot(p, tri_u) for p in a3])
    acum_col = _sum3([_dot_nt(tri_l, p) for p in a3])
    dt_col = _sum3([_dot_nt(eye_b, p) for p in _split3_bf16(dt_row)])
    causal = _iota((L, L), 1) <= _iota((L, L), 0)

    groups = [(b, g) for b in range(nb) for g in range(SSM_GROUPS)]
    gidx = lambda b, g: b * SSM_GROUPS + g
    bg = [xbc[b][:, SSM_INNER + g * SSM_STATE:SSM_INNER + (g + 1) * SSM_STATE].astype(BF16) for b, g in groups]
    cg = [xbc[b][:, SSM_INNER + SSM_BC + g * SSM_STATE:SSM_INNER + SSM_BC + (g + 1) * SSM_STATE].astype(BF16)
          for b, g in groups]
    s_prev = [s_s[gidx(b, g)] for b, g in groups]

    cb = [_dot_nt(c, bq) for c, bq in zip(cg, bg)]
    bg_t = [_dot_nt(eye_b, bq).astype(BF16) for bq in bg]
    y_off = [_dot(c, sp.astype(BF16)) for c, sp in zip(cg, s_prev)]

    yield
    heads = [(b, g, j) for b, g in groups for j in range(HG)]
    scores, xcs, e_col, xws, cds = [], [], [], [], []
    for b, g, j in heads:
        h = g * HG + j
        col = b * NH + h
        a_col = acum_col[:, col:col + 1]
        a_last = a_col[L - 1:L, :]
        decay = jnp.exp(jnp.where(causal, a_col - acum_row[col:col + 1, :], -jnp.inf))
        xc = xbc[b][:, h * P:(h + 1) * P] * dt_col[:, col:col + 1]
        scores.append((cb[gidx(b, g)] * decay).astype(BF16))
        xcs.append(xc.astype(BF16))
        e_col.append(jnp.exp(a_col))
        xws.append(xc * jnp.exp(a_last - a_col))
        cds.append(jnp.broadcast_to(jnp.exp(a_last), (1, P)))

    y_diag = [_dot(sc, xc) for sc, xc in zip(scores, xcs)]
    states = [_dot(bg_t[i], jnp.concatenate(xws[i * HG:(i + 1) * HG], axis=1).astype(BF16))
              for i in range(len(groups))]

    yield
    for i, (b, g) in enumerate(groups):
        s_s[gidx(b, g)] = s_prev[i] * jnp.concatenate(cds[i * HG:(i + 1) * HG], axis=1) + states[i]

    gw = SSM_INNER // SSM_GROUPS
    for b in range(nb):
        ys = []
        for g in range(SSM_GROUPS):
            for j in range(HG):
                i = (b * SSM_GROUPS + g) * HG + j
                ys.append(y_diag[i] + y_off[gidx(b, g)][:, j * P:(j + 1) * P] * e_col[i])
        xs = xbc[b][:, :SSM_INNER]
        y = jnp.concatenate(ys, axis=1) + d_ref[...] * xs
        y = y * _silu(z_ref[b])
        normed = []
        for g in range(SSM_GROUPS):
            yg = y[:, g * gw:(g + 1) * gw]
            normed.append(yg * lax.rsqrt(jnp.mean(yg * yg, axis=-1, keepdims=True) + NORM_EPS))
        emit(b, (jnp.concatenate(normed, axis=1) * nw_ref[...]).astype(BF16))


def _layer1_kernel(xbc_ref, tail_ref, z_ref, dt_ref, cw_ref, cb_ref, dtb_ref, alog_ref, d_ref, nw_ref,
                   yrw_ref, h_ref, wa_ref, wb_ref, g_ref, wu_ref, wd_ref, fg_ref,
                   o_ref, s_s, y_s, *, nb, nc):
    step = pl.program_id(0)
    chunk = jnp.minimum(step, nc - 1)
    L = SSM_CHUNK

    @pl.when(step == 0)
    def _():
        s_s[...] = jnp.zeros_like(s_s)
        y_s[...] = jnp.zeros_like(y_s)

    wr = step % 2
    rd = 1 - wr

    def emit_ssm(b, block):
        y_s[wr, b * L:(b + 1) * L, :] = block

    def emit_out(block):
        o_ref[...] = block.reshape(nb, L, D_MODEL)

    mixers = [_ssd_stages(xbc_ref, tail_ref, z_ref, dt_ref, cw_ref, cb_ref, dtb_ref, alog_ref, d_ref, nw_ref, s_s,
                          chunk, nb, emit_ssm)]
    mlp = _outproj_mlp_stages([(y_s[rd], wa_ref), (yrw_ref[...].reshape(nb * L, RW_W), wb_ref)],
                              h_ref[...].reshape(nb * L, D_MODEL), g_ref, wu_ref, wd_ref, fg_ref, True, emit_out)
    _drive(mixers, mlp)


def _layer1(xbc, z, dt_t, conv_w, conv_b, dt_bias, a_log, d_row, norm_w, y_rw, h, wa, wb, g, wu, wd, fg, batch, seq):
    L = SSM_CHUNK
    nc = seq // L
    per8 = L // SUBLANES
    three = lambda t: t.reshape(batch, seq, t.shape[-1])
    mix = lambda c: jnp.minimum(c, nc - 1)
    cur = lambda width: pl.BlockSpec((batch, L, width), lambda c: (0, mix(c), 0))
    tail = pl.BlockSpec((batch, SUBLANES, SSM_CONV_W), lambda c: (0, jnp.maximum(mix(c) * per8 - 1, 0), 0))
    late = lambda width: pl.BlockSpec((batch, L, width), lambda c: (0, jnp.maximum(c - 1, 0), 0))
    consts = (conv_w, conv_b, dt_bias, a_log, d_row, norm_w)
    weights = (wa, wb, g, wu, wd, fg)
    x3 = three(xbc)
    out = pl.pallas_call(
        functools.partial(_layer1_kernel, nb=batch, nc=nc),
        grid=(nc + 1,),
        in_specs=[cur(SSM_CONV_W), tail, cur(SSM_INNER),
                  pl.BlockSpec((batch, SSM_HEADS, L), lambda c: (0, 0, mix(c)))]
                 + [_const_spec(a.shape) for a in consts] + [late(RW_W), late(D_MODEL)]
                 + [_weight_spec(a, 1) for a in weights],
        out_specs=late(D_MODEL),
        out_shape=jax.ShapeDtypeStruct((batch, seq, D_MODEL), F32),
        scratch_shapes=[pltpu.VMEM((batch * SSM_GROUPS, SSM_STATE, SSM_HEADS_PER_GROUP * SSM_HEAD_DIM), F32),
                        pltpu.VMEM((2, batch * L, SSM_INNER), BF16)],
        compiler_params=_params("arbitrary"),
        name="layer1_mix_mlp",
    )(x3, x3, three(z), dt_t, *consts, three(y_rw), three(h), *weights)
    return out.reshape(batch * seq, D_MODEL)


def _split_bf16(x):
    hi = x.astype(BF16)
    lo = (x - hi.astype(F32)).astype(BF16)
    return hi, lo


def _dot_ones_exact(t, ones_matrix):
    hi, lo = _split_bf16(t)
    return _dot(hi, ones_matrix) + _dot(lo, ones_matrix)


def _unit_lower_inverse_steps(mats, eye, n):
    xbs = [(eye - a).astype(BF16) for a in mats]
    mats = [a.astype(BF16) for a in mats]
    terms = 2
    while terms < n:
        terms *= 2
        xfs = [xb.astype(F32) for xb in xbs]
        resids = [(eye - xf - _dot(a, xb)).astype(BF16) for a, xb, xf in zip(mats, xbs, xfs)]
        xbs = [(xf + _dot(xb, e)).astype(BF16) for xb, xf, e in zip(xbs, xfs, resids)]
        yield xbs


RW_FRONT_STAGES = 4
RW_CHAIN_STAGES = 10


def _rwkv_stages(x, before, mu_ref, w0_ref, w2_ref, a0_ref, a2_ref, g2_ref, kk_ref, ka_ref, rk_ref, lnw_ref, lnb_ref,
                 st_s, nb, emit):
    C = RW_CHUNK
    D = RW_HEAD_DIM
    R = nb * C
    rolled = pltpu.roll(x, 1, axis=0)
    first_row = _iota((C, 1), 0) == 0
    prev = jnp.concatenate([jnp.where(first_row, before[b], rolled[b * C:(b + 1) * C]) for b in range(nb)], axis=0)
    xm = x + (prev - x) * mu_ref[...]
    r = xm[:, 0:RW_W]
    k = xm[:, RW_W:2 * RW_W]
    v = xm[:, 2 * RW_W:3 * RW_W]
    o = 3 * RW_W
    wd = xm[:, o:o + RW_DECAY_LORA]
    ad = xm[:, o + RW_DECAY_LORA:o + RW_DECAY_LORA + RW_AAA_LORA]
    gd = xm[:, o + RW_DECAY_LORA + RW_AAA_LORA:]
    yield

    z = w0_ref[...] + _dot(jnp.tanh(wd).astype(BF16), w2_ref[...])
    ld = -RW_EXP_MINUS_HALF * _sigmoid(z)
    a = _sigmoid(a0_ref[...] + _dot(ad.astype(BF16), a2_ref[...]))
    gate = _dot(_sigmoid(gd).astype(BF16), g2_ref[...])
    yield

    half = RW_W // 2
    same_head = (_iota((half, half), 0) // D == _iota((half, half), 1) // D).astype(BF16)
    head_sum = lambda t: jnp.concatenate(
        [_dot_ones_exact(t[:, :half], same_head), _dot_ones_exact(t[:, half:], same_head)], axis=1)

    kk = k * kk_ref[...]
    kk = kk * lax.rsqrt(jnp.maximum(head_sum(kk * kk), 1e-24))
    k2 = k * (1.0 + (a - 1.0) * ka_ref[...])
    bvec = kk * a
    yield

    tri = ((_iota((R, R), 1) <= _iota((R, R), 0)) & (_iota((R, R), 1) // C == _iota((R, R), 0) // C)).astype(BF16)
    cl = _sum3([_dot(tri, p) for p in _split3_bf16(ld)])
    c_last = jnp.concatenate(
        [jnp.broadcast_to(cl[(b + 1) * C - 1:(b + 1) * C, :], (C, RW_W)) for b in range(nb)], axis=0)
    e_neg = jnp.exp(-cl)
    e_end = jnp.exp(c_last - cl)
    w_end = [jnp.exp(cl[(b + 1) * C - 1:(b + 1) * C, :]) for b in range(nb)]
    rw_ = r * jnp.exp(cl)
    kkw = kk * jnp.exp(cl - ld)
    bd = bvec * e_neg
    kd = k2 * e_neg
    bd_end = bvec * e_end
    kd_end = k2 * e_end
    yield

    g_row = _iota((2 * C, 2 * C), 0)
    g_col = _iota((2 * C, 2 * C), 1) % C
    keep = g_col < jnp.where(g_row < C, g_row, g_row - C + 1)
    eye = _eye(C)
    chains = [(b, h) for b in range(nb) for h in range(RW_HEADS)]
    blk = lambda t, b, h: t[b * C:(b + 1) * C, h * D:(h + 1) * D]
    kkw_c = [blk(kkw, b, h).astype(BF16) for b, h in chains]
    rw_c = [blk(rw_, b, h).astype(BF16) for b, h in chains]
    vb_c = [blk(v, b, h).astype(BF16) for b, h in chains]
    gms = [jnp.where(keep, _dot_nt(jnp.concatenate([kq, rq], axis=0),
                                   jnp.concatenate([blk(bd, b, h), blk(kd, b, h)], axis=0).astype(BF16)), 0.0)
           for (b, h), kq, rq in zip(chains, kkw_c, rw_c)]
    yield
    a_v = [_dot(g[:, C:].astype(BF16), vb) for g, vb in zip(gms, vb_c)]
    a_rb = [g[C:, :C].astype(BF16) for g in gms]
    yield
    invs = None
    for invs in _unit_lower_inverse_steps([g[:C, :C] for g in gms], eye, C):
        yield
    tws = [_dot(inv, jnp.concatenate([av[:C].astype(BF16), kq], axis=1))
           for inv, av, kq in zip(invs, a_v, kkw_c)]
    yield
    sts = [st_s[b * RW_HEADS + h] for b, h in chains]
    xss = [_dot_nt(jnp.concatenate([tw[:, D:].astype(BF16), rq], axis=0), st.astype(BF16))
           for tw, rq, st in zip(tws, rw_c, sts)]
    yield
    ubs =[(-tw[:, :D] - xs[:C]).astype(BF16) for tw, xs in zip(tws, xss)]
    ys = [xs[C:] + _dot(m, ub) + av[C:] for xs, m, ub, av in zip(xss, a_rb, ubs, a_v)]
    for (b, h), st, ub, vb in zip(chains, sts, ubs, vb_c):
        ends = jnp.concatenate([blk(bd_end, b, h), blk(kd_end, b, h)], axis=0).astype(BF16)
        upd = lax.dot_general(jnp.concatenate([ub, vb], axis=0), ends, (((0,), (0,)), ((), ())),
                              preferred_element_type=F32)
        st_s[b * RW_HEADS + h] = st * w_end[b][:, h * D:(h + 1) * D] + upd
    yield

    y = jnp.concatenate([jnp.concatenate(ys[b * RW_HEADS:(b + 1) * RW_HEADS], axis=1) for b in range(nb)], axis=0)
    mu = head_sum(y) * (1.0 / D)
    var = head_sum(jnp.square(y - mu)) * (1.0 / D)
    yn = (y - mu) * lax.rsqrt(var + RW_LN_EPS) * lnw_ref[...] + lnb_ref[...]
    bonus = head_sum(r * k2 * rk_ref[...]) * v
    emit(((yn + bonus) * gate).astype(BF16))


RW_STEP_CHUNKS = 2


def _rwkv_kernel(rw_ref, tail_ref, mu_ref, w0_ref, w2_ref, a0_ref, a2_ref, g2_ref, kk_ref, ka_ref, rk_ref,
                 lnw_ref, lnb_ref, y_ref, st_s, *, nb):
    step = pl.program_id(0)
    C = RW_CHUNK

    @pl.when(step == 0)
    def _():
        st_s[...] = jnp.zeros_like(st_s)

    xs = [rw_ref[:, j * C:(j + 1) * C, :].reshape(nb * C, RW_IN) for j in range(RW_STEP_CHUNKS)]
    first = [jnp.where(step == 0, 0.0, tail_ref[b, SUBLANES - 1:SUBLANES, :]) for b in range(nb)]
    second = [xs[0][(b + 1) * C - 1:(b + 1) * C, :] for b in range(nb)]

    def emit(j):
        def write(block):
            y_ref[:, j * C:(j + 1) * C, :] = block.reshape(nb, C, RW_W)
        return write

    args = (mu_ref, w0_ref, w2_ref, a0_ref, a2_ref, g2_ref, kk_ref, ka_ref, rk_ref, lnw_ref, lnb_ref, st_s, nb)
    g0 = _rwkv_stages(xs[0], first, *args, emit(0))
    g1 = _rwkv_stages(xs[1], second, *args, emit(1))
    for _ in range(RW_FRONT_STAGES):
        next(g0)
    for _ in range(RW_FRONT_STAGES):
        next(g1)
    for _ in range(RW_CHAIN_STAGES):
        next(g0)
    next(g1)
    for _ in g0:
        pass
    for _ in g1:
        pass


def _rwkv(rw, consts, batch, seq):
    C = RW_CHUNK * RW_STEP_CHUNKS
    nc = seq // C
    per8 = C // SUBLANES
    rw3 = rw.reshape(batch, seq, RW_IN)
    y = pl.pallas_call(
        functools.partial(_rwkv_kernel, nb=batch),
        grid=(nc,),
        in_specs=[pl.BlockSpec((batch, C, RW_IN), lambda c: (0, c, 0)),
                  pl.BlockSpec((batch, SUBLANES, RW_IN), lambda c: (0, jnp.maximum(c * per8 - 1, 0), 0))]
                 + [_const_spec(a.shape) for a in consts],
        out_specs=pl.BlockSpec((batch, C, RW_W), lambda c: (0, c, 0)),
        out_shape=jax.ShapeDtypeStruct((batch, seq, RW_W), BF16),
        scratch_shapes=[pltpu.VMEM((batch * RW_HEADS, RW_HEAD_DIM, RW_HEAD_DIM), F32)],
        compiler_params=_params("arbitrary"),
        name="rwkv7",
    )(rw3, rw3, *consts)
    return y.reshape(batch * seq, RW_W)


@functools.lru_cache(maxsize=None)
def _rope_tables(seq):
    half = ROPE_DIM // 2
    inv_freq = ROPE_THETA ** (-np.arange(half, dtype=np.float64) * 2.0 / ROPE_DIM)
    ang = np.arange(seq, dtype=np.float64)[:, None] * inv_freq[None, :]
    cos, sin = np.cos(ang), np.sin(ang)
    ones = np.ones((seq, ATT_HEAD_DIM - ROPE_DIM))
    zeros = np.zeros((seq, ATT_HEAD_DIM - ROPE_DIM))
    zh = np.zeros((seq, half))
    reps = LANES // ATT_HEAD_DIM
    cos_t = np.concatenate([cos, cos, ones] * reps, axis=1).astype(np.float32)
    sin_lo = np.concatenate([-sin, zh, zeros] * reps, axis=1).astype(np.float32)
    sin_hi = np.concatenate([zh, sin, zeros] * reps, axis=1).astype(np.float32)
    return cos_t, sin_lo, sin_hi


def _row(v):
    return v.reshape(1, -1).astype(F32)


def _col(v):
    return v.reshape(-1, 1).astype(F32)


def kernel(x, mix_norm, mlp_norm, w_up, w_down, e_w_in, e_w_out, att_sinks, ml_conv_w, ml_conv_b, ml_i_bias,
           ml_f_bias, ml_norm, o_w_in, o_w_out, ssm_conv_w, ssm_conv_b, ssm_dt_bias, ssm_a_log, ssm_d, ssm_norm,
           rw_mu, rw_w0, rw_w2, rw_a0, rw_a2, rw_g2, rw_k_k, rw_k_a, rw_r_k, rw_ln_w, rw_ln_b, final_norm):
    batch, seq, _ = x.shape
    h = x.reshape(batch * seq, D_MODEL)
    fg = _row(final_norm)

    wu_all, wd_all = w_up.astype(BF16), w_down.astype(BF16)
    qa, ka, va, qkm, vm, om, gt = _inproj_even(h, _row(mix_norm[0]), _rope_tables(seq), e_w_in[0].astype(BF16), seq)
    gate_bias = _col(jnp.concatenate([ml_i_bias[0], ml_f_bias[0]]))
    h = _layer0(qa, ka, va, _row(att_sinks[0]), qkm, vm, om, gt, ml_conv_w[0].astype(F32), _row(ml_conv_b[0]),
                gate_bias, _row(ml_norm[0]), h, e_w_out[0].astype(BF16), _row(mlp_norm[0]), wu_all, wd_all, fg,
                batch, seq)

    w = o_w_in[0].astype(BF16)
    o_x = SSM_INNER
    o_dt = o_x + SSM_CONV_W
    o_rw = o_dt + SSM_HEADS
    parts = [w[:, :o_x], w[:, o_x:o_dt], w[:, o_dt:o_rw], w[:, o_rw:]]
    z, xbc, dt_t, rw = _inproj_odd(h, _row(mix_norm[1]), parts, seq)
    rw_consts = (_row(rw_mu[0]), _row(rw_w0[0]), rw_w2[0].astype(BF16), _row(rw_a0[0]), rw_a2[0].astype(BF16),
                 rw_g2[0].astype(BF16), _row(rw_k_k[0]), _row(rw_k_a[0]), _row(rw_r_k[0]), _row(rw_ln_w[0]),
                 _row(rw_ln_b[0]))
    y_rw = _rwkv(rw, rw_consts, batch, seq)
    wo = o_w_out[0].astype(BF16)
    h = _layer1(xbc, z, dt_t, ssm_conv_w[0].astype(F32), _row(ssm_conv_b[0]), _col(ssm_dt_bias[0]),
                _col(ssm_a_log[0]), _row(jnp.repeat(ssm_d[0], SSM_HEAD_DIM)), _row(ssm_norm[0]), y_rw, h,
                wo[:SSM_INNER], wo[SSM_INNER:], _row(mlp_norm[1]), wu_all, wd_all, fg, batch, seq)
    return h.reshape(batch, seq, D_MODEL)
```

```python
import functools

import numpy as np

import jax
import jax.numpy as jnp
from jax import lax
from jax.experimental import pallas as pl
from jax.experimental.pallas import tpu as pltpu

F32 = jnp.float32
BF16 = jnp.bfloat16

D_MODEL = 1024
D_FF = 4 * D_MODEL
NORM_EPS = 1e-5
CONV_TAPS = 4

ATT_HEAD_DIM = 64
ATT_Q_HEADS = 8
ATT_KV_HEADS = 2
ATT_GROUP = ATT_Q_HEADS // ATT_KV_HEADS
ATT_WINDOW = 128
ROPE_THETA = 500000.0
ROPE_DIM = 16
ATT_Q_W = ATT_Q_HEADS * ATT_HEAD_DIM
ATT_KV_W = ATT_KV_HEADS * ATT_HEAD_DIM

ML_HEADS = 4
ML_V_DIM = 128
ML_QK_DIM = 64
ML_CHUNK = 128
ML_NORM_EPS = 1e-6
ML_QK_W = ML_HEADS * ML_QK_DIM
ML_V_W = ML_HEADS * ML_V_DIM

SSM_HEAD_DIM = 64
SSM_HEADS = 8
SSM_GROUPS = 2
SSM_HEADS_PER_GROUP = SSM_HEADS // SSM_GROUPS
SSM_STATE = 128
SSM_CHUNK = 128
SSM_INNER = SSM_HEADS * SSM_HEAD_DIM
SSM_BC = SSM_GROUPS * SSM_STATE
SSM_CONV_W = SSM_INNER + 2 * SSM_BC

RW_HEAD_DIM = 64
RW_HEADS = 8
RW_W = RW_HEADS * RW_HEAD_DIM
RW_DECAY_LORA = 64
RW_AAA_LORA = 64
RW_GATE_LORA = 128
RW_IN = 3 * RW_W + RW_DECAY_LORA + RW_AAA_LORA + RW_GATE_LORA
RW_LN_EPS = 64e-5
RW_CHUNK = 64
RW_EXP_MINUS_HALF = float(np.exp(-0.5))

V7X_VMEM_BYTES = 64 * 1024 * 1024
VMEM_LIMIT_BYTES = V7X_VMEM_BYTES - 6 * 1024 * 1024
SUBLANES = 8
LANES = 128

PROJ_ROWS = 1024
FF_CHUNK = 512


def _params(*sem):
    return pltpu.CompilerParams(dimension_semantics=sem, vmem_limit_bytes=VMEM_LIMIT_BYTES)


def _const_spec(shape):
    nd = len(shape)
    return pl.BlockSpec(shape, lambda *_: (0,) * nd, pipeline_mode=pl.Buffered(1))


def _weight_spec(a, layer):
    if a.ndim == 2:
        return _const_spec(a.shape)
    return pl.BlockSpec((None,) + tuple(a.shape[1:]), lambda *_: (layer, 0, 0), pipeline_mode=pl.Buffered(1))


def _iota(shape, dim):
    return lax.broadcasted_iota(jnp.int32, shape, dim)


def _tri_lower(n):
    return (_iota((n, n), 1) <= _iota((n, n), 0)).astype(F32)


def _tri_upper(n):
    return (_iota((n, n), 0) <= _iota((n, n), 1)).astype(F32)


def _eye(n):
    return (_iota((n, n), 0) == _iota((n, n), 1)).astype(F32)


def _dot(a, b):
    return jnp.dot(a, b, preferred_element_type=F32)


def _dot_nt(a, b):
    return lax.dot_general(a, b, (((1,), (1,)), ((), ())), preferred_element_type=F32)


def _dot_tn(a, b):
    return lax.dot_general(a, b, (((0,), (1,)), ((), ())), preferred_element_type=F32)


def _split3_bf16(x):
    hi = x.astype(BF16)
    r1 = x - hi.astype(F32)
    mid = r1.astype(BF16)
    lo = (r1 - mid.astype(F32)).astype(BF16)
    return hi, mid, lo


def _sum3(terms):
    hi, mid, lo = terms
    return hi + (mid + lo)


def _sigmoid(x):
    return 1.0 / (1.0 + jnp.exp(-x))


def _softplus(x):
    return jnp.maximum(x, 0.0) + jnp.log1p(jnp.exp(-jnp.abs(x)))


def _silu(x):
    return x * _sigmoid(x)


def _rmsnorm(x, g):
    y = x * lax.rsqrt(jnp.mean(x * x, axis=-1, keepdims=True) + NORM_EPS)
    return y * g


def _shifted_rows(x, tail, k):
    rolled = pltpu.roll(x, k, axis=0)
    tail_rolled = pltpu.roll(tail, k, axis=0)
    first = jnp.where(_iota((SUBLANES, 1), 0) < k, tail_rolled, rolled[:SUBLANES])
    return jnp.concatenate([first, rolled[SUBLANES:]], axis=0)


def _causal_conv_silu(x, tail, w, b):
    out = b + x * w[CONV_TAPS - 1:CONV_TAPS, :]
    for k in range(1, CONV_TAPS):
        out = out + _shifted_rows(x, tail, k) * w[CONV_TAPS - 1 - k:CONV_TAPS - k, :]
    return _silu(out)


def _rope(x, cos, sin_lo, sin_hi, reps):
    w = x.shape[1]
    c = jnp.concatenate([cos] * reps, axis=1)
    s_lo = jnp.concatenate([sin_lo] * reps, axis=1)
    s_hi = jnp.concatenate([sin_hi] * reps, axis=1)
    half = ROPE_DIM // 2
    nxt = pltpu.roll(x, w - half, axis=1)
    prv = pltpu.roll(x, half, axis=1)
    return x * c + nxt * s_lo + prv * s_hi


EVEN_SEGMENTS = (ATT_Q_W, ATT_KV_W, ATT_KV_W, 2 * ML_QK_W, ML_V_W, ML_V_W, 2 * ML_HEADS)
EVEN_OFFSETS = tuple(sum(EVEN_SEGMENTS[:i]) for i in range(len(EVEN_SEGMENTS)))


def _inproj_even_kernel(x_ref, g_ref, cos_ref, slo_ref, shi_ref, w_ref,
                        qa_ref, ka_ref, va_ref, qkm_ref, vm_ref, om_ref, gt_ref):
    hb = _rmsnorm(x_ref[...], g_ref[...]).astype(BF16)
    cos, s_lo, s_hi = cos_ref[...], slo_ref[...], shi_ref[...]
    seg = lambda i: w_ref[:, EVEN_OFFSETS[i]:EVEN_OFFSETS[i] + EVEN_SEGMENTS[i]]
    q = _rope(_dot(hb, seg(0)), cos, s_lo, s_hi, ATT_Q_W // LANES)
    qa_ref[...] = (q * (ATT_HEAD_DIM ** -0.5)).astype(BF16)
    ka_ref[...] = _rope(_dot(hb, seg(1)), cos, s_lo, s_hi, ATT_KV_W // LANES).astype(BF16)
    va_ref[...] = _dot(hb, seg(2)).astype(BF16)
    qkm_ref[...] = _dot(hb, seg(3))
    vm_ref[...] = _dot(hb, seg(4)).astype(BF16)
    om_ref[...] = _dot(hb, seg(5))
    gt_ref[...] = _dot_tn(seg(6), hb)


def _inproj_even(x2, g, tables, w, seq):
    n = x2.shape[0]
    tm = min(PROJ_ROWS, seq)
    per_seq = seq // tm
    row = lambda width: pl.BlockSpec((tm, width), lambda i: (i, 0))
    tab = pl.BlockSpec((tm, LANES), lambda i: (i % per_seq, 0))
    out_shapes = (
        jax.ShapeDtypeStruct((n, ATT_Q_W), BF16), jax.ShapeDtypeStruct((n, ATT_KV_W), BF16),
        jax.ShapeDtypeStruct((n, ATT_KV_W), BF16), jax.ShapeDtypeStruct((n, 2 * ML_QK_W), F32),
        jax.ShapeDtypeStruct((n, ML_V_W), BF16), jax.ShapeDtypeStruct((n, ML_V_W), F32),
        jax.ShapeDtypeStruct((n // seq, 2 * ML_HEADS, seq), F32))
    return pl.pallas_call(
        _inproj_even_kernel,
        grid=(n // tm,),
        in_specs=[row(D_MODEL), _const_spec((1, D_MODEL)), tab, tab, tab, _const_spec(w.shape)],
        out_specs=(row(ATT_Q_W), row(ATT_KV_W), row(ATT_KV_W), row(2 * ML_QK_W), row(ML_V_W), row(ML_V_W),
                   pl.BlockSpec((None, 2 * ML_HEADS, tm), lambda i: (i // per_seq, 0, i % per_seq))),
        out_shape=out_shapes,
        compiler_params=_params("arbitrary"),
        name="inproj_even",
    )(x2, g, *tables, w)


def _attention_stages(q_ref, kc_ref, kp_ref, vc_ref, vp_ref, sink_ref, blk, nb, emit):
    w = ATT_WINDOW
    dh = ATT_HEAD_DIM
    G = ATT_GROUP
    i = _iota((G * w, 2 * w), 0) % w
    j = _iota((G * w, 2 * w), 1)
    mask = (j > i) & (j <= i + w) & ((j >= w) | (blk > 0))
    sinks = sink_ref[...]
    problems = [(b, g) for b in range(nb) for g in range(ATT_KV_HEADS)]
    scores, values, sink_cols = [], [], []
    for b, g in problems:
        q = q_ref[b]
        qs = jnp.concatenate([q[:, (g * G + a) * dh:(g * G + a + 1) * dh] for a in range(G)], axis=0)
        ks = jnp.concatenate([kp_ref[b][:, g * dh:(g + 1) * dh], kc_ref[b][:, g * dh:(g + 1) * dh]], axis=0)
        values.append(jnp.concatenate([vp_ref[b][:, g * dh:(g + 1) * dh], vc_ref[b][:, g * dh:(g + 1) * dh]], axis=0))
        scores.append(_dot_nt(qs, ks))
        sink_cols.append(jnp.concatenate(
            [jnp.broadcast_to(sinks[:, g * G + a:g * G + a + 1], (w, 1)) for a in range(G)], axis=0))
    yield
    probs, dens = [], []
    for s, sink in zip(scores, sink_cols):
        s = jnp.where(mask, s, -jnp.inf)
        m = jnp.maximum(jnp.max(s, axis=-1, keepdims=True), sink)
        p = jnp.exp(s - m)
        dens.append(jnp.sum(p, axis=-1, keepdims=True) + jnp.exp(sink - m))
        probs.append(p.astype(BF16))
    pv = [_dot(p, v) for p, v in zip(probs, values)]
    yield
    outs = [o / d for o, d in zip(pv, dens)]
    for b in range(nb):
        heads = [outs[b * ATT_KV_HEADS + g][a * w:(a + 1) * w] for g in range(ATT_KV_HEADS) for a in range(G)]
        emit(b, jnp.concatenate(heads, axis=1).astype(BF16))


def _mlstm_stages(qk_ref, tail_ref, v_ref, o_ref, gt_ref, cw_ref, cb_ref, gb_ref, nw_ref, c_s, m_s, chunk, nb, emit):
    L = ML_CHUNK
    H = ML_HEADS
    DV = ML_V_DIM
    NG = 2 * H

    cw, cb = cw_ref[...], cb_ref[...]
    qk = [_causal_conv_silu(qk_ref[b], jnp.where(chunk == 0, 0.0, tail_ref[b]), cw, cb) for b in range(nb)]

    g = jnp.concatenate([gt_ref[b] + gb_ref[...] for b in range(nb)], axis=0)
    lg = jnp.where(_iota(g.shape, 0) % NG < H, g, -_softplus(-g))
    tri_l, tri_u, eye = _tri_lower(L).astype(BF16), _tri_upper(L).astype(BF16), _eye(L).astype(BF16)
    lg3 = _split3_bf16(lg)
    cum_row = _sum3([_dot(p, tri_u) for p in lg3])
    lg_col = _sum3([_dot_nt(eye, p) for p in lg3])
    cum_col = _sum3([_dot_nt(tri_l, p) for p in lg3])
    causal = _iota((L, L), 1) <= _iota((L, L), 0)

    chains = [(b, h) for b in range(nb) for h in range(H)]
    idx = lambda b, h: b * H + h
    q_c = [qk[b][:, h * ML_QK_DIM:(h + 1) * ML_QK_DIM] for b, h in chains]
    k_c = [qk[b][:, ML_QK_W + h * ML_QK_DIM:ML_QK_W + (h + 1) * ML_QK_DIM] * (ML_QK_DIM ** -0.5) for b, h in chains]
    ones = jnp.ones((L, DV), BF16)
    v_c = [jnp.concatenate([v_ref[b][:, h * DV:(h + 1) * DV], ones], axis=1) for b, h in chains]
    qb_c = [q.astype(BF16) for q in q_c]
    c_prev = [c_s[idx(b, h)] for b, h in chains]
    m_prev = [m_s[idx(b, h):idx(b, h) + 1, 0:1] for b, h in chains]

    s_qk = [_dot_nt(qb, k.astype(BF16)) for qb, k in zip(qb_c, k_c)]
    inter = [_dot(qb, c.astype(BF16)) for qb, c in zip(qb_c, c_prev)]

    yield
    gi_c = [b * NG + h for b, h in chains]
    b_col = [cum_col[:, gi + H:gi + H + 1] for gi in gi_c]
    b_last = [bc[L - 1:L, :] for bc in b_col]
    dlog = [jnp.where(causal, bc - cum_row[gi + H:gi + H + 1, :] + lg[gi:gi + 1, :], -jnp.inf)
            for gi, bc in zip(gi_c, b_col)]
    row_max = [jnp.max(d, axis=1, keepdims=True) for d in dlog]
    a_loc = [bl - bc + lg_col[:, gi:gi + 1] for gi, bc, bl in zip(gi_c, b_col, b_last)]
    m_loc = [jnp.max(a, axis=0, keepdims=True) for a in a_loc]
    inter_log = [bc + mp for bc, mp in zip(b_col, m_prev)]
    m_t = [jnp.maximum(il, rm) for il, rm in zip(inter_log, row_max)]
    qk_w = [s * jnp.exp(d - mt) for s, d, mt in zip(s_qk, dlog, m_t)]
    w_inter = [jnp.exp(il - mt) for il, mt in zip(inter_log, m_t)]
    kw = [k * jnp.exp(a - ml) for k, a, ml in zip(k_c, a_loc, m_loc)]

    intra = [_dot(w.astype(BF16), v) for w, v in zip(qk_w, v_c)]
    c_loc = [lax.dot_general(w.astype(BF16), v, (((0,), (0,)), ((), ())), preferred_element_type=F32)
             for w, v in zip(kw, v_c)]

    yield
    both = [a + w * e for a, w, e in zip(intra, w_inter, inter)]
    xs = [nd[:, :DV] / jnp.maximum(jnp.abs(nd[:, DV:DV + 1]), jnp.exp(-mt)) for nd, mt in zip(both, m_t)]
    mus = [jnp.mean(x, axis=-1, keepdims=True) for x in xs]
    cen = [x - mu for x, mu in zip(xs, mus)]
    var = [jnp.mean(jnp.square(c), axis=-1, keepdims=True) for c in cen]
    outs = [c * lax.rsqrt(vr + ML_NORM_EPS) for c, vr in zip(cen, var)]
    for i, (b, h) in enumerate(chains):
        m_new = jnp.maximum(b_last[i] + m_prev[i], m_loc[i])
        fa = jnp.exp(b_last[i] + m_prev[i] - m_new)
        fb = jnp.exp(m_loc[i] - m_new)
        c_s[idx(b, h)] = fa * c_prev[i] + fb * c_loc[i]
        m_s[idx(b, h):idx(b, h) + 1, 0:1] = m_new

    nw = nw_ref[...]
    for b in range(nb):
        hm = jnp.concatenate(outs[b * H:(b + 1) * H], axis=1) * nw
        emit(b, (hm * _sigmoid(o_ref[b])).astype(BF16))


def _outproj_mlp_stages(y_parts, h, g_ref, wu_ref, wd_ref, fg_ref, final_norm, emit):
    h1 = h
    for y, w_ref in y_parts:
        h1 = h1 + _dot(y, w_ref[...])
    hb = _rmsnorm(h1, g_ref[...]).astype(BF16)
    n_chunks = D_FF // FF_CHUNK
    mlp = None
    for c in range(n_chunks):
        u = _dot(hb, wu_ref[:, c * FF_CHUNK:(c + 1) * FF_CHUNK])
        a = jnp.square(jnp.maximum(u, 0.0)).astype(BF16)
        d = _dot(a, wd_ref[c * FF_CHUNK:(c + 1) * FF_CHUNK, :])
        mlp = d if mlp is None else mlp + d
        if c + 1 in MLP_SPLIT:
            yield
    h2 = h1 + mlp
    if final_norm:
        h2 = _rmsnorm(h2, fg_ref[...])
    emit(h2)


MLP_SPLIT = (5,)


def _drive(mixers, mlp):
    for m in mixers:
        next(m)
    next(mlp)
    for m in mixers:
        next(m)
    for m in mixers:
        for _ in m:
            pass
    for _ in mlp:
        pass


def _layer0_kernel(q_ref, kc_ref, kp_ref, vc_ref, vp_ref, sink_ref,
                   qk_ref, tail_ref, vm_ref, om_ref, gt_ref, cw_ref, cb_ref, gb_ref, nw_ref,
                   h_ref, wo_ref, g_ref, wu_ref, wd_ref, fg_ref,
                   o_ref, c_s, m_s, y_s, *, nb, nc):
    step = pl.program_id(0)
    chunk = jnp.minimum(step, nc - 1)
    L = ML_CHUNK

    @pl.when(step == 0)
    def _():
        c_s[...] = jnp.zeros_like(c_s)
        m_s[...] = jnp.zeros_like(m_s)
        y_s[...] = jnp.zeros_like(y_s)

    wr = step % 2
    rd = 1 - wr

    def emit_att(b, block):
        y_s[wr, b * L:(b + 1) * L, 0:ATT_Q_W] = block

    def emit_ml(b, block):
        y_s[wr, b * L:(b + 1) * L, ATT_Q_W:ATT_Q_W + ML_V_W] = block

    def emit_out(block):
        o_ref[...] = block.reshape(nb, L, D_MODEL)

    mixers = [
        _attention_stages(q_ref, kc_ref, kp_ref, vc_ref, vp_ref, sink_ref, chunk, nb, emit_att),
        _mlstm_stages(qk_ref, tail_ref, vm_ref, om_ref, gt_ref, cw_ref, cb_ref, gb_ref, nw_ref, c_s, m_s, chunk, nb,
                      emit_ml),
    ]
    mlp = _outproj_mlp_stages([(y_s[rd], wo_ref)], h_ref[...].reshape(nb * L, D_MODEL), g_ref, wu_ref, wd_ref,
                              fg_ref, False, emit_out)
    _drive(mixers, mlp)


def _layer0(qa, ka, va, sinks, qkm, vm, om, gt, conv_w, conv_b, gate_bias, norm_w, h, wo, g, wu, wd, fg, batch, seq):
    L = ML_CHUNK
    nc = seq // L
    per8 = L // SUBLANES
    three = lambda t: t.reshape(batch, seq, t.shape[-1])
    mix = lambda c: jnp.minimum(c, nc - 1)
    cur = lambda width: pl.BlockSpec((batch, L, width), lambda c: (0, mix(c), 0))
    prev = lambda width: pl.BlockSpec((batch, L, width), lambda c: (0, jnp.maximum(mix(c) - 1, 0), 0))
    tail = pl.BlockSpec((batch, SUBLANES, 2 * ML_QK_W), lambda c: (0, jnp.maximum(mix(c) * per8 - 1, 0), 0))
    late = pl.BlockSpec((batch, L, D_MODEL), lambda c: (0, jnp.maximum(c - 1, 0), 0))
    q3, k3, v3, qk3 = three(qa), three(ka), three(va), three(qkm)
    consts = (conv_w, conv_b, gate_bias, norm_w)
    weights = (wo, g, wu, wd, fg)
    out = pl.pallas_call(
        functools.partial(_layer0_kernel, nb=batch, nc=nc),
        grid=(nc + 1,),
        in_specs=[cur(ATT_Q_W), cur(ATT_KV_W), prev(ATT_KV_W), cur(ATT_KV_W), prev(ATT_KV_W),
                  _const_spec((1, ATT_Q_HEADS)),
                  cur(2 * ML_QK_W), tail, cur(ML_V_W), cur(ML_V_W),
                  pl.BlockSpec((batch, 2 * ML_HEADS, L), lambda c: (0, 0, mix(c)))]
                 + [_const_spec(a.shape) for a in consts] + [late] + [_weight_spec(a, 0) for a in weights],
        out_specs=late,
        out_shape=jax.ShapeDtypeStruct((batch, seq, D_MODEL), F32),
        scratch_shapes=[pltpu.VMEM((batch * ML_HEADS, ML_QK_DIM, 2 * ML_V_DIM), F32),
                        pltpu.VMEM((batch * ML_HEADS, LANES), F32),
                        pltpu.VMEM((2, batch * L, ATT_Q_W + ML_V_W), BF16)],
        compiler_params=_params("arbitrary"),
        name="layer0_mix_mlp",
    )(q3, k3, k3, v3, v3, sinks, qk3, qk3, three(vm), three(om), gt, *consts, three(h), *weights)
    return out.reshape(batch * seq, D_MODEL)


def _inproj_odd_kernel(x_ref, g_ref, wz_ref, wx_ref, wdt_ref, wrw_ref, z_ref, xbc_ref, dt_ref, rw_ref):
    hb = _rmsnorm(x_ref[...], g_ref[...]).astype(BF16)
    z_ref[...] = _dot(hb, wz_ref[...])
    xbc_ref[...] = _dot(hb, wx_ref[...])
    dt_ref[...] = _dot_tn(wdt_ref[...], hb)
    rw_ref[...] = _dot(hb, wrw_ref[...])


def _inproj_odd(x2, g, w, seq):
    n = x2.shape[0]
    tm = min(PROJ_ROWS, seq)
    per_seq = seq // tm
    row = lambda width: pl.BlockSpec((tm, width), lambda i: (i, 0))
    out_shapes = (jax.ShapeDtypeStruct((n, SSM_INNER), F32), jax.ShapeDtypeStruct((n, SSM_CONV_W), F32),
                  jax.ShapeDtypeStruct((n // seq, SSM_HEADS, seq), F32), jax.ShapeDtypeStruct((n, RW_IN), F32))
    return pl.pallas_call(
        _inproj_odd_kernel,
        grid=(n // tm,),
        in_specs=[row(D_MODEL), _const_spec((1, D_MODEL))] + [_const_spec(a.shape) for a in w],
        out_specs=(row(SSM_INNER), row(SSM_CONV_W),
                   pl.BlockSpec((None, SSM_HEADS, tm), lambda i: (i // per_seq, 0, i % per_seq)), row(RW_IN)),
        out_shape=out_shapes,
        compiler_params=_params("arbitrary"),
        name="inproj_odd",
    )(x2, g, *w)


def _ssd_stages(xbc_ref, tail_ref, z_ref, dt_ref, cw_ref, cb_ref, dtb_ref, alog_ref, d_ref, nw_ref, s_s, chunk, nb,
                emit):
    L = SSM_CHUNK
    P = SSM_HEAD_DIM
    HG = SSM_HEADS_PER_GROUP
    NH = SSM_HEADS

    cw, cb_ = cw_ref[...], cb_ref[...]
    xbc = [_causal_conv_silu(xbc_ref[b], jnp.where(chunk == 0, 0.0, tail_ref[b]), cw, cb_) for b in range(nb)]

    dt_row = _softplus(jnp.concatenate([dt_ref[b] + dtb_ref[...] for b in range(nb)], axis=0))
    neg_a = -jnp.exp(jnp.concatenate([alog_ref[...]] * nb, axis=0))
    a_row = dt_row * neg_a
    tri_l, tri_u, eye_b = _tri_lower(L).astype(BF16), _tri_upper(L).astype(BF16), _eye(L).astype(BF16)
    a3 = _split3_bf16(a_row)
    acum_row = _sum3([_dot(p, tri_u) for p in a3])
    acum_col = _sum3([_dot_nt(tri_l, p) for p in a3])
    dt_col = _sum3([_dot_nt(eye_b, p) for p in _split3_bf16(dt_row)])
    causal = _iota((L, L), 1) <= _iota((L, L), 0)

    groups = [(b, g) for b in range(nb) for g in range(SSM_GROUPS)]
    gidx = lambda b, g: b * SSM_GROUPS + g
    bg = [xbc[b][:, SSM_INNER + g * SSM_STATE:SSM_INNER + (g + 1) * SSM_STATE].astype(BF16) for b, g in groups]
    cg = [xbc[b][:, SSM_INNER + SSM_BC + g * SSM_STATE:SSM_INNER + SSM_BC + (g + 1) * SSM_STATE].astype(BF16)
          for b, g in groups]
    s_prev = [s_s[gidx(b, g)] for b, g in groups]

    cb = [_dot_nt(c, bq) for c, bq in zip(cg, bg)]
    bg_t = [_dot_nt(eye_b, bq).astype(BF16) for bq in bg]
    y_off = [_dot(c, sp.astype(BF16)) for c, sp in zip(cg, s_prev)]

    yield
    heads = [(b, g, j) for b, g in groups for j in range(HG)]
    scores, xcs, e_col, xws, cds = [], [], [], [], []
    for b, g, j in heads:
        h = g * HG + j
        col = b * NH + h
        a_col = acum_col[:, col:col + 1]
        a_last = a_col[L - 1:L, :]
        decay = jnp.exp(jnp.where(causal, a_col - acum_row[col:col + 1, :], -jnp.inf))
        xc = xbc[b][:, h * P:(h + 1) * P] * dt_col[:, col:col + 1]
        scores.append((cb[gidx(b, g)] * decay).astype(BF16))
        xcs.append(xc.astype(BF16))
        e_col.append(jnp.exp(a_col))
        xws.append(xc * jnp.exp(a_last - a_col))
        cds.append(jnp.broadcast_to(jnp.exp(a_last), (1, P)))

    y_diag = [_dot(sc, xc) for sc, xc in zip(scores, xcs)]
    states = [_dot(bg_t[i], jnp.concatenate(xws[i * HG:(i + 1) * HG], axis=1).astype(BF16))
              for i in range(len(groups))]

    yield
    for i, (b, g) in enumerate(groups):
        s_s[gidx(b, g)] = s_prev[i] * jnp.concatenate(cds[i * HG:(i + 1) * HG], axis=1) + states[i]

    gw = SSM_INNER // SSM_GROUPS
    for b in range(nb):
        ys = []
        for g in range(SSM_GROUPS):
            for j in range(HG):
                i = (b * SSM_GROUPS + g) * HG + j
                ys.append(y_diag[i] + y_off[gidx(b, g)][:, j * P:(j + 1) * P] * e_col[i])
        xs = xbc[b][:, :SSM_INNER]
        y = jnp.concatenate(ys, axis=1) + d_ref[...] * xs
        y = y * _silu(z_ref[b])
        normed = []
        for g in range(SSM_GROUPS):
            yg = y[:, g * gw:(g + 1) * gw]
            normed.append(yg * lax.rsqrt(jnp.mean(yg * yg, axis=-1, keepdims=True) + NORM_EPS))
        emit(b, (jnp.concatenate(normed, axis=1) * nw_ref[...]).astype(BF16))


def _layer1_kernel(xbc_ref, tail_ref, z_ref, dt_ref, cw_ref, cb_ref, dtb_ref, alog_ref, d_ref, nw_ref,
                   yrw_ref, h_ref, wa_ref, wb_ref, g_ref, wu_ref, wd_ref, fg_ref,
                   o_ref, s_s, y_s, *, nb, nc):
    step = pl.program_id(0)
    chunk = jnp.minimum(step, nc - 1)
    L = SSM_CHUNK

    @pl.when(step == 0)
    def _():
        s_s[...] = jnp.zeros_like(s_s)
        y_s[...] = jnp.zeros_like(y_s)

    wr = step % 2
    rd = 1 - wr

    def emit_ssm(b, block):
        y_s[wr, b * L:(b + 1) * L, :] = block

    def emit_out(block):
        o_ref[...] = block.reshape(nb, L, D_MODEL)

    mixers = [_ssd_stages(xbc_ref, tail_ref, z_ref, dt_ref, cw_ref, cb_ref, dtb_ref, alog_ref, d_ref, nw_ref, s_s,
                          chunk, nb, emit_ssm)]
    mlp = _outproj_mlp_stages([(y_s[rd], wa_ref), (yrw_ref[...].reshape(nb * L, RW_W), wb_ref)],
                              h_ref[...].reshape(nb * L, D_MODEL), g_ref, wu_ref, wd_ref, fg_ref, True, emit_out)
    _drive(mixers, mlp)


def _layer1(xbc, z, dt_t, conv_w, conv_b, dt_bias, a_log, d_row, norm_w, y_rw, h, wa, wb, g, wu, wd, fg, batch, seq):
    L = SSM_CHUNK
    nc = seq // L
    per8 = L // SUBLANES
    three = lambda t: t.reshape(batch, seq, t.shape[-1])
    mix = lambda c: jnp.minimum(c, nc - 1)
    cur = lambda width: pl.BlockSpec((batch, L, width), lambda c: (0, mix(c), 0))
    tail = pl.BlockSpec((batch, SUBLANES, SSM_CONV_W), lambda c: (0, jnp.maximum(mix(c) * per8 - 1, 0), 0))
    late = lambda width: pl.BlockSpec((batch, L, width), lambda c: (0, jnp.maximum(c - 1, 0), 0))
    consts = (conv_w, conv_b, dt_bias, a_log, d_row, norm_w)
    weights = (wa, wb, g, wu, wd, fg)
    x3 = three(xbc)
    out = pl.pallas_call(
        functools.partial(_layer1_kernel, nb=batch, nc=nc),
        grid=(nc + 1,),
        in_specs=[cur(SSM_CONV_W), tail, cur(SSM_INNER),
                  pl.BlockSpec((batch, SSM_HEADS, L), lambda c: (0, 0, mix(c)))]
                 + [_const_spec(a.shape) for a in consts] + [late(RW_W), late(D_MODEL)]
                 + [_weight_spec(a, 1) for a in weights],
        out_specs=late(D_MODEL),
        out_shape=jax.ShapeDtypeStruct((batch, seq, D_MODEL), F32),
        scratch_shapes=[pltpu.VMEM((batch * SSM_GROUPS, SSM_STATE, SSM_HEADS_PER_GROUP * SSM_HEAD_DIM), F32),
                        pltpu.VMEM((2, batch * L, SSM_INNER), BF16)],
        compiler_params=_params("arbitrary"),
        name="layer1_mix_mlp",
    )(x3, x3, three(z), dt_t, *consts, three(y_rw), three(h), *weights)
    return out.reshape(batch * seq, D_MODEL)


def _split_bf16(x):
    hi = x.astype(BF16)
    lo = (x - hi.astype(F32)).astype(BF16)
    return hi, lo


def _dot_ones_exact(t, ones_matrix):
    hi, lo = _split_bf16(t)
    return _dot(hi, ones_matrix) + _dot(lo, ones_matrix)


def _unit_lower_inverse_steps(mats, eye, n):
    xbs = [(eye - a).astype(BF16) for a in mats]
    mats = [a.astype(BF16) for a in mats]
    terms = 2
    while terms < n:
        terms *= 2
        xfs = [xb.astype(F32) for xb in xbs]
        resids = [(eye - xf - _dot(a, xb)).astype(BF16) for a, xb, xf in zip(mats, xbs, xfs)]
        xbs = [(xf + _dot(xb, e)).astype(BF16) for xb, xf, e in zip(xbs, xfs, resids)]
        yield xbs


RW_FRONT_STAGES = 4
RW_CHAIN_STAGES = 10


def _rwkv_stages(x, before, mu_ref, w0_ref, w2_ref, a0_ref, a2_ref, g2_ref, kk_ref, ka_ref, rk_ref, lnw_ref, lnb_ref,
                 st_s, nb, emit):
    C = RW_CHUNK
    D = RW_HEAD_DIM
    R = nb * C
    rolled = pltpu.roll(x, 1, axis=0)
    first_row = _iota((C, 1), 0) == 0
    prev = jnp.concatenate([jnp.where(first_row, before[b], rolled[b * C:(b + 1) * C]) for b in range(nb)], axis=0)
    xm = x + (prev - x) * mu_ref[...]
    r = xm[:, 0:RW_W]
    k = xm[:, RW_W:2 * RW_W]
    v = xm[:, 2 * RW_W:3 * RW_W]
    o = 3 * RW_W
    wd = xm[:, o:o + RW_DECAY_LORA]
    ad = xm[:, o + RW_DECAY_LORA:o + RW_DECAY_LORA + RW_AAA_LORA]
    gd = xm[:, o + RW_DECAY_LORA + RW_AAA_LORA:]
    yield

    z = w0_ref[...] + _dot(jnp.tanh(wd).astype(BF16), w2_ref[...])
    ld = -RW_EXP_MINUS_HALF * _sigmoid(z)
    a = _sigmoid(a0_ref[...] + _dot(ad.astype(BF16), a2_ref[...]))
    gate = _dot(_sigmoid(gd).astype(BF16), g2_ref[...])
    yield

    half = RW_W // 2
    same_head = (_iota((half, half), 0) // D == _iota((half, half), 1) // D).astype(BF16)
    head_sum = lambda t: jnp.concatenate(
        [_dot_ones_exact(t[:, :half], same_head), _dot_ones_exact(t[:, half:], same_head)], axis=1)

    kk = k * kk_ref[...]
    kk = kk * lax.rsqrt(jnp.maximum(head_sum(kk * kk), 1e-24))
    k2 = k * (1.0 + (a - 1.0) * ka_ref[...])
    bvec = kk * a
    yield

    tri = ((_iota((R, R), 1) <= _iota((R, R), 0)) & (_iota((R, R), 1) // C == _iota((R, R), 0) // C)).astype(BF16)
    cl = _sum3([_dot(tri, p) for p in _split3_bf16(ld)])
    c_last = jnp.concatenate(
        [jnp.broadcast_to(cl[(b + 1) * C - 1:(b + 1) * C, :], (C, RW_W)) for b in range(nb)], axis=0)
    e_neg = jnp.exp(-cl)
    e_end = jnp.exp(c_last - cl)
    w_end = [jnp.exp(cl[(b + 1) * C - 1:(b + 1) * C, :]) for b in range(nb)]
    rw_ = r * jnp.exp(cl)
    kkw = kk * jnp.exp(cl - ld)
    bd = bvec * e_neg
    kd = k2 * e_neg
    bd_end = bvec * e_end
    kd_end = k2 * e_end
    yield

    g_row = _iota((2 * C, 2 * C), 0)
    g_col = _iota((2 * C, 2 * C), 1) % C
    keep = g_col < jnp.where(g_row < C, g_row, g_row - C + 1)
    eye = _eye(C)
    chains = [(b, h) for b in range(nb) for h in range(RW_HEADS)]
    blk = lambda t, b, h: t[b * C:(b + 1) * C, h * D:(h + 1) * D]
    kkw_c = [blk(kkw, b, h).astype(BF16) for b, h in chains]
    rw_c = [blk(rw_, b, h).astype(BF16) for b, h in chains]
    vb_c = [blk(v, b, h).astype(BF16) for b, h in chains]
    gms = [jnp.where(keep, _dot_nt(jnp.concatenate([kq, rq], axis=0),
                                   jnp.concatenate([blk(bd, b, h), blk(kd, b, h)], axis=0).astype(BF16)), 0.0)
           for (b, h), kq, rq in zip(chains, kkw_c, rw_c)]
    yield
    a_v = [_dot(g[:, C:].astype(BF16), vb) for g, vb in zip(gms, vb_c)]
    a_rb = [g[C:, :C].astype(BF16) for g in gms]
    yield
    invs = None
    for invs in _unit_lower_inverse_steps([g[:C, :C] for g in gms], eye, C):
        yield
    tws = [_dot(inv, jnp.concatenate([av[:C].astype(BF16), kq], axis=1))
           for inv, av, kq in zip(invs, a_v, kkw_c)]
    yield
    sts = [st_s[b * RW_HEADS + h] for b, h in chains]
    xss = [_dot_nt(jnp.concatenate([tw[:, D:].astype(BF16), rq], axis=0), st.astype(BF16))
           for tw, rq, st in zip(tws, rw_c, sts)]
    yield
    ubs =[(-tw[:, :D] - xs[:C]).astype(BF16) for tw, xs in zip(tws, xss)]
    ys = [xs[C:] + _dot(m, ub) + av[C:] for xs, m, ub, av in zip(xss, a_rb, ubs, a_v)]
    for (b, h), st, ub, vb in zip(chains, sts, ubs, vb_c):
        ends = jnp.concatenate([blk(bd_end, b, h), blk(kd_end, b, h)], axis=0).astype(BF16)
        upd = lax.dot_general(jnp.concatenate([ub, vb], axis=0), ends, (((0,), (0,)), ((), ())),
                              preferred_element_type=F32)
        st_s[b * RW_HEADS + h] = st * w_end[b][:, h * D:(h + 1) * D] + upd
    yield

    y = jnp.concatenate([jnp.concatenate(ys[b * RW_HEADS:(b + 1) * RW_HEADS], axis=1) for b in range(nb)], axis=0)
    mu = head_sum(y) * (1.0 / D)
    var = head_sum(jnp.square(y - mu)) * (1.0 / D)
    yn = (y - mu) * lax.rsqrt(var + RW_LN_EPS) * lnw_ref[...] + lnb_ref[...]
    bonus = head_sum(r * k2 * rk_ref[...]) * v
    emit(((yn + bonus) * gate).astype(BF16))


RW_STEP_CHUNKS = 4


def _rwkv_kernel(rw_ref, tail_ref, mu_ref, w0_ref, w2_ref, a0_ref, a2_ref, g2_ref, kk_ref, ka_ref, rk_ref,
                 lnw_ref, lnb_ref, y_ref, st_s, *, nb):
    step = pl.program_id(0)
    C = RW_CHUNK

    @pl.when(step == 0)
    def _():
        st_s[...] = jnp.zeros_like(st_s)

    xs = [rw_ref[:, j * C:(j + 1) * C, :].reshape(nb * C, RW_IN) for j in range(RW_STEP_CHUNKS)]
    befores = [[jnp.where(step == 0, 0.0, tail_ref[b, SUBLANES - 1:SUBLANES, :]) for b in range(nb)]]
    befores += [[xs[j][(b + 1) * C - 1:(b + 1) * C, :] for b in range(nb)] for j in range(RW_STEP_CHUNKS - 1)]

    def emit(j):
        def write(block):
            y_ref[:, j * C:(j + 1) * C, :] = block.reshape(nb, C, RW_W)
        return write

    args = (mu_ref, w0_ref, w2_ref, a0_ref, a2_ref, g2_ref, kk_ref, ka_ref, rk_ref, lnw_ref, lnb_ref, st_s, nb)
    gens = [_rwkv_stages(xs[j], befores[j], *args, emit(j)) for j in range(RW_STEP_CHUNKS)]
    for g in gens:
        for _ in range(RW_FRONT_STAGES):
            next(g)
    for j, g in enumerate(gens):
        for _ in range(RW_CHAIN_STAGES - (1 if j else 0)):
            next(g)
        if j + 1 < RW_STEP_CHUNKS:
            next(gens[j + 1])
        for _ in g:
            pass


def _rwkv(rw, consts, batch, seq):
    C = RW_CHUNK * RW_STEP_CHUNKS
    nc = seq // C
    per8 = C // SUBLANES
    rw3 = rw.reshape(batch, seq, RW_IN)
    y = pl.pallas_call(
        functools.partial(_rwkv_kernel, nb=batch),
        grid=(nc,),
        in_specs=[pl.BlockSpec((batch, C, RW_IN), lambda c: (0, c, 0)),
                  pl.BlockSpec((batch, SUBLANES, RW_IN), lambda c: (0, jnp.maximum(c * per8 - 1, 0), 0))]
                 + [_const_spec(a.shape) for a in consts],
        out_specs=pl.BlockSpec((batch, C, RW_W), lambda c: (0, c, 0)),
        out_shape=jax.ShapeDtypeStruct((batch, seq, RW_W), BF16),
        scratch_shapes=[pltpu.VMEM((batch * RW_HEADS, RW_HEAD_DIM, RW_HEAD_DIM), F32)],
        compiler_params=_params("arbitrary"),
        name="rwkv7",
    )(rw3, rw3, *consts)
    return y.reshape(batch * seq, RW_W)


@functools.lru_cache(maxsize=None)
def _rope_tables(seq):
    half = ROPE_DIM // 2
    inv_freq = ROPE_THETA ** (-np.arange(half, dtype=np.float64) * 2.0 / ROPE_DIM)
    ang = np.arange(seq, dtype=np.float64)[:, None] * inv_freq[None, :]
    cos, sin = np.cos(ang), np.sin(ang)
    ones = np.ones((seq, ATT_HEAD_DIM - ROPE_DIM))
    zeros = np.zeros((seq, ATT_HEAD_DIM - ROPE_DIM))
    zh = np.zeros((seq, half))
    reps = LANES // ATT_HEAD_DIM
    cos_t = np.concatenate([cos, cos, ones] * reps, axis=1).astype(np.float32)
    sin_lo = np.concatenate([-sin, zh, zeros] * reps, axis=1).astype(np.float32)
    sin_hi = np.concatenate([zh, sin, zeros] * reps, axis=1).astype(np.float32)
    return cos_t, sin_lo, sin_hi


def _row(v):
    return v.reshape(1, -1).astype(F32)


def _col(v):
    return v.reshape(-1, 1).astype(F32)


def kernel(x, mix_norm, mlp_norm, w_up, w_down, e_w_in, e_w_out, att_sinks, ml_conv_w, ml_conv_b, ml_i_bias,
           ml_f_bias, ml_norm, o_w_in, o_w_out, ssm_conv_w, ssm_conv_b, ssm_dt_bias, ssm_a_log, ssm_d, ssm_norm,
           rw_mu, rw_w0, rw_w2, rw_a0, rw_a2, rw_g2, rw_k_k, rw_k_a, rw_r_k, rw_ln_w, rw_ln_b, final_norm):
    batch, seq, _ = x.shape
    h = x.reshape(batch * seq, D_MODEL)
    fg = _row(final_norm)

    wu_all, wd_all = w_up.astype(BF16), w_down.astype(BF16)
    qa, ka, va, qkm, vm, om, gt = _inproj_even(h, _row(mix_norm[0]), _rope_tables(seq), e_w_in[0].astype(BF16), seq)
    gate_bias = _col(jnp.concatenate([ml_i_bias[0], ml_f_bias[0]]))
    h = _layer0(qa, ka, va, _row(att_sinks[0]), qkm, vm, om, gt, ml_conv_w[0].astype(F32), _row(ml_conv_b[0]),
                gate_bias, _row(ml_norm[0]), h, e_w_out[0].astype(BF16), _row(mlp_norm[0]), wu_all, wd_all, fg,
                batch, seq)

    w = o_w_in[0].astype(BF16)
    o_x = SSM_INNER
    o_dt = o_x + SSM_CONV_W
    o_rw = o_dt + SSM_HEADS
    parts = [w[:, :o_x], w[:, o_x:o_dt], w[:, o_dt:o_rw], w[:, o_rw:]]
    z, xbc, dt_t, rw = _inproj_odd(h, _row(mix_norm[1]), parts, seq)
    rw_consts = (_row(rw_mu[0]), _row(rw_w0[0]), rw_w2[0].astype(BF16), _row(rw_a0[0]), rw_a2[0].astype(BF16),
                 rw_g2[0].astype(BF16), _row(rw_k_k[0]), _row(rw_k_a[0]), _row(rw_r_k[0]), _row(rw_ln_w[0]),
                 _row(rw_ln_b[0]))
    y_rw = _rwkv(rw, rw_consts, batch, seq)
    wo = o_w_out[0].astype(BF16)
    h = _layer1(xbc, z, dt_t, ssm_conv_w[0].astype(F32), _row(ssm_conv_b[0]), _col(ssm_dt_bias[0]),
                _col(ssm_a_log[0]), _row(jnp.repeat(ssm_d[0], SSM_HEAD_DIM)), _row(ssm_norm[0]), y_rw, h,
                wo[:SSM_INNER], wo[SSM_INNER:], _row(mlp_norm[1]), wu_all, wd_all, fg, batch, seq)
    return h.reshape(batch, seq, D_MODEL)
```

```python
import functools

import numpy as np

import jax
import jax.numpy as jnp
from jax import lax
from jax.experimental import pallas as pl
from jax.experimental.pallas import tpu as pltpu

F32 = jnp.float32
BF16 = jnp.bfloat16

D_MODEL = 1024
D_FF = 4 * D_MODEL
NORM_EPS = 1e-5
CONV_TAPS = 4

ATT_HEAD_DIM = 64
ATT_Q_HEADS = 8
ATT_KV_HEADS = 2
ATT_GROUP = ATT_Q_HEADS // ATT_KV_HEADS
ATT_WINDOW = 128
ROPE_THETA = 500000.0
ROPE_DIM = 16
ATT_Q_W = ATT_Q_HEADS * ATT_HEAD_DIM
ATT_KV_W = ATT_KV_HEADS * ATT_HEAD_DIM

ML_HEADS = 4
ML_V_DIM = 128
ML_QK_DIM = 64
ML_CHUNK = 128
ML_NORM_EPS = 1e-6
ML_QK_W = ML_HEADS * ML_QK_DIM
ML_V_W = ML_HEADS * ML_V_DIM

SSM_HEAD_DIM = 64
SSM_HEADS = 8
SSM_GROUPS = 2
SSM_HEADS_PER_GROUP = SSM_HEADS // SSM_GROUPS
SSM_STATE = 128
SSM_CHUNK = 128
SSM_INNER = SSM_HEADS * SSM_HEAD_DIM
SSM_BC = SSM_GROUPS * SSM_STATE
SSM_CONV_W = SSM_INNER + 2 * SSM_BC

RW_HEAD_DIM = 64
RW_HEADS = 8
RW_W = RW_HEADS * RW_HEAD_DIM
RW_DECAY_LORA = 64
RW_AAA_LORA = 64
RW_GATE_LORA = 128
RW_IN = 3 * RW_W + RW_DECAY_LORA + RW_AAA_LORA + RW_GATE_LORA
RW_LN_EPS = 64e-5
RW_CHUNK = 64
RW_EXP_MINUS_HALF = float(np.exp(-0.5))

V7X_VMEM_BYTES = 64 * 1024 * 1024
VMEM_LIMIT_BYTES = V7X_VMEM_BYTES - 6 * 1024 * 1024
SUBLANES = 8
LANES = 128

PROJ_ROWS = 1024
FF_CHUNK = 512


def _params(*sem):
    return pltpu.CompilerParams(dimension_semantics=sem, vmem_limit_bytes=VMEM_LIMIT_BYTES)


def _const_spec(shape):
    nd = len(shape)
    return pl.BlockSpec(shape, lambda *_: (0,) * nd, pipeline_mode=pl.Buffered(1))


def _weight_spec(a, layer):
    if a.ndim == 2:
        return _const_spec(a.shape)
    return pl.BlockSpec((None,) + tuple(a.shape[1:]), lambda *_: (layer, 0, 0), pipeline_mode=pl.Buffered(1))


def _iota(shape, dim):
    return lax.broadcasted_iota(jnp.int32, shape, dim)


def _tri_lower(n):
    return (_iota((n, n), 1) <= _iota((n, n), 0)).astype(F32)


def _tri_upper(n):
    return (_iota((n, n), 0) <= _iota((n, n), 1)).astype(F32)


def _eye(n):
    return (_iota((n, n), 0) == _iota((n, n), 1)).astype(F32)


def _dot(a, b):
    return jnp.dot(a, b, preferred_element_type=F32)


def _dot_nt(a, b):
    return lax.dot_general(a, b, (((1,), (1,)), ((), ())), preferred_element_type=F32)


def _dot_tn(a, b):
    return lax.dot_general(a, b, (((0,), (1,)), ((), ())), preferred_element_type=F32)


def _split3_bf16(x):
    hi = x.astype(BF16)
    r1 = x - hi.astype(F32)
    mid = r1.astype(BF16)
    lo = (r1 - mid.astype(F32)).astype(BF16)
    return hi, mid, lo


def _sum3(terms):
    hi, mid, lo = terms
    return hi + (mid + lo)


def _sigmoid(x):
    return 1.0 / (1.0 + jnp.exp(-x))


def _softplus(x):
    return jnp.maximum(x, 0.0) + jnp.log1p(jnp.exp(-jnp.abs(x)))


def _silu(x):
    return x * _sigmoid(x)


def _rmsnorm(x, g):
    y = x * lax.rsqrt(jnp.mean(x * x, axis=-1, keepdims=True) + NORM_EPS)
    return y * g


def _shifted_rows(x, tail, k):
    rolled = pltpu.roll(x, k, axis=0)
    tail_rolled = pltpu.roll(tail, k, axis=0)
    first = jnp.where(_iota((SUBLANES, 1), 0) < k, tail_rolled, rolled[:SUBLANES])
    return jnp.concatenate([first, rolled[SUBLANES:]], axis=0)


def _causal_conv_silu(x, tail, w, b):
    out = b + x * w[CONV_TAPS - 1:CONV_TAPS, :]
    for k in range(1, CONV_TAPS):
        out = out + _shifted_rows(x, tail, k) * w[CONV_TAPS - 1 - k:CONV_TAPS - k, :]
    return _silu(out)


def _rope(x, cos, sin_lo, sin_hi, reps):
    w = x.shape[1]
    c = jnp.concatenate([cos] * reps, axis=1)
    s_lo = jnp.concatenate([sin_lo] * reps, axis=1)
    s_hi = jnp.concatenate([sin_hi] * reps, axis=1)
    half = ROPE_DIM // 2
    nxt = pltpu.roll(x, w - half, axis=1)
    prv = pltpu.roll(x, half, axis=1)
    return x * c + nxt * s_lo + prv * s_hi


EVEN_SEGMENTS = (ATT_Q_W, ATT_KV_W, ATT_KV_W, 2 * ML_QK_W, ML_V_W, ML_V_W, 2 * ML_HEADS)
EVEN_OFFSETS = tuple(sum(EVEN_SEGMENTS[:i]) for i in range(len(EVEN_SEGMENTS)))


def _inproj_even_kernel(x_ref, g_ref, cos_ref, slo_ref, shi_ref, w_ref,
                        qa_ref, ka_ref, va_ref, qkm_ref, vm_ref, om_ref, gt_ref):
    hb = _rmsnorm(x_ref[...], g_ref[...]).astype(BF16)
    cos, s_lo, s_hi = cos_ref[...], slo_ref[...], shi_ref[...]
    seg = lambda i: w_ref[:, EVEN_OFFSETS[i]:EVEN_OFFSETS[i] + EVEN_SEGMENTS[i]]
    q = _rope(_dot(hb, seg(0)), cos, s_lo, s_hi, ATT_Q_W // LANES)
    qa_ref[...] = (q * (ATT_HEAD_DIM ** -0.5)).astype(BF16)
    ka_ref[...] = _rope(_dot(hb, seg(1)), cos, s_lo, s_hi, ATT_KV_W // LANES).astype(BF16)
    va_ref[...] = _dot(hb, seg(2)).astype(BF16)
    qkm_ref[...] = _dot(hb, seg(3))
    vm_ref[...] = _dot(hb, seg(4)).astype(BF16)
    om_ref[...] = _dot(hb, seg(5))
    gt_ref[...] = _dot_tn(seg(6), hb)


def _inproj_even(x2, g, tables, w, seq):
    n = x2.shape[0]
    tm = min(PROJ_ROWS, seq)
    per_seq = seq // tm
    row = lambda width: pl.BlockSpec((tm, width), lambda i: (i, 0))
    tab = pl.BlockSpec((tm, LANES), lambda i: (i % per_seq, 0))
    out_shapes = (
        jax.ShapeDtypeStruct((n, ATT_Q_W), BF16), jax.ShapeDtypeStruct((n, ATT_KV_W), BF16),
        jax.ShapeDtypeStruct((n, ATT_KV_W), BF16), jax.ShapeDtypeStruct((n, 2 * ML_QK_W), F32),
        jax.ShapeDtypeStruct((n, ML_V_W), BF16), jax.ShapeDtypeStruct((n, ML_V_W), F32),
        jax.ShapeDtypeStruct((n // seq, 2 * ML_HEADS, seq), F32))
    return pl.pallas_call(
        _inproj_even_kernel,
        grid=(n // tm,),
        in_specs=[row(D_MODEL), _const_spec((1, D_MODEL)), tab, tab, tab, _const_spec(w.shape)],
        out_specs=(row(ATT_Q_W), row(ATT_KV_W), row(ATT_KV_W), row(2 * ML_QK_W), row(ML_V_W), row(ML_V_W),
                   pl.BlockSpec((None, 2 * ML_HEADS, tm), lambda i: (i // per_seq, 0, i % per_seq))),
        out_shape=out_shapes,
        compiler_params=_params("arbitrary"),
        name="inproj_even",
    )(x2, g, *tables, w)


def _attention_stages(q_ref, kc_ref, kp_ref, vc_ref, vp_ref, sink_ref, blk, nb, emit):
    w = ATT_WINDOW
    dh = ATT_HEAD_DIM
    G = ATT_GROUP
    i = _iota((G * w, 2 * w), 0) % w
    j = _iota((G * w, 2 * w), 1)
    mask = (j > i) & (j <= i + w) & ((j >= w) | (blk > 0))
    sinks = sink_ref[...]
    problems = [(b, g) for b in range(nb) for g in range(ATT_KV_HEADS)]
    scores, values, sink_cols = [], [], []
    for b, g in problems:
        q = q_ref[b]
        qs = jnp.concatenate([q[:, (g * G + a) * dh:(g * G + a + 1) * dh] for a in range(G)], axis=0)
        ks = jnp.concatenate([kp_ref[b][:, g * dh:(g + 1) * dh], kc_ref[b][:, g * dh:(g + 1) * dh]], axis=0)
        values.append(jnp.concatenate([vp_ref[b][:, g * dh:(g + 1) * dh], vc_ref[b][:, g * dh:(g + 1) * dh]], axis=0))
        scores.append(_dot_nt(qs, ks))
        sink_cols.append(jnp.concatenate(
            [jnp.broadcast_to(sinks[:, g * G + a:g * G + a + 1], (w, 1)) for a in range(G)], axis=0))
    yield
    probs, dens = [], []
    for s, sink in zip(scores, sink_cols):
        s = jnp.where(mask, s, -jnp.inf)
        m = jnp.maximum(jnp.max(s, axis=-1, keepdims=True), sink)
        p = jnp.exp(s - m)
        dens.append(jnp.sum(p, axis=-1, keepdims=True) + jnp.exp(sink - m))
        probs.append(p.astype(BF16))
    pv = [_dot(p, v) for p, v in zip(probs, values)]
    yield
    outs = [o / d for o, d in zip(pv, dens)]
    for b in range(nb):
        heads = [outs[b * ATT_KV_HEADS + g][a * w:(a + 1) * w] for g in range(ATT_KV_HEADS) for a in range(G)]
        emit(b, jnp.concatenate(heads, axis=1).astype(BF16))


def _mlstm_stages(qk_ref, tail_ref, v_ref, o_ref, gt_ref, cw_ref, cb_ref, gb_ref, nw_ref, c_s, m_s, chunk, nb, emit):
    L = ML_CHUNK
    H = ML_HEADS
    DV = ML_V_DIM
    NG = 2 * H

    cw, cb = cw_ref[...], cb_ref[...]
    qk = [_causal_conv_silu(qk_ref[b], jnp.where(chunk == 0, 0.0, tail_ref[b]), cw, cb) for b in range(nb)]

    g = jnp.concatenate([gt_ref[b] + gb_ref[...] for b in range(nb)], axis=0)
    lg = jnp.where(_iota(g.shape, 0) % NG < H, g, -_softplus(-g))
    tri_l, tri_u, eye = _tri_lower(L).astype(BF16), _tri_upper(L).astype(BF16), _eye(L).astype(BF16)
    lg3 = _split3_bf16(lg)
    cum_row = _sum3([_dot(p, tri_u) for p in lg3])
    lg_col = _sum3([_dot_nt(eye, p) for p in lg3])
    cum_col = _sum3([_dot_nt(tri_l, p) for p in lg3])
    causal = _iota((L, L), 1) <= _iota((L, L), 0)

    chains = [(b, h) for b in range(nb) for h in range(H)]
    idx = lambda b, h: b * H + h
    q_c = [qk[b][:, h * ML_QK_DIM:(h + 1) * ML_QK_DIM] for b, h in chains]
    k_c = [qk[b][:, ML_QK_W + h * ML_QK_DIM:ML_QK_W + (h + 1) * ML_QK_DIM] * (ML_QK_DIM ** -0.5) for b, h in chains]
    ones = jnp.ones((L, DV), BF16)
    v_c = [jnp.concatenate([v_ref[b][:, h * DV:(h + 1) * DV], ones], axis=1) for b, h in chains]
    qb_c = [q.astype(BF16) for q in q_c]
    c_prev = [c_s[idx(b, h)] for b, h in chains]
    m_prev = [m_s[idx(b, h):idx(b, h) + 1, 0:1] for b, h in chains]

    s_qk = [_dot_nt(qb, k.astype(BF16)) for qb, k in zip(qb_c, k_c)]
    inter = [_dot(qb, c.astype(BF16)) for qb, c in zip(qb_c, c_prev)]

    yield
    gi_c = [b * NG + h for b, h in chains]
    b_col = [cum_col[:, gi + H:gi + H + 1] for gi in gi_c]
    b_last = [bc[L - 1:L, :] for bc in b_col]
    dlog = [jnp.where(causal, bc - cum_row[gi + H:gi + H + 1, :] + lg[gi:gi + 1, :], -jnp.inf)
            for gi, bc in zip(gi_c, b_col)]
    row_max = [jnp.max(d, axis=1, keepdims=True) for d in dlog]
    a_loc = [bl - bc + lg_col[:, gi:gi + 1] for gi, bc, bl in zip(gi_c, b_col, b_last)]
    m_loc = [jnp.max(a, axis=0, keepdims=True) for a in a_loc]
    inter_log = [bc + mp for bc, mp in zip(b_col, m_prev)]
    m_t = [jnp.maximum(il, rm) for il, rm in zip(inter_log, row_max)]
    qk_w = [s * jnp.exp(d - mt) for s, d, mt in zip(s_qk, dlog, m_t)]
    w_inter = [jnp.exp(il - mt) for il, mt in zip(inter_log, m_t)]
    kw = [k * jnp.exp(a - ml) for k, a, ml in zip(k_c, a_loc, m_loc)]

    intra = [_dot(w.astype(BF16), v) for w, v in zip(qk_w, v_c)]
    c_loc = [lax.dot_general(w.astype(BF16), v, (((0,), (0,)), ((), ())), preferred_element_type=F32)
             for w, v in zip(kw, v_c)]

    yield
    both = [a + w * e for a, w, e in zip(intra, w_inter, inter)]
    xs = [nd[:, :DV] / jnp.maximum(jnp.abs(nd[:, DV:DV + 1]), jnp.exp(-mt)) for nd, mt in zip(both, m_t)]
    mus = [jnp.mean(x, axis=-1, keepdims=True) for x in xs]
    cen = [x - mu for x, mu in zip(xs, mus)]
    var = [jnp.mean(jnp.square(c), axis=-1, keepdims=True) for c in cen]
    outs = [c * lax.rsqrt(vr + ML_NORM_EPS) for c, vr in zip(cen, var)]
    for i, (b, h) in enumerate(chains):
        m_new = jnp.maximum(b_last[i] + m_prev[i], m_loc[i])
        fa = jnp.exp(b_last[i] + m_prev[i] - m_new)
        fb = jnp.exp(m_loc[i] - m_new)
        c_s[idx(b, h)] = fa * c_prev[i] + fb * c_loc[i]
        m_s[idx(b, h):idx(b, h) + 1, 0:1] = m_new

    nw = nw_ref[...]
    for b in range(nb):
        hm = jnp.concatenate(outs[b * H:(b + 1) * H], axis=1) * nw
        emit(b, (hm * _sigmoid(o_ref[b])).astype(BF16))


def _outproj_mlp_stages(y_parts, h, g_ref, wu_ref, wd_ref, fg_ref, final_norm, emit):
    h1 = h
    for y, w_ref in y_parts:
        h1 = h1 + _dot(y, w_ref[...])
    hb = _rmsnorm(h1, g_ref[...]).astype(BF16)
    n_chunks = D_FF // FF_CHUNK
    mlp = None
    for c in range(n_chunks):
        u = _dot(hb, wu_ref[:, c * FF_CHUNK:(c + 1) * FF_CHUNK])
        a = jnp.square(jnp.maximum(u, 0.0)).astype(BF16)
        d = _dot(a, wd_ref[c * FF_CHUNK:(c + 1) * FF_CHUNK, :])
        mlp = d if mlp is None else mlp + d
        if c + 1 in MLP_SPLIT:
            yield
    h2 = h1 + mlp
    if final_norm:
        h2 = _rmsnorm(h2, fg_ref[...])
    emit(h2)


MLP_SPLIT = (5,)


def _drive(mixers, mlp):
    for m in mixers:
        next(m)
    next(mlp)
    for m in mixers:
        next(m)
    for m in mixers:
        for _ in m:
            pass
    for _ in mlp:
        pass


def _layer0_kernel(q_ref, kc_ref, kp_ref, vc_ref, vp_ref, sink_ref,
                   qk_ref, tail_ref, vm_ref, om_ref, gt_ref, cw_ref, cb_ref, gb_ref, nw_ref,
                   h_ref, wo_ref, g_ref, wu_ref, wd_ref, fg_ref,
                   o_ref, c_s, m_s, y_s, *, nb, nc):
    step = pl.program_id(0)
    chunk = jnp.minimum(step, nc - 1)
    L = ML_CHUNK

    @pl.when(step == 0)
    def _():
        c_s[...] = jnp.zeros_like(c_s)
        m_s[...] = jnp.zeros_like(m_s)
        y_s[...] = jnp.zeros_like(y_s)

    wr = step % 2
    rd = 1 - wr

    def emit_att(b, block):
        y_s[wr, b * L:(b + 1) * L, 0:ATT_Q_W] = block

    def emit_ml(b, block):
        y_s[wr, b * L:(b + 1) * L, ATT_Q_W:ATT_Q_W + ML_V_W] = block

    def emit_out(block):
        o_ref[...] = block.reshape(nb, L, D_MODEL)

    mixers = [
        _attention_stages(q_ref, kc_ref, kp_ref, vc_ref, vp_ref, sink_ref, chunk, nb, emit_att),
        _mlstm_stages(qk_ref, tail_ref, vm_ref, om_ref, gt_ref, cw_ref, cb_ref, gb_ref, nw_ref, c_s, m_s, chunk, nb,
                      emit_ml),
    ]
    mlp = _outproj_mlp_stages([(y_s[rd], wo_ref)], h_ref[...].reshape(nb * L, D_MODEL), g_ref, wu_ref, wd_ref,
                              fg_ref, False, emit_out)
    _drive(mixers, mlp)


def _layer0(qa, ka, va, sinks, qkm, vm, om, gt, conv_w, conv_b, gate_bias, norm_w, h, wo, g, wu, wd, fg, batch, seq):
    L = ML_CHUNK
    nc = seq // L
    per8 = L // SUBLANES
    three = lambda t: t.reshape(batch, seq, t.shape[-1])
    mix = lambda c: jnp.minimum(c, nc - 1)
    cur = lambda width: pl.BlockSpec((batch, L, width), lambda c: (0, mix(c), 0))
    prev = lambda width: pl.BlockSpec((batch, L, width), lambda c: (0, jnp.maximum(mix(c) - 1, 0), 0))
    tail = pl.BlockSpec((batch, SUBLANES, 2 * ML_QK_W), lambda c: (0, jnp.maximum(mix(c) * per8 - 1, 0), 0))
    late = pl.BlockSpec((batch, L, D_MODEL), lambda c: (0, jnp.maximum(c - 1, 0), 0))
    q3, k3, v3, qk3 = three(qa), three(ka), three(va), three(qkm)
    consts = (conv_w, conv_b, gate_bias, norm_w)
    weights = (wo, g, wu, wd, fg)
    out = pl.pallas_call(
        functools.partial(_layer0_kernel, nb=batch, nc=nc),
        grid=(nc + 1,),
        in_specs=[cur(ATT_Q_W), cur(ATT_KV_W), prev(ATT_KV_W), cur(ATT_KV_W), prev(ATT_KV_W),
                  _const_spec((1, ATT_Q_HEADS)),
                  cur(2 * ML_QK_W), tail, cur(ML_V_W), cur(ML_V_W),
                  pl.BlockSpec((batch, 2 * ML_HEADS, L), lambda c: (0, 0, mix(c)))]
                 + [_const_spec(a.shape) for a in consts] + [late] + [_weight_spec(a, 0) for a in weights],
        out_specs=late,
        out_shape=jax.ShapeDtypeStruct((batch, seq, D_MODEL), F32),
        scratch_shapes=[pltpu.VMEM((batch * ML_HEADS, ML_QK_DIM, 2 * ML_V_DIM), F32),
                        pltpu.VMEM((batch * ML_HEADS, LANES), F32),
                        pltpu.VMEM((2, batch * L, ATT_Q_W + ML_V_W), BF16)],
        compiler_params=_params("arbitrary"),
        name="layer0_mix_mlp",
    )(q3, k3, k3, v3, v3, sinks, qk3, qk3, three(vm), three(om), gt, *consts, three(h), *weights)
    return out.reshape(batch * seq, D_MODEL)


def _inproj_odd_kernel(x_ref, g_ref, wz_ref, wx_ref, wdt_ref, wrw_ref, z_ref, xbc_ref, dt_ref, rw_ref):
    hb = _rmsnorm(x_ref[...], g_ref[...]).astype(BF16)
    z_ref[...] = _dot(hb, wz_ref[...])
    xbc_ref[...] = _dot(hb, wx_ref[...])
    dt_ref[...] = _dot_tn(wdt_ref[...], hb)
    rw_ref[...] = _dot(hb, wrw_ref[...])


def _inproj_odd(x2, g, w, seq):
    n = x2.shape[0]
    tm = min(PROJ_ROWS, seq)
    per_seq = seq // tm
    row = lambda width: pl.BlockSpec((tm, width), lambda i: (i, 0))
    out_shapes = (jax.ShapeDtypeStruct((n, SSM_INNER), F32), jax.ShapeDtypeStruct((n, SSM_CONV_W), F32),
                  jax.ShapeDtypeStruct((n // seq, SSM_HEADS, seq), F32), jax.ShapeDtypeStruct((n, RW_IN), F32))
    return pl.pallas_call(
        _inproj_odd_kernel,
        grid=(n // tm,),
        in_specs=[row(D_MODEL), _const_spec((1, D_MODEL))] + [_const_spec(a.shape) for a in w],
        out_specs=(row(SSM_INNER), row(SSM_CONV_W),
                   pl.BlockSpec((None, SSM_HEADS, tm), lambda i: (i // per_seq, 0, i % per_seq)), row(RW_IN)),
        out_shape=out_shapes,
        compiler_params=_params("arbitrary"),
        name="inproj_odd",
    )(x2, g, *w)


def _ssd_stages(xbc_ref, tail_ref, z_ref, dt_ref, cw_ref, cb_ref, dtb_ref, alog_ref, d_ref, nw_ref, s_s, chunk, nb,
                emit):
    L = SSM_CHUNK
    P = SSM_HEAD_DIM
    HG = SSM_HEADS_PER_GROUP
    NH = SSM_HEADS

    cw, cb_ = cw_ref[...], cb_ref[...]
    xbc = [_causal_conv_silu(xbc_ref[b], jnp.where(chunk == 0, 0.0, tail_ref[b]), cw, cb_) for b in range(nb)]

    dt_row = _softplus(jnp.concatenate([dt_ref[b] + dtb_ref[...] for b in range(nb)], axis=0))
    neg_a = -jnp.exp(jnp.concatenate([alog_ref[...]] * nb, axis=0))
    a_row = dt_row * neg_a
    tri_l, tri_u, eye_b = _tri_lower(L).astype(BF16), _tri_upper(L).astype(BF16), _eye(L).astype(BF16)
    a3 = _split3_bf16(a_row)
    acum_row = _sum3([_dot(p, tri_u) for p in a3])
    acum_col = _sum3([_dot_nt(tri_l, p) for p in a3])
    dt_col = _sum3([_dot_nt(eye_b, p) for p in _split3_bf16(dt_row)])
    causal = _iota((L, L), 1) <= _iota((L, L), 0)

    groups = [(b, g) for b in range(nb) for g in range(SSM_GROUPS)]
    gidx = lambda b, g: b * SSM_GROUPS + g
    bg = [xbc[b][:, SSM_INNER + g * SSM_STATE:SSM_INNER + (g + 1) * SSM_STATE].astype(BF16) for b, g in groups]
    cg = [xbc[b][:, SSM_INNER + SSM_BC + g * SSM_STATE:SSM_INNER + SSM_BC + (g + 1) * SSM_STATE].astype(BF16)
          for b, g in groups]
    s_prev = [s_s[gidx(b, g)] for b, g in groups]

    cb = [_dot_nt(c, bq) for c, bq in zip(cg, bg)]
    bg_t = [_dot_nt(eye_b, bq).astype(BF16) for bq in bg]
    y_off = [_dot(c, sp.astype(BF16)) for c, sp in zip(cg, s_prev)]

    yield
    heads = [(b, g, j) for b, g in groups for j in range(HG)]
    scores, xcs, e_col, xws, cds = [], [], [], [], []
    for b, g, j in heads:
        h = g * HG + j
        col = b * NH + h
        a_col = acum_col[:, col:col + 1]
        a_last = a_col[L - 1:L, :]
        decay = jnp.exp(jnp.where(causal, a_col - acum_row[col:col + 1, :], -jnp.inf))
        xc = xbc[b][:, h * P:(h + 1) * P] * dt_col[:, col:col + 1]
        scores.append((cb[gidx(b, g)] * decay).astype(BF16))
        xcs.append(xc.astype(BF16))
        e_col.append(jnp.exp(a_col))
        xws.append(xc * jnp.exp(a_last - a_col))
        cds.append(jnp.broadcast_to(jnp.exp(a_last), (1, P)))

    y_diag = [_dot(sc, xc) for sc, xc in zip(scores, xcs)]
    states = [_dot(bg_t[i], jnp.concatenate(xws[i * HG:(i + 1) * HG], axis=1).astype(BF16))
              for i in range(len(groups))]

    yield
    for i, (b, g) in enumerate(groups):
        s_s[gidx(b, g)] = s_prev[i] * jnp.concatenate(cds[i * HG:(i + 1) * HG], axis=1) + states[i]

    gw = SSM_INNER // SSM_GROUPS
    for b in range(nb):
        ys = []
        for g in range(SSM_GROUPS):
            for j in range(HG):
                i = (b * SSM_GROUPS + g) * HG + j
                ys.append(y_diag[i] + y_off[gidx(b, g)][:, j * P:(j + 1) * P] * e_col[i])
        xs = xbc[b][:, :SSM_INNER]
        y = jnp.concatenate(ys, axis=1) + d_ref[...] * xs
        y = y * _silu(z_ref[b])
        normed = []
        for g in range(SSM_GROUPS):
            yg = y[:, g * gw:(g + 1) * gw]
            normed.append(yg * lax.rsqrt(jnp.mean(yg * yg, axis=-1, keepdims=True) + NORM_EPS))
        emit(b, (jnp.concatenate(normed, axis=1) * nw_ref[...]).astype(BF16))


def _layer1_kernel(xbc_ref, tail_ref, z_ref, dt_ref, cw_ref, cb_ref, dtb_ref, alog_ref, d_ref, nw_ref,
                   yrw_ref, h_ref, wa_ref, wb_ref, g_ref, wu_ref, wd_ref, fg_ref,
                   o_ref, s_s, y_s, *, nb, nc):
    step = pl.program_id(0)
    chunk = jnp.minimum(step, nc - 1)
    L = SSM_CHUNK

    @pl.when(step == 0)
    def _():
        s_s[...] = jnp.zeros_like(s_s)
        y_s[...] = jnp.zeros_like(y_s)

    wr = step % 2
    rd = 1 - wr

    def emit_ssm(b, block):
        y_s[wr, b * L:(b + 1) * L, :] = block

    def emit_out(block):
        o_ref[...] = block.reshape(nb, L, D_MODEL)

    mixers = [_ssd_stages(xbc_ref, tail_ref, z_ref, dt_ref, cw_ref, cb_ref, dtb_ref, alog_ref, d_ref, nw_ref, s_s,
                          chunk, nb, emit_ssm)]
    mlp = _outproj_mlp_stages([(y_s[rd], wa_ref), (yrw_ref[...].reshape(nb * L, RW_W), wb_ref)],
                              h_ref[...].reshape(nb * L, D_MODEL), g_ref, wu_ref, wd_ref, fg_ref, True, emit_out)
    _drive(mixers, mlp)


def _layer1(xbc, z, dt_t, conv_w, conv_b, dt_bias, a_log, d_row, norm_w, y_rw, h, wa, wb, g, wu, wd, fg, batch, seq):
    L = SSM_CHUNK
    nc = seq // L
    per8 = L // SUBLANES
    three = lambda t: t.reshape(batch, seq, t.shape[-1])
    mix = lambda c: jnp.minimum(c, nc - 1)
    cur = lambda width: pl.BlockSpec((batch, L, width), lambda c: (0, mix(c), 0))
    tail = pl.BlockSpec((batch, SUBLANES, SSM_CONV_W), lambda c: (0, jnp.maximum(mix(c) * per8 - 1, 0), 0))
    late = lambda width: pl.BlockSpec((batch, L, width), lambda c: (0, jnp.maximum(c - 1, 0), 0))
    consts = (conv_w, conv_b, dt_bias, a_log, d_row, norm_w)
    weights = (wa, wb, g, wu, wd, fg)
    x3 = three(xbc)
    out = pl.pallas_call(
        functools.partial(_layer1_kernel, nb=batch, nc=nc),
        grid=(nc + 1,),
        in_specs=[cur(SSM_CONV_W), tail, cur(SSM_INNER),
                  pl.BlockSpec((batch, SSM_HEADS, L), lambda c: (0, 0, mix(c)))]
                 + [_const_spec(a.shape) for a in consts] + [late(RW_W), late(D_MODEL)]
                 + [_weight_spec(a, 1) for a in weights],
        out_specs=late(D_MODEL),
        out_shape=jax.ShapeDtypeStruct((batch, seq, D_MODEL), F32),
        scratch_shapes=[pltpu.VMEM((batch * SSM_GROUPS, SSM_STATE, SSM_HEADS_PER_GROUP * SSM_HEAD_DIM), F32),
                        pltpu.VMEM((2, batch * L, SSM_INNER), BF16)],
        compiler_params=_params("arbitrary"),
        name="layer1_mix_mlp",
    )(x3, x3, three(z), dt_t, *consts, three(y_rw), three(h), *weights)
    return out.reshape(batch * seq, D_MODEL)


def _split_bf16(x):
    hi = x.astype(BF16)
    lo = (x - hi.astype(F32)).astype(BF16)
    return hi, lo


def _dot_ones_exact(t, ones_matrix):
    hi, lo = _split_bf16(t)
    return _dot(hi, ones_matrix) + _dot(lo, ones_matrix)


def _unit_lower_inverse_steps(mats, eye, n):
    xbs = [(eye - a).astype(BF16) for a in mats]
    mbs = [(eye + a).astype(BF16) for a in mats]
    two_eye = 2.0 * eye
    terms = 2
    while terms < n:
        terms *= 2
        gbs = [(two_eye - _dot(m, xb)).astype(BF16) for m, xb in zip(mbs, xbs)]
        xbs = [_dot(xb, g).astype(BF16) for xb, g in zip(xbs, gbs)]
        yield xbs


RW_FRONT_STAGES = 4
RW_CHAIN_STAGES = 10


def _rwkv_stages(x, before, mu_ref, w0_ref, w2_ref, a0_ref, a2_ref, g2_ref, kk_ref, ka_ref, rk_ref, lnw_ref, lnb_ref,
                 st_s, nb, emit):
    C = RW_CHUNK
    D = RW_HEAD_DIM
    R = nb * C
    rolled = pltpu.roll(x, 1, axis=0)
    first_row = _iota((C, 1), 0) == 0
    prev = jnp.concatenate([jnp.where(first_row, before[b], rolled[b * C:(b + 1) * C]) for b in range(nb)], axis=0)
    xm = x + (prev - x) * mu_ref[...]
    r = xm[:, 0:RW_W]
    k = xm[:, RW_W:2 * RW_W]
    v = xm[:, 2 * RW_W:3 * RW_W]
    o = 3 * RW_W
    wd = xm[:, o:o + RW_DECAY_LORA]
    ad = xm[:, o + RW_DECAY_LORA:o + RW_DECAY_LORA + RW_AAA_LORA]
    gd = xm[:, o + RW_DECAY_LORA + RW_AAA_LORA:]
    yield

    z = w0_ref[...] + _dot(jnp.tanh(wd).astype(BF16), w2_ref[...])
    ld = -RW_EXP_MINUS_HALF * _sigmoid(z)
    a = _sigmoid(a0_ref[...] + _dot(ad.astype(BF16), a2_ref[...]))
    gate = _dot(_sigmoid(gd).astype(BF16), g2_ref[...])
    yield

    half = RW_W // 2
    same_head = (_iota((half, half), 0) // D == _iota((half, half), 1) // D).astype(BF16)
    head_sum = lambda t: jnp.concatenate(
        [_dot_ones_exact(t[:, :half], same_head), _dot_ones_exact(t[:, half:], same_head)], axis=1)

    kk = k * kk_ref[...]
    kk = kk * lax.rsqrt(jnp.maximum(head_sum(kk * kk), 1e-24))
    k2 = k * (1.0 + (a - 1.0) * ka_ref[...])
    bvec = kk * a
    yield

    tri = ((_iota((R, R), 1) <= _iota((R, R), 0)) & (_iota((R, R), 1) // C == _iota((R, R), 0) // C)).astype(BF16)
    cl = _sum3([_dot(tri, p) for p in _split3_bf16(ld)])
    c_last = jnp.concatenate(
        [jnp.broadcast_to(cl[(b + 1) * C - 1:(b + 1) * C, :], (C, RW_W)) for b in range(nb)], axis=0)
    e_neg = jnp.exp(-cl)
    e_end = jnp.exp(c_last - cl)
    w_end = [jnp.exp(cl[(b + 1) * C - 1:(b + 1) * C, :]) for b in range(nb)]
    rw_ = r * jnp.exp(cl)
    kkw = kk * jnp.exp(cl - ld)
    bd = bvec * e_neg
    kd = k2 * e_neg
    bd_end = bvec * e_end
    kd_end = k2 * e_end
    yield

    g_row = _iota((2 * C, 2 * C), 0)
    g_col = _iota((2 * C, 2 * C), 1) % C
    keep = g_col < jnp.where(g_row < C, g_row, g_row - C + 1)
    eye = _eye(C)
    chains = [(b, h) for b in range(nb) for h in range(RW_HEADS)]
    blk = lambda t, b, h: t[b * C:(b + 1) * C, h * D:(h + 1) * D]
    kkw_c = [blk(kkw, b, h).astype(BF16) for b, h in chains]
    rw_c = [blk(rw_, b, h).astype(BF16) for b, h in chains]
    vb_c = [blk(v, b, h).astype(BF16) for b, h in chains]
    gms = [jnp.where(keep, _dot_nt(jnp.concatenate([kq, rq], axis=0),
                                   jnp.concatenate([blk(bd, b, h), blk(kd, b, h)], axis=0).astype(BF16)), 0.0)
           for (b, h), kq, rq in zip(chains, kkw_c, rw_c)]
    yield
    a_v = [_dot(g[:, C:].astype(BF16), vb) for g, vb in zip(gms, vb_c)]
    a_rb = [g[C:, :C].astype(BF16) for g in gms]
    yield
    invs = None
    for invs in _unit_lower_inverse_steps([g[:C, :C] for g in gms], eye, C):
        yield
    tws = [_dot(inv, jnp.concatenate([av[:C].astype(BF16), kq], axis=1))
           for inv, av, kq in zip(invs, a_v, kkw_c)]
    yield
    sts = [st_s[b * RW_HEADS + h] for b, h in chains]
    xss = [_dot_nt(jnp.concatenate([tw[:, D:].astype(BF16), rq], axis=0), st.astype(BF16))
           for tw, rq, st in zip(tws, rw_c, sts)]
    yield
    ubs =[(-tw[:, :D] - xs[:C]).astype(BF16) for tw, xs in zip(tws, xss)]
    ys = [xs[C:] + _dot(m, ub) + av[C:] for xs, m, ub, av in zip(xss, a_rb, ubs, a_v)]
    for (b, h), st, ub, vb in zip(chains, sts, ubs, vb_c):
        ends = jnp.concatenate([blk(bd_end, b, h), blk(kd_end, b, h)], axis=0).astype(BF16)
        upd = lax.dot_general(jnp.concatenate([ub, vb], axis=0), ends, (((0,), (0,)), ((), ())),
                              preferred_element_type=F32)
        st_s[b * RW_HEADS + h] = st * w_end[b][:, h * D:(h + 1) * D] + upd
    yield

    y = jnp.concatenate([jnp.concatenate(ys[b * RW_HEADS:(b + 1) * RW_HEADS], axis=1) for b in range(nb)], axis=0)
    mu = head_sum(y) * (1.0 / D)
    var = head_sum(jnp.square(y - mu)) * (1.0 / D)
    yn = (y - mu) * lax.rsqrt(var + RW_LN_EPS) * lnw_ref[...] + lnb_ref[...]
    bonus = head_sum(r * k2 * rk_ref[...]) * v
    emit(((yn + bonus) * gate).astype(BF16))


RW_STEP_CHUNKS = 4


def _rwkv_kernel(rw_ref, tail_ref, mu_ref, w0_ref, w2_ref, a0_ref, a2_ref, g2_ref, kk_ref, ka_ref, rk_ref,
                 lnw_ref, lnb_ref, y_ref, st_s, *, nb):
    step = pl.program_id(0)
    C = RW_CHUNK

    @pl.when(step == 0)
    def _():
        st_s[...] = jnp.zeros_like(st_s)

    xs = [rw_ref[:, j * C:(j + 1) * C, :].reshape(nb * C, RW_IN) for j in range(RW_STEP_CHUNKS)]
    befores = [[jnp.where(step == 0, 0.0, tail_ref[b, SUBLANES - 1:SUBLANES, :]) for b in range(nb)]]
    befores += [[xs[j][(b + 1) * C - 1:(b + 1) * C, :] for b in range(nb)] for j in range(RW_STEP_CHUNKS - 1)]

    def emit(j):
        def write(block):
            y_ref[:, j * C:(j + 1) * C, :] = block.reshape(nb, C, RW_W)
        return write

    args = (mu_ref, w0_ref, w2_ref, a0_ref, a2_ref, g2_ref, kk_ref, ka_ref, rk_ref, lnw_ref, lnb_ref, st_s, nb)
    gens = [_rwkv_stages(xs[j], befores[j], *args, emit(j)) for j in range(RW_STEP_CHUNKS)]
    for g in gens:
        for _ in range(RW_FRONT_STAGES):
            next(g)
    for j, g in enumerate(gens):
        for _ in range(RW_CHAIN_STAGES - (1 if j else 0)):
            next(g)
        if j + 1 < RW_STEP_CHUNKS:
            next(gens[j + 1])
        for _ in g:
            pass


def _rwkv(rw, consts, batch, seq):
    C = RW_CHUNK * RW_STEP_CHUNKS
    nc = seq // C
    per8 = C // SUBLANES
    rw3 = rw.reshape(batch, seq, RW_IN)
    y = pl.pallas_call(
        functools.partial(_rwkv_kernel, nb=batch),
        grid=(nc,),
        in_specs=[pl.BlockSpec((batch, C, RW_IN), lambda c: (0, c, 0)),
                  pl.BlockSpec((batch, SUBLANES, RW_IN), lambda c: (0, jnp.maximum(c * per8 - 1, 0), 0))]
                 + [_const_spec(a.shape) for a in consts],
        out_specs=pl.BlockSpec((batch, C, RW_W), lambda c: (0, c, 0)),
        out_shape=jax.ShapeDtypeStruct((batch, seq, RW_W), BF16),
        scratch_shapes=[pltpu.VMEM((batch * RW_HEADS, RW_HEAD_DIM, RW_HEAD_DIM), F32)],
        compiler_params=_params("arbitrary"),
        name="rwkv7",
    )(rw3, rw3, *consts)
    return y.reshape(batch * seq, RW_W)


@functools.lru_cache(maxsize=None)
def _rope_tables(seq):
    half = ROPE_DIM // 2
    inv_freq = ROPE_THETA ** (-np.arange(half, dtype=np.float64) * 2.0 / ROPE_DIM)
    ang = np.arange(seq, dtype=np.float64)[:, None] * inv_freq[None, :]
    cos, sin = np.cos(ang), np.sin(ang)
    ones = np.ones((seq, ATT_HEAD_DIM - ROPE_DIM))
    zeros = np.zeros((seq, ATT_HEAD_DIM - ROPE_DIM))
    zh = np.zeros((seq, half))
    reps = LANES // ATT_HEAD_DIM
    cos_t = np.concatenate([cos, cos, ones] * reps, axis=1).astype(np.float32)
    sin_lo = np.concatenate([-sin, zh, zeros] * reps, axis=1).astype(np.float32)
    sin_hi = np.concatenate([zh, sin, zeros] * reps, axis=1).astype(np.float32)
    return cos_t, sin_lo, sin_hi


def _row(v):
    return v.reshape(1, -1).astype(F32)


def _col(v):
    return v.reshape(-1, 1).astype(F32)


def kernel(x, mix_norm, mlp_norm, w_up, w_down, e_w_in, e_w_out, att_sinks, ml_conv_w, ml_conv_b, ml_i_bias,
           ml_f_bias, ml_norm, o_w_in, o_w_out, ssm_conv_w, ssm_conv_b, ssm_dt_bias, ssm_a_log, ssm_d, ssm_norm,
           rw_mu, rw_w0, rw_w2, rw_a0, rw_a2, rw_g2, rw_k_k, rw_k_a, rw_r_k, rw_ln_w, rw_ln_b, final_norm):
    batch, seq, _ = x.shape
    h = x.reshape(batch * seq, D_MODEL)
    fg = _row(final_norm)

    wu_all, wd_all = w_up.astype(BF16), w_down.astype(BF16)
    qa, ka, va, qkm, vm, om, gt = _inproj_even(h, _row(mix_norm[0]), _rope_tables(seq), e_w_in[0].astype(BF16), seq)
    gate_bias = _col(jnp.concatenate([ml_i_bias[0], ml_f_bias[0]]))
    h = _layer0(qa, ka, va, _row(att_sinks[0]), qkm, vm, om, gt, ml_conv_w[0].astype(F32), _row(ml_conv_b[0]),
                gate_bias, _row(ml_norm[0]), h, e_w_out[0].astype(BF16), _row(mlp_norm[0]), wu_all, wd_all, fg,
                batch, seq)

    w = o_w_in[0].astype(BF16)
    o_x = SSM_INNER
    o_dt = o_x + SSM_CONV_W
    o_rw = o_dt + SSM_HEADS
    parts = [w[:, :o_x], w[:, o_x:o_dt], w[:, o_dt:o_rw], w[:, o_rw:]]
    z, xbc, dt_t, rw = _inproj_odd(h, _row(mix_norm[1]), parts, seq)
    rw_consts = (_row(rw_mu[0]), _row(rw_w0[0]), rw_w2[0].astype(BF16), _row(rw_a0[0]), rw_a2[0].astype(BF16),
                 rw_g2[0].astype(BF16), _row(rw_k_k[0]), _row(rw_k_a[0]), _row(rw_r_k[0]), _row(rw_ln_w[0]),
                 _row(rw_ln_b[0]))
    y_rw = _rwkv(rw, rw_consts, batch, seq)
    wo = o_w_out[0].astype(BF16)
    h = _layer1(xbc, z, dt_t, ssm_conv_w[0].astype(F32), _row(ssm_conv_b[0]), _col(ssm_dt_bias[0]),
                _col(ssm_a_log[0]), _row(jnp.repeat(ssm_d[0], SSM_HEAD_DIM)), _row(ssm_norm[0]), y_rw, h,
                wo[:SSM_INNER], wo[SSM_INNER:], _row(mlp_norm[1]), wu_all, wd_all, fg, batch, seq)
    return h.reshape(batch, seq, D_MODEL)
```
